```python
import jax, jax.numpy as jnp
from jax import lax
import numpy as np

D_MODEL = 1024
BATCH = 8
SEQ = 4096
DEPTH = 1

GRID_W = 64
CTX_LEN = 256
HEAD_DIM = 64
A_HEADS = 8
A_KV_HEADS = 2
A_WINDOW = 128
A_BLOCK = 128
B_HEADS = 8
NA_MAX_KH = 8
NA_KW = 16
NA_ROW_BLOCK = 2
A_WIDTH = A_HEADS * HEAD_DIM
A_KV_WIDTH = A_KV_HEADS * HEAD_DIM
B_WIDTH = B_HEADS * HEAD_DIM
MIX_WIDTH = A_WIDTH + B_WIDTH
IN_COLS = A_WIDTH + 2 * A_KV_WIDTH + 3 * B_WIDTH
FFN_HIDDEN = -(-8 * D_MODEL // (3 * 256)) * 256
N_MOD = 6
ROPE_BASE = 10000.0
EPS = 1e-6

kernel_name = "hybrid_window_neighbourhood_dit_block"


def _rms_norm(x, g):
    xf = x.astype(jnp.float32)
    y = xf * lax.rsqrt(jnp.mean(xf * xf, axis=-1, keepdims=True) + EPS)
    return (y * g.astype(jnp.float32)).astype(x.dtype)


def _modulate(h, shift, scale):
    return h * (1 + scale) + shift


def _axial_rope(x, rows, cols):
    half = HEAD_DIM // 2
    quarter = half // 2
    inv_freq = 1.0 / (ROPE_BASE ** (jnp.arange(quarter, dtype=jnp.float32) / quarter))

    def rot(xh, pos):
        ang = pos.astype(jnp.float32)[:, None] * inv_freq[None, :]
        cos = jnp.cos(ang)[None, :, None, :]
        sin = jnp.sin(ang)[None, :, None, :]
        x1 = xh[..., :quarter].astype(jnp.float32)
        x2 = xh[..., quarter:].astype(jnp.float32)
        return jnp.concatenate([x1 * cos - x2 * sin, x1 * sin + x2 * cos], axis=-1)

    out = jnp.concatenate([rot(x[..., :half], rows), rot(x[..., half:], cols)], axis=-1)
    return out.astype(x.dtype)


def _in_proj(h, w_in):
    B, L, _ = h.shape
    p = h @ w_in
    offs = np.cumsum([A_WIDTH, A_KV_WIDTH, A_KV_WIDTH, B_WIDTH, B_WIDTH])
    q_a, k_a, v_a, q_b, k_b, v_b = jnp.split(p, offs, axis=-1)
    hd = lambda t, n: t.reshape(B, L, n, HEAD_DIM)
    return (hd(q_a, A_HEADS), hd(k_a, A_KV_HEADS), hd(v_a, A_KV_HEADS),
            hd(q_b, B_HEADS), hd(k_b, B_HEADS), hd(v_b, B_HEADS))


def _band_mask(S):
    nb = S // A_BLOCK
    span = A_BLOCK + 2 * A_WINDOW
    qpos = np.arange(nb)[:, None, None] * A_BLOCK + np.arange(A_BLOCK)[None, :, None]
    kpos = np.arange(nb)[:, None, None] * A_BLOCK - A_WINDOW + np.arange(span)[None, None, :]
    return (np.abs(kpos - qpos) <= A_WINDOW) & (kpos >= 0) & (kpos < S)


def _window_gqa_latent(q, k, v, k_ctx, v_ctx, sink):
    B, S = q.shape[:2]
    nb = S // A_BLOCK
    G = A_HEADS // A_KV_HEADS
    n_side = A_WINDOW // A_BLOCK
    scale = HEAD_DIM ** -0.5
    pad = ((0, 0), (A_WINDOW, A_WINDOW), (0, 0), (0, 0))
    kp = jnp.pad(k, pad).reshape(B, nb + 2 * n_side, A_BLOCK, A_KV_HEADS, HEAD_DIM)
    vp = jnp.pad(v, pad).reshape(B, nb + 2 * n_side, A_BLOCK, A_KV_HEADS, HEAD_DIM)
    kb = jnp.concatenate([kp[:, o:o + nb] for o in range(2 * n_side + 1)], axis=2)
    vb = jnp.concatenate([vp[:, o:o + nb] for o in range(2 * n_side + 1)], axis=2)
    span = kb.shape[2]
    qb = q.reshape(B, nb, A_BLOCK, A_KV_HEADS, G, HEAD_DIM)
    mask = jnp.asarray(_band_mask(S))[:, None, None]
    s_loc = jnp.einsum('bnqkgd,bnjkd->bnkgqj', qb, kb).astype(jnp.float32) * scale
    s_loc = jnp.where(mask, s_loc, -jnp.inf)
    s_ctx = jnp.einsum('bnqkgd,bjkd->bnkgqj', qb, k_ctx).astype(jnp.float32) * scale
    sink_col = jnp.broadcast_to(sink.astype(jnp.float32).reshape(1, 1, A_KV_HEADS, G, 1, 1),
                                s_loc.shape[:-1] + (1,))
    p = jax.nn.softmax(jnp.concatenate([s_loc, s_ctx, sink_col], axis=-1), axis=-1)
    L = k_ctx.shape[1]
    p_loc = p[..., :span].astype(v.dtype)
    p_ctx = p[..., span:span + L].astype(v.dtype)
    o = (jnp.einsum('bnkgqj,bnjkd->bnqkgd', p_loc, vb)
         + jnp.einsum('bnkgqj,bjkd->bnqkgd', p_ctx, v_ctx))
    return o.reshape(B, S, A_WIDTH)


def _na_pattern(rows):
    kh = min(NA_MAX_KH, rows)
    n_blk = rows // NA_ROW_BLOCK
    n_kr = min(NA_ROW_BLOCK + kh - 1, rows)
    r0 = np.arange(n_blk) * NA_ROW_BLOCK
    row_start = lambda r: np.clip(r - kh // 2, 0, rows - kh)
    k_start = np.minimum(row_start(r0), rows - n_kr)
    key_rows = k_start[:, None] + np.arange(n_kr)[None, :]
    qi = np.arange(NA_ROW_BLOCK * GRID_W)
    q_r = r0[:, None] + qi[None, :] // GRID_W
    q_c = qi % GRID_W
    kj = np.arange(n_kr * GRID_W)
    k_r = key_rows[:, kj // GRID_W]
    k_c = kj % GRID_W
    rs = row_start(q_r)
    cs = np.clip(q_c - NA_KW // 2, 0, GRID_W - NA_KW)
    row_ok = (k_r[:, None, :] >= rs[:, :, None]) & (k_r[:, None, :] < rs[:, :, None] + kh)
    col_ok = (k_c[None, :] >= cs[:, None]) & (k_c[None, :] < cs[:, None] + NA_KW)
    mask = row_ok & col_ok[None]
    dr = np.clip(k_r[:, None, :] - q_r[:, :, None] + NA_MAX_KH - 1, 0, 2 * NA_MAX_KH - 2)
    dc = np.broadcast_to(np.clip(k_c[None, :] - q_c[:, None] + NA_KW - 1, 0, 2 * NA_KW - 2)[None],
                         mask.shape)
    return key_rows, mask, dr, dc


def _neighbourhood_latent(q, k, v, k_ctx, v_ctx, rpb):
    B, S = q.shape[:2]
    rows = S // GRID_W
    key_rows, mask, dr, dc = _na_pattern(rows)
    n_blk, n_kr = key_rows.shape
    scale = HEAD_DIM ** -0.5
    idx = jnp.asarray(key_rows)
    kg = k.reshape(B, rows, GRID_W, B_HEADS, HEAD_DIM)[:, idx].reshape(
        B, n_blk, n_kr * GRID_W, B_HEADS, HEAD_DIM)
    vg = v.reshape(B, rows, GRID_W, B_HEADS, HEAD_DIM)[:, idx].reshape(
        B, n_blk, n_kr * GRID_W, B_HEADS, HEAD_DIM)
    qb = q.reshape(B, n_blk, NA_ROW_BLOCK * GRID_W, B_HEADS, HEAD_DIM)
    bias = jnp.transpose(rpb[:, jnp.asarray(dr), jnp.asarray(dc)], (1, 0, 2, 3)).astype(jnp.float32)
    s_loc = jnp.einsum('bnqhd,bnjhd->bnhqj', qb, kg).astype(jnp.float32) * scale + bias
    s_loc = jnp.where(jnp.asarray(mask)[:, None], s_loc, -jnp.inf)
    s_ctx = jnp.einsum('bnqhd,bjhd->bnhqj', qb, k_ctx).astype(jnp.float32) * scale
    p = jax.nn.softmax(jnp.concatenate([s_loc, s_ctx], axis=-1), axis=-1)
    kb_len = kg.shape[2]
    p_loc = p[..., :kb_len].astype(v.dtype)
    p_ctx = p[..., kb_len:].astype(v.dtype)
    o = (jnp.einsum('bnhqj,bnjhd->bnqhd', p_loc, vg)
         + jnp.einsum('bnhqj,bjhd->bnqhd', p_ctx, v_ctx))
    return o.reshape(B, S, B_WIDTH)


def _ctx_self_attention(q, k, v, n_kv, sink=None):
    B, L, H, _ = q.shape
    G = H // n_kv
    qg = q.reshape(B, L, n_kv, G, HEAD_DIM)
    s = jnp.einsum('blkgd,bjkd->bkglj', qg, k).astype(jnp.float32) * HEAD_DIM ** -0.5
    if sink is not None:
        sink_col = jnp.broadcast_to(sink.astype(jnp.float32).reshape(1, n_kv, G, 1, 1), s.shape[:-1] + (1,))
        s = jnp.concatenate([s, sink_col], axis=-1)
    p = jax.nn.softmax(s, axis=-1)[..., :L].astype(v.dtype)
    o = jnp.einsum('bkglj,bjkd->blkgd', p, v)
    return o.reshape(B, L, H * HEAD_DIM)


def _merge(o_a, o_b, g_a, g_b, w_out):
    return jnp.concatenate([_rms_norm(o_a, g_a), _rms_norm(o_b, g_b)], axis=-1) @ w_out


def _swiglu(h, w_gate, w_up, w_down):
    return (jax.nn.silu(h @ w_gate) * (h @ w_up)) @ w_down


def setup_inputs(seed: int = 0) -> dict:
    key = jax.random.key(seed)
    ks = jax.random.split(key, 24)
    nrm = lambda k, shape: jax.random.normal(k, shape, dtype=jnp.float32)
    gain = lambda k, shape: 1.0 + 0.05 * nrm(k, shape)
    D = D_MODEL
    return {
        "x": nrm(ks[0], (BATCH, SEQ, D)),
        "c": nrm(ks[1], (BATCH, D)),
        "ctx": nrm(ks[2], (BATCH, CTX_LEN, D)),
        "c_ctx": nrm(ks[3], (D,)),
        "w_mod": nrm(ks[4], (DEPTH, D, N_MOD * D)) * (0.5 * D ** -0.5),
        "b_mod": 0.01 * nrm(ks[5], (DEPTH, N_MOD * D)),
        "norm1_g": gain(ks[6], (DEPTH, D)),
        "w_in": nrm(ks[7], (DEPTH, D, IN_COLS)) * D ** -0.5,
        "qn_a": gain(ks[8], (DEPTH, HEAD_DIM)),
        "kn_a": gain(ks[9], (DEPTH, HEAD_DIM)),
        "sink_a": 0.5 * nrm(ks[10], (DEPTH, A_HEADS)),
        "qn_b": gain(ks[11], (DEPTH, HEAD_DIM)),
        "kn_b": gain(ks[12], (DEPTH, HEAD_DIM)),
        "rpb_b": 0.1 * nrm(ks[13], (DEPTH, B_HEADS, 2 * NA_MAX_KH - 1, 2 * NA_KW - 1)),
        "on_a": gain(ks[14], (DEPTH, A_WIDTH)),
        "on_b": gain(ks[15], (DEPTH, B_WIDTH)),
        "w_out": nrm(ks[16], (DEPTH, MIX_WIDTH, D)) * MIX_WIDTH ** -0.5,
        "norm2_g": gain(ks[17], (DEPTH, D)),
        "w_gate": nrm(ks[18], (DEPTH, D, FFN_HIDDEN)) * D ** -0.5,
        "w_up": nrm(ks[19], (DEPTH, D, FFN_HIDDEN)) * D ** -0.5,
        "w_down": nrm(ks[20], (DEPTH, FFN_HIDDEN, D)) * FFN_HIDDEN ** -0.5,
    }


def reference(x, c, ctx, c_ctx, w_mod, b_mod, norm1_g, w_in, qn_a, kn_a, sink_a, qn_b, kn_b,
              rpb_b, on_a, on_b, w_out, norm2_g, w_gate, w_up, w_down):
    S = x.shape[1]
    t = jnp.arange(S)
    row_pos = t // GRID_W
    col_pos = t % GRID_W
    for l in range(DEPTH):
        mod = (jax.nn.silu(c) @ w_mod[l] + b_mod[l])[:, None, :]
        mod_c = (jax.nn.silu(c_ctx) @ w_mod[l] + b_mod[l])[None, None, :]
        sh1, sc1, g1, sh2, sc2, g2 = jnp.split(mod, N_MOD, axis=-1)
        csh1, csc1, cg1, csh2, csc2, cg2 = jnp.split(mod_c, N_MOD, axis=-1)

        h = _modulate(_rms_norm(x, norm1_g[l]), sh1, sc1)
        hc = _modulate(_rms_norm(ctx, norm1_g[l]), csh1, csc1)
        q_a, k_a, v_a, q_b, k_b, v_b = _in_proj(h, w_in[l])
        q_ac, k_ac, v_ac, q_bc, k_bc, v_bc = _in_proj(hc, w_in[l])

        q_a = _axial_rope(_rms_norm(q_a, qn_a[l]), row_pos, col_pos)
        k_a = _axial_rope(_rms_norm(k_a, kn_a[l]), row_pos, col_pos)
        q_ac, k_ac = _rms_norm(q_ac, qn_a[l]), _rms_norm(k_ac, kn_a[l])
        q_b, k_b = _rms_norm(q_b, qn_b[l]), _rms_norm(k_b, kn_b[l])
        q_bc, k_bc = _rms_norm(q_bc, qn_b[l]), _rms_norm(k_bc, kn_b[l])

        o_a = _window_gqa_latent(q_a, k_a, v_a, k_ac, v_ac, sink_a[l])
        o_b = _neighbourhood_latent(q_b, k_b, v_b, k_bc, v_bc, rpb_b[l])
        x_new = x + g1 * _merge(o_a, o_b, on_a[l], on_b[l], w_out[l])
        h2 = _modulate(_rms_norm(x_new, norm2_g[l]), sh2, sc2)
        x_new = x_new + g2 * _swiglu(h2, w_gate[l], w_up[l], w_down[l])

        if l < DEPTH - 1:
            o_ac = _ctx_self_attention(q_ac, k_ac, v_ac, A_KV_HEADS, sink_a[l])
            o_bc = _ctx_self_attention(q_bc, k_bc, v_bc, B_HEADS)
            ctx = ctx + cg1 * _merge(o_ac, o_bc, on_a[l], on_b[l], w_out[l])
            hc2 = _modulate(_rms_norm(ctx, norm2_g[l]), csh2, csc2)
            ctx = ctx + cg2 * _swiglu(hc2, w_gate[l], w_up[l], w_down[l])
        x = x_new
    return x
```

```python
import functools

import numpy as np
import jax
import jax.numpy as jnp
from jax import lax
from jax.experimental import pallas as pl
from jax.experimental.pallas import tpu as pltpu

GRID_W = 64
HEAD_DIM = 64
A_HEADS = 8
A_KV_HEADS = 2
A_WINDOW = 128
B_HEADS = 8
NA_KH = 8
NA_KW = 16
NA_ROW_BLOCK = 2
N_MOD = 6
ROPE_BASE = 10000.0
EPS = 1e-6

A_WIDTH = A_HEADS * HEAD_DIM
A_KV_WIDTH = A_KV_HEADS * HEAD_DIM
B_WIDTH = B_HEADS * HEAD_DIM
QBLK = NA_ROW_BLOCK * GRID_W
A_SPAN = QBLK + 2 * A_WINDOW
B_KROWS = 10
B_SPAN = B_KROWS * GRID_W
N_PATTERNS = 5
NEG = -1e30

LANES = 128
MXU_TILE = 256
VMEM_LIMIT = 56 * 1024 * 1024

F32 = jnp.float32
BF16 = jnp.bfloat16


def _dot(a, b):
    return jnp.dot(a, b, preferred_element_type=F32)


def _dot_nt(a, b):
    return lax.dot_general(a, b, (((1,), (1,)), ((), ())), preferred_element_type=F32)


def _silu(x):
    return x / (1.0 + jnp.exp(-x))


def _mod_kernel(c_ref, w_ref, b_ref, o_ref):
    a = _silu(c_ref[...])
    o_ref[...] = jnp.dot(a, w_ref[...], precision=lax.Precision.HIGHEST,
                         preferred_element_type=F32) + b_ref[...]


def _mod_call(cc, w_mod, b_mod):
    rows, d = cc.shape
    n = w_mod.shape[1]
    return pl.pallas_call(
        _mod_kernel,
        grid=(n // d,),
        in_specs=[pl.BlockSpec((rows, d), lambda j: (0, 0)),
                  pl.BlockSpec((d, d), lambda j: (0, j)),
                  pl.BlockSpec((1, d), lambda j: (0, j))],
        out_specs=pl.BlockSpec((rows, d), lambda j: (0, j)),
        out_shape=jax.ShapeDtypeStruct((rows, n), F32),
        name="mod",
    )(cc, w_mod, b_mod)


def _rope_kernel(invf_ref, cos_ref, sin_ref, *, tm):
    t = pl.program_id(0) * tm + lax.broadcasted_iota(jnp.int32, (tm, LANES), 0)
    lane = lax.broadcasted_iota(jnp.int32, (tm, LANES), 1)
    l64 = lane & (HEAD_DIM - 1)
    pos = jnp.where(l64 < HEAD_DIM // 2, t // GRID_W, t % GRID_W).astype(F32)
    ang = pos * invf_ref[...]
    first = (lane & (HEAD_DIM // 2 - 1)) < HEAD_DIM // 4
    cos_ref[...] = jnp.cos(ang)
    sin_ref[...] = jnp.where(first, -jnp.sin(ang), jnp.sin(ang))


def _rope_call(seq, tm):
    quarter = HEAD_DIM // 4
    inv = (1.0 / (np.float32(ROPE_BASE) ** (np.arange(quarter, dtype=np.float32) / quarter))).astype(np.float32)
    invf = jnp.asarray(np.tile(inv, LANES // quarter)[None, :])
    return pl.pallas_call(
        functools.partial(_rope_kernel, tm=tm),
        grid=(seq // tm,),
        in_specs=[pl.BlockSpec((1, LANES), lambda i: (0, 0))],
        out_specs=[pl.BlockSpec((tm, LANES), lambda i: (i, 0))] * 2,
        out_shape=[jax.ShapeDtypeStruct((seq, LANES), F32)] * 2,
        name="rope_tables",
    )(invf)


def _b_patterns(rows):
    n_blk = rows // NA_ROW_BLOCK
    blocks = [0, 1, 2, n_blk - 2, n_blk - 1]
    return [(NA_ROW_BLOCK * i, _b_key_start(i, rows)) for i in blocks]


def _b_key_start(i, rows):
    return min(max(NA_ROW_BLOCK * i - NA_KH // 2, 0), rows - B_KROWS)


def _bias_kernel(rpb_ref, o_ref, *, patterns, rows):
    h = pl.program_id(0)
    n_dr, n_dc = 2 * NA_KH - 1, 2 * NA_KW - 1
    row = lax.broadcasted_iota(jnp.int32, (GRID_W, LANES), 0)
    lane = lax.broadcasted_iota(jnp.int32, (GRID_W, LANES), 1)
    kc = lane & (GRID_W - 1)
    dc = jnp.clip(kc - row + NA_KW - 1, 0, n_dc - 1)
    cs = jnp.clip(row - NA_KW // 2, 0, GRID_W - NA_KW)
    col_ok = (kc >= cs) & (kc < cs + NA_KW)
    lo = lane < GRID_W
    base = h * (n_dr * n_dc)
    per_dr = []
    for dr in range(n_dr):
        m = jnp.zeros((GRID_W, LANES), F32)
        for d in range(n_dc):
            m = jnp.where(dc == d, rpb_ref[base + dr * n_dc + d], m)
        per_dr.append(m)
    neg = jnp.full((GRID_W, LANES), NEG, F32)
    for p, (r0, ks) in enumerate(patterns):
        for qr in range(NA_ROW_BLOCK):
            q_row = r0 + qr
            rs = min(max(q_row - NA_KH // 2, 0), rows - NA_KH)
            for kp in range(B_KROWS // 2):
                k_rows = (ks + 2 * kp, ks + 2 * kp + 1)
                ok = [rs <= k < rs + NA_KH for k in k_rows]
                drs = [min(max(k - q_row + NA_KH - 1, 0), n_dr - 1) for k in k_rows]
                if not any(ok):
                    piece = neg
                else:
                    vals = jnp.where(lo, per_dr[drs[0]], per_dr[drs[1]])
                    mask = col_ok
                    if not ok[1]:
                        mask = mask & lo
                    if not ok[0]:
                        mask = mask & jnp.logical_not(lo)
                    piece = jnp.where(mask, vals, neg)
                o_ref[p, 0, qr * GRID_W:(qr + 1) * GRID_W, kp * LANES:(kp + 1) * LANES] = piece


def _bias_call(rpb, rows):
    patterns = _b_patterns(rows)
    heads = rpb.shape[0]
    return pl.pallas_call(
        functools.partial(_bias_kernel, patterns=patterns, rows=rows),
        grid=(heads,),
        in_specs=[pl.BlockSpec(memory_space=pltpu.SMEM)],
        out_specs=pl.BlockSpec((N_PATTERNS, 1, QBLK, B_SPAN), lambda h: (0, h, 0, 0)),
        out_shape=jax.ShapeDtypeStruct((N_PATTERNS, heads, QBLK, B_SPAN), F32),
        name="na_bias",
    )(rpb.reshape(-1))


def _group_rms(p, ones_ref):
    sq = (p * p).astype(BF16)
    n = p.shape[1]
    if n <= MXU_TILE:
        ssq = _dot(sq, ones_ref[:n, :n])
    else:
        ssq = jnp.concatenate(
            [_dot(sq[:, c:c + MXU_TILE], ones_ref[...]) for c in range(0, n, MXU_TILE)], axis=1)
    return lax.rsqrt(ssq * (1.0 / HEAD_DIM) + EPS)


def _rope(x, cos, sin):
    lane = lax.broadcasted_iota(jnp.int32, (x.shape[0], LANES), 1)
    first = (lane & (HEAD_DIM // 2 - 1)) < HEAD_DIM // 4
    outs = []
    for c in range(0, x.shape[1], LANES):
        xt = x[:, c:c + LANES]
        sw = jnp.where(first, pltpu.roll(xt, LANES - HEAD_DIM // 4, 1), pltpu.roll(xt, HEAD_DIM // 4, 1))
        outs.append(xt * cos + sw * sin)
    return outs[0] if len(outs) == 1 else jnp.concatenate(outs, axis=1)


def _inproj_kernel(x_ref, sh_ref, sc_ref, g_ref, w_ref, gain_ref, ones_ref, *rest, groups, use_rope):
    if use_rope:
        cos_ref, sin_ref = rest[:2]
        out_refs = rest[2:]
        cos, sin = cos_ref[...], sin_ref[...]
    else:
        out_refs = rest
    x = x_ref[0]
    ms = jnp.mean(x * x, axis=-1, keepdims=True)
    geff = g_ref[...] * (1.0 + sc_ref[0])
    h = ((x * lax.rsqrt(ms + EPS)) * geff + sh_ref[0]).astype(BF16)
    for c0, width, subs in groups:
        p = _dot(h, w_ref[:, c0:c0 + width])
        for s0, sw, norm, rope, oi in subs:
            y = p[:, s0:s0 + sw]
            if norm:
                y = y * _group_rms(y, ones_ref) * gain_ref[:, c0 + s0:c0 + s0 + sw]
            if rope:
                y = _rope(y, cos, sin)
            out_refs[oi][0] = y.astype(BF16)


def _inproj_call(x, mod3, mod_batched, g, w, gain, ones, rope_tabs, groups, out_widths, tm):
    b, s, d = x.shape
    n = w.shape[1]
    use_rope = rope_tabs is not None
    mod_idx = (lambda bi, i, j: (bi, 0, j)) if mod_batched else (lambda bi, i, j: (0, 0, j))
    in_specs = [
        pl.BlockSpec((1, tm, d), lambda bi, i: (bi, i, 0)),
        pl.BlockSpec((1, 1, d), lambda bi, i: mod_idx(bi, i, 0)),
        pl.BlockSpec((1, 1, d), lambda bi, i: mod_idx(bi, i, 1)),
        pl.BlockSpec((1, d), lambda bi, i: (0, 0)),
        pl.BlockSpec((d, n), lambda bi, i: (0, 0)),
        pl.BlockSpec((1, n), lambda bi, i: (0, 0)),
        pl.BlockSpec((MXU_TILE, MXU_TILE), lambda bi, i: (0, 0)),
    ]
    args = [x, mod3, mod3, g, w, gain, ones]
    if use_rope:
        in_specs += [pl.BlockSpec((tm, LANES), lambda bi, i: (i, 0))] * 2
        args += list(rope_tabs)
    return pl.pallas_call(
        functools.partial(_inproj_kernel, groups=groups, use_rope=use_rope),
        grid=(b, s // tm),
        in_specs=in_specs,
        out_specs=[pl.BlockSpec((1, tm, wd), lambda bi, i: (bi, i, 0)) for wd in out_widths],
        out_shape=[jax.ShapeDtypeStruct((b, s, wd), BF16) for wd in out_widths],
        compiler_params=pltpu.CompilerParams(dimension_semantics=("parallel", "arbitrary"),
                                             vmem_limit_bytes=VMEM_LIMIT),
        name="in_proj_rope" if use_rope else "in_proj_ctx",
    )(*args)


def _softmax_pv(s_parts, v_parts, extra_logit):
    m = functools.reduce(jnp.maximum, [jnp.max(s, axis=-1, keepdims=True) for s in s_parts])
    if extra_logit is not None:
        m = jnp.maximum(m, extra_logit)
    ps = [jnp.exp(s - m) for s in s_parts]
    l = functools.reduce(lambda a, c: a + c, [jnp.sum(p, axis=-1, keepdims=True) for p in ps])
    if extra_logit is not None:
        l = l + jnp.exp(extra_logit - m)
    o = functools.reduce(lambda a, c: a + c, [_dot(p.astype(BF16), v) for p, v in zip(ps, v_parts)])
    return o * (1.0 / l)


def _group_norm(o, g):
    ms = jnp.mean(o * o, axis=-1, keepdims=True)
    return ((o * lax.rsqrt(ms + EPS)) * g).astype(BF16)


def _attn_kernel(sink_ref, qa_ref, qb_ref, ka_ref, va_ref, kb_ref, vb_ref, kac_ref, vac_ref,
                 kbc_ref, vbc_ref, bias_ref, ga_ref, gb_ref, o_ref, *, seq):
    i = pl.program_id(1)
    rows = seq // GRID_W
    lane = lax.broadcasted_iota(jnp.int32, (QBLK, LANES), 1)
    lo = lane < HEAD_DIM
    m_lo = lo.astype(F32).astype(BF16)
    m_hi = jnp.logical_not(lo).astype(F32).astype(BF16)
    half_masks = (m_lo, m_hi)

    start_a = pl.multiple_of(jnp.clip((i - 1) * QBLK, 0, seq - A_SPAN), QBLK)
    d = (lax.broadcasted_iota(jnp.int32, (QBLK, A_SPAN), 1)
         - lax.broadcasted_iota(jnp.int32, (QBLK, A_SPAN), 0) + (start_a - i * QBLK))
    amask = jnp.where((d <= A_WINDOW) & (d >= -A_WINDOW), 0.0, NEG).astype(F32)
    k_loc = ka_ref[0, pl.ds(start_a, A_SPAN), :]
    v_loc = va_ref[0, pl.ds(start_a, A_SPAN), :]
    k_ctx = kac_ref[0]
    v_ctx = vac_ref[0]
    group = A_HEADS // A_KV_HEADS
    oa = []
    for pair in range(group):
        qp = qa_ref[0, :, pair * LANES:(pair + 1) * LANES]
        halves = []
        for half in range(A_KV_HEADS):
            qm = qp * half_masks[half]
            s_loc = _dot_nt(qm, k_loc) + amask
            s_ctx = _dot_nt(qm, k_ctx)
            sink = sink_ref[half * group + pair]
            halves.append(_softmax_pv([s_loc, s_ctx], [v_loc, v_ctx], sink))
        oa.append(jnp.where(lo, halves[0], halves[1]))
    o_ref[0, :, :A_WIDTH] = _group_norm(jnp.concatenate(oa, axis=1), ga_ref[...])

    start_b = pl.multiple_of(
        jnp.clip(NA_ROW_BLOCK * i - NA_KH // 2, 0, rows - B_KROWS) * GRID_W, GRID_W)
    ob = []
    for pair in range(B_HEADS // 2):
        cols = slice(pair * LANES, (pair + 1) * LANES)
        qp = qb_ref[0, :, cols]
        k_loc = kb_ref[0, pl.ds(start_b, B_SPAN), cols]
        v_loc = vb_ref[0, pl.ds(start_b, B_SPAN), cols]
        k_ctx = kbc_ref[0, :, cols]
        v_ctx = vbc_ref[0, :, cols]
        halves = []
        for half in range(2):
            qm = qp * half_masks[half]
            s_loc = _dot_nt(qm, k_loc) + bias_ref[0, 2 * pair + half]
            s_ctx = _dot_nt(qm, k_ctx)
            halves.append(_softmax_pv([s_loc, s_ctx], [v_loc, v_ctx], None))
        ob.append(jnp.where(lo, halves[0], halves[1]))
    o_ref[0, :, A_WIDTH:] = _group_norm(jnp.concatenate(ob, axis=1), gb_ref[...])


def _attn_call(sink, qa, qb, ka, va, kb, vb, kac, vac, kbc, vbc, bias, ga, gb):
    b, s, _ = qa.shape
    l = kac.shape[1]
    n_blk = s // QBLK

    def pat(i):
        return jnp.where(i < 2, i, jnp.where(i >= n_blk - 2, i - (n_blk - N_PATTERNS), 2))

    full = lambda wd, ln: pl.BlockSpec((1, ln, wd), lambda bi, i: (bi, 0, 0))
    blk = lambda wd: pl.BlockSpec((1, QBLK, wd), lambda bi, i: (bi, i, 0))
    return pl.pallas_call(
        functools.partial(_attn_kernel, seq=s),
        grid=(b, n_blk),
        in_specs=[
            pl.BlockSpec(memory_space=pltpu.SMEM),
            blk(A_WIDTH), blk(B_WIDTH),
            full(A_KV_WIDTH, s), full(A_KV_WIDTH, s), full(B_WIDTH, s), full(B_WIDTH, s),
            full(A_KV_WIDTH, l), full(A_KV_WIDTH, l), full(B_WIDTH, l), full(B_WIDTH, l),
            pl.BlockSpec((1, B_HEADS, QBLK, B_SPAN), lambda bi, i: (pat(i), 0, 0, 0)),
            pl.BlockSpec((1, A_WIDTH), lambda bi, i: (0, 0)),
            pl.BlockSpec((1, B_WIDTH), lambda bi, i: (0, 0)),
        ],
        out_specs=pl.BlockSpec((1, QBLK, A_WIDTH + B_WIDTH), lambda bi, i: (bi, i, 0)),
        out_shape=jax.ShapeDtypeStruct((b, s, A_WIDTH + B_WIDTH), BF16),
        compiler_params=pltpu.CompilerParams(dimension_semantics=("parallel", "arbitrary"),
                                             vmem_limit_bytes=VMEM_LIMIT),
        name="attn",
    )(sink, qa, qb, ka, va, kb, vb, kac, vac, kbc, vbc, bias, ga, gb)


def _ffn_kernel(x_ref, o_ref, g1_ref, sh2_ref, sc2_ref, g2_ref, n2_ref, wo_ref, wg_ref, wu_ref, wd_ref,
                out_ref):
    x1 = x_ref[0] + g1_ref[0] * _dot(o_ref[0], wo_ref[...])
    ms = jnp.mean(x1 * x1, axis=-1, keepdims=True)
    geff = n2_ref[...] * (1.0 + sc2_ref[0])
    h2 = ((x1 * lax.rsqrt(ms + EPS)) * geff + sh2_ref[0]).astype(BF16)
    gate = _dot(h2, wg_ref[...])
    up = _dot(h2, wu_ref[...])
    act = (_silu(gate) * up).astype(BF16)
    out_ref[0] = x1 + g2_ref[0] * _dot(act, wd_ref[...])


def _ffn_call(x, o, mod3, n2, wo, wg, wu, wd, tm):
    b, s, d = x.shape
    hid = wg.shape[1]
    modspec = lambda j: pl.BlockSpec((1, 1, d), lambda bi, i: (bi, 0, j))
    const = lambda shape: pl.BlockSpec(shape, lambda bi, i: (0, 0), pipeline_mode=pl.Buffered(1))
    return pl.pallas_call(
        _ffn_kernel,
        grid=(b, s // tm),
        in_specs=[
            pl.BlockSpec((1, tm, d), lambda bi, i: (bi, i, 0)),
            pl.BlockSpec((1, tm, o.shape[2]), lambda bi, i: (bi, i, 0)),
            modspec(2), modspec(3), modspec(4), modspec(5),
            pl.BlockSpec((1, d), lambda bi, i: (0, 0)),
            const((o.shape[2], d)), const((d, hid)), const((d, hid)), const((hid, d)),
        ],
        out_specs=pl.BlockSpec((1, tm, d), lambda bi, i: (bi, i, 0)),
        out_shape=jax.ShapeDtypeStruct((b, s, d), F32),
        compiler_params=pltpu.CompilerParams(dimension_semantics=("parallel", "arbitrary"),
                                             vmem_limit_bytes=VMEM_LIMIT),
        name="out_ffn",
    )(x, o, mod3, mod3, mod3, mod3, n2, wo, wg, wu, wd)


def _layer(x, ctx, mod, norm1_g, w_in, qn_a, kn_a, sink_a, qn_b, kn_b, rpb_b, on_a, on_b, w_out,
           norm2_g, w_gate, w_up, w_down, rope_tabs):
    b, s, d = x.shape
    rows = s // GRID_W
    group = A_HEADS // A_KV_HEADS
    perm_heads = np.array([h for t in range(group) for h in (t, t + group)])
    perm_cols = (perm_heads[:, None] * HEAD_DIM + np.arange(HEAD_DIM)[None, :]).reshape(-1)

    o_qa, o_ka, o_va, o_qb, o_kb, o_vb = np.cumsum(
        [0, A_WIDTH, A_KV_WIDTH, A_KV_WIDTH, B_WIDTH, B_WIDTH])
    w_in_l = jnp.concatenate([w_in[:, :A_WIDTH][:, perm_cols], w_in[:, A_WIDTH:]], axis=1).astype(BF16)
    w_in_c = jnp.concatenate([w_in[:, o_ka:o_qb], w_in[:, o_kb:]], axis=1).astype(BF16)
    scale = HEAD_DIM ** -0.5
    ones_v = lambda n: jnp.ones((n,), F32)
    gain_l = jnp.concatenate([jnp.tile(qn_a, A_HEADS) * scale, jnp.tile(kn_a, A_KV_HEADS), ones_v(A_KV_WIDTH),
                              jnp.tile(qn_b, B_HEADS) * scale, jnp.tile(kn_b, B_HEADS), ones_v(B_WIDTH)])[None]
    gain_c = jnp.concatenate([jnp.tile(kn_a, A_KV_HEADS), ones_v(A_KV_WIDTH),
                              jnp.tile(kn_b, B_HEADS), ones_v(B_WIDTH)])[None]
    ones_bd = jnp.asarray(np.kron(np.eye(MXU_TILE // HEAD_DIM), np.ones((HEAD_DIM, HEAD_DIM))), BF16)

    mod3 = mod[:b].reshape(b, 1, -1)
    mod3_c = mod[b:b + 1].reshape(1, 1, -1)
    g1 = norm1_g[None]

    groups_l = [
        (o_qa, A_WIDTH, [(0, A_WIDTH, True, True, 0)]),
        (o_ka, 2 * A_KV_WIDTH, [(0, A_KV_WIDTH, True, True, 1), (A_KV_WIDTH, A_KV_WIDTH, False, False, 2)]),
        (o_qb, B_WIDTH, [(0, B_WIDTH, True, False, 3)]),
        (o_kb, B_WIDTH, [(0, B_WIDTH, True, False, 4)]),
        (o_vb, B_WIDTH, [(0, B_WIDTH, False, False, 5)]),
    ]
    qa, ka, va, qb, kb, vb = _inproj_call(
        x, mod3, True, g1, w_in_l, gain_l, ones_bd, rope_tabs, groups_l,
        [A_WIDTH, A_KV_WIDTH, A_KV_WIDTH, B_WIDTH, B_WIDTH, B_WIDTH], tm=512)
    groups_c = [
        (0, 2 * A_KV_WIDTH, [(0, A_KV_WIDTH, True, False, 0), (A_KV_WIDTH, A_KV_WIDTH, False, False, 1)]),
        (2 * A_KV_WIDTH, B_WIDTH, [(0, B_WIDTH, True, False, 2)]),
        (2 * A_KV_WIDTH + B_WIDTH, B_WIDTH, [(0, B_WIDTH, False, False, 3)]),
    ]
    kac, vac, kbc, vbc = _inproj_call(
        ctx, mod3_c, False, g1, w_in_c, gain_c, ones_bd, None, groups_c,
        [A_KV_WIDTH, A_KV_WIDTH, B_WIDTH, B_WIDTH], tm=ctx.shape[1])

    bias = _bias_call(rpb_b, rows)
    o = _attn_call(sink_a, qa, qb, ka, va, kb, vb, kac, vac, kbc, vbc, bias,
                   on_a[perm_cols][None], on_b[None])
    w_out_p = jnp.concatenate([w_out[:A_WIDTH][perm_cols], w_out[A_WIDTH:]], axis=0).astype(BF16)
    return _ffn_call(x, o, mod3, norm2_g[None], w_out_p, w_gate.astype(BF16), w_up.astype(BF16),
                     w_down.astype(BF16), tm=256)


def kernel(x, c, ctx, c_ctx, w_mod, b_mod, norm1_g, w_in, qn_a, kn_a, sink_a, qn_b, kn_b, rpb_b, on_a, on_b,
           w_out, norm2_g, w_gate, w_up, w_down):
    b, s, d = x.shape
    depth = w_mod.shape[0]
    assert depth == 1, "the context stream update is only needed when a later layer reads it"
    rope_tabs = _rope_call(s, tm=512)
    pad = jnp.zeros((16 - b - 1, d), F32)
    cc = jnp.concatenate([c, c_ctx[None], pad], axis=0)
    l = 0
    mod = _mod_call(cc, w_mod[l], b_mod[l][None])
    return _layer(x, ctx, mod, norm1_g[l], w_in[l], qn_a[l], kn_a[l], sink_a[l], qn_b[l], kn_b[l], rpb_b[l],
                  on_a[l], on_b[l], w_out[l], norm2_g[l], w_gate[l], w_up[l], w_down[l], rope_tabs)
```

```python
import functools

import numpy as np
import jax
import jax.numpy as jnp
from jax import lax
from jax.experimental import pallas as pl
from jax.experimental.pallas import tpu as pltpu

GRID_W = 64
HEAD_DIM = 64
A_HEADS = 8
A_KV_HEADS = 2
A_WINDOW = 128
B_HEADS = 8
NA_KH = 8
NA_KW = 16
NA_ROW_BLOCK = 2
N_MOD = 6
ROPE_BASE = 10000.0
EPS = 1e-6

A_WIDTH = A_HEADS * HEAD_DIM
A_KV_WIDTH = A_KV_HEADS * HEAD_DIM
B_WIDTH = B_HEADS * HEAD_DIM
QBLK = NA_ROW_BLOCK * GRID_W
A_SPAN = QBLK + 2 * A_WINDOW
B_KROWS = 10
B_SPAN = B_KROWS * GRID_W
N_PATTERNS = 5
NEG = -1e30
LOG2E = 1.4426950408889634

LANES = 128
MXU_TILE = 256
VMEM_LIMIT = 56 * 1024 * 1024

F32 = jnp.float32
BF16 = jnp.bfloat16


def _dot(a, b):
    return jnp.dot(a, b, preferred_element_type=F32)


def _dot_nt(a, b):
    return lax.dot_general(a, b, (((1,), (1,)), ((), ())), preferred_element_type=F32)


def _silu(x):
    return x / (1.0 + jnp.exp(-x))


def _mod_kernel(c_ref, w_ref, b_ref, o_ref):
    a = _silu(c_ref[...])
    o_ref[...] = jnp.dot(a, w_ref[...], precision=lax.Precision.HIGHEST,
                         preferred_element_type=F32) + b_ref[...]


def _mod_call(cc, w_mod, b_mod):
    rows, d = cc.shape
    n = w_mod.shape[1]
    return pl.pallas_call(
        _mod_kernel,
        grid=(n // d,),
        in_specs=[pl.BlockSpec((rows, d), lambda j: (0, 0)),
                  pl.BlockSpec((d, d), lambda j: (0, j)),
                  pl.BlockSpec((1, d), lambda j: (0, j))],
        out_specs=pl.BlockSpec((rows, d), lambda j: (0, j)),
        out_shape=jax.ShapeDtypeStruct((rows, n), F32),
        name="mod",
    )(cc, w_mod, b_mod)


def _rope_kernel(invf_ref, cos_ref, sin_ref, *, tm):
    t = pl.program_id(0) * tm + lax.broadcasted_iota(jnp.int32, (tm, LANES), 0)
    lane = lax.broadcasted_iota(jnp.int32, (tm, LANES), 1)
    l64 = lane & (HEAD_DIM - 1)
    pos = jnp.where(l64 < HEAD_DIM // 2, t // GRID_W, t % GRID_W).astype(F32)
    ang = pos * invf_ref[...]
    first = (lane & (HEAD_DIM // 2 - 1)) < HEAD_DIM // 4
    cos_ref[...] = jnp.cos(ang)
    sin_ref[...] = jnp.where(first, -jnp.sin(ang), jnp.sin(ang))


def _rope_call(seq, tm):
    quarter = HEAD_DIM // 4
    inv = (1.0 / (np.float32(ROPE_BASE) ** (np.arange(quarter, dtype=np.float32) / quarter))).astype(np.float32)
    invf = jnp.asarray(np.tile(inv, LANES // quarter)[None, :])
    return pl.pallas_call(
        functools.partial(_rope_kernel, tm=tm),
        grid=(seq // tm,),
        in_specs=[pl.BlockSpec((1, LANES), lambda i: (0, 0))],
        out_specs=[pl.BlockSpec((tm, LANES), lambda i: (i, 0))] * 2,
        out_shape=[jax.ShapeDtypeStruct((seq, LANES), F32)] * 2,
        name="rope_tables",
    )(invf)


def _b_patterns(rows):
    n_blk = rows // NA_ROW_BLOCK
    blocks = [0, 1, 2, n_blk - 2, n_blk - 1]
    return [(NA_ROW_BLOCK * i, _b_key_start(i, rows)) for i in blocks]


def _b_key_start(i, rows):
    return min(max(NA_ROW_BLOCK * i - NA_KH // 2, 0), rows - B_KROWS)


def _bias_kernel(rpb_ref, o_ref, *, patterns, rows):
    h = pl.program_id(0)
    n_dr, n_dc = 2 * NA_KH - 1, 2 * NA_KW - 1
    row = lax.broadcasted_iota(jnp.int32, (GRID_W, LANES), 0)
    lane = lax.broadcasted_iota(jnp.int32, (GRID_W, LANES), 1)
    kc = lane & (GRID_W - 1)
    dc = jnp.clip(kc - row + NA_KW - 1, 0, n_dc - 1)
    cs = jnp.clip(row - NA_KW // 2, 0, GRID_W - NA_KW)
    col_ok = (kc >= cs) & (kc < cs + NA_KW)
    lo = lane < GRID_W
    base = h * (n_dr * n_dc)
    per_dr = []
    for dr in range(n_dr):
        m = jnp.zeros((GRID_W, LANES), F32)
        for d in range(n_dc):
            m = jnp.where(dc == d, rpb_ref[base + dr * n_dc + d] * LOG2E, m)
        per_dr.append(m)
    neg = jnp.full((GRID_W, LANES), NEG, F32)
    for p, (r0, ks) in enumerate(patterns):
        for qr in range(NA_ROW_BLOCK):
            q_row = r0 + qr
            rs = min(max(q_row - NA_KH // 2, 0), rows - NA_KH)
            for kp in range(B_KROWS // 2):
                k_rows = (ks + 2 * kp, ks + 2 * kp + 1)
                ok = [rs <= k < rs + NA_KH for k in k_rows]
                drs = [min(max(k - q_row + NA_KH - 1, 0), n_dr - 1) for k in k_rows]
                if not any(ok):
                    piece = neg
                else:
                    vals = jnp.where(lo, per_dr[drs[0]], per_dr[drs[1]])
                    mask = col_ok
                    if not ok[1]:
                        mask = mask & lo
                    if not ok[0]:
                        mask = mask & jnp.logical_not(lo)
                    piece = jnp.where(mask, vals, neg)
                o_ref[p, 0, qr * GRID_W:(qr + 1) * GRID_W, kp * LANES:(kp + 1) * LANES] = piece


def _bias_call(rpb, rows):
    patterns = _b_patterns(rows)
    heads = rpb.shape[0]
    return pl.pallas_call(
        functools.partial(_bias_kernel, patterns=patterns, rows=rows),
        grid=(heads,),
        in_specs=[pl.BlockSpec(memory_space=pltpu.SMEM)],
        out_specs=pl.BlockSpec((N_PATTERNS, 1, QBLK, B_SPAN), lambda h: (0, h, 0, 0)),
        out_shape=jax.ShapeDtypeStruct((N_PATTERNS, heads, QBLK, B_SPAN), F32),
        name="na_bias",
    )(rpb.reshape(-1))


def _group_rms(p, ones_ref):
    sq = (p * p).astype(BF16)
    n = p.shape[1]
    if n <= MXU_TILE:
        ssq = _dot(sq, ones_ref[:n, :n])
    else:
        ssq = jnp.concatenate(
            [_dot(sq[:, c:c + MXU_TILE], ones_ref[...]) for c in range(0, n, MXU_TILE)], axis=1)
    return lax.rsqrt(ssq * (1.0 / HEAD_DIM) + EPS)


def _rope(x, cos, sin):
    lane = lax.broadcasted_iota(jnp.int32, (x.shape[0], LANES), 1)
    first = (lane & (HEAD_DIM // 2 - 1)) < HEAD_DIM // 4
    outs = []
    for c in range(0, x.shape[1], LANES):
        xt = x[:, c:c + LANES]
        sw = jnp.where(first, pltpu.roll(xt, LANES - HEAD_DIM // 4, 1), pltpu.roll(xt, HEAD_DIM // 4, 1))
        outs.append(xt * cos + sw * sin)
    return outs[0] if len(outs) == 1 else jnp.concatenate(outs, axis=1)


def _inproj_kernel(x_ref, sh_ref, sc_ref, g_ref, w_ref, gain_ref, ones_ref, *rest, groups, use_rope):
    if use_rope:
        cos_ref, sin_ref = rest[:2]
        out_refs = rest[2:]
        cos, sin = cos_ref[...], sin_ref[...]
    else:
        out_refs = rest
    x = x_ref[0]
    ms = jnp.mean(x * x, axis=-1, keepdims=True)
    geff = g_ref[...] * (1.0 + sc_ref[0])
    h = ((x * lax.rsqrt(ms + EPS)) * geff + sh_ref[0]).astype(BF16)
    for c0, width, subs in groups:
        p = _dot(h, w_ref[:, c0:c0 + width])
        for s0, sw, norm, rope, oi in subs:
            y = p[:, s0:s0 + sw]
            if norm:
                y = y * _group_rms(y, ones_ref) * gain_ref[:, c0 + s0:c0 + s0 + sw]
            if rope:
                y = _rope(y, cos, sin)
            out_refs[oi][0] = y.astype(BF16)


def _inproj_call(x, mod3, mod_batched, g, w, gain, ones, rope_tabs, groups, out_widths, tm):
    b, s, d = x.shape
    n = w.shape[1]
    use_rope = rope_tabs is not None
    mod_idx = (lambda bi, i, j: (bi, 0, j)) if mod_batched else (lambda bi, i, j: (0, 0, j))
    in_specs = [
        pl.BlockSpec((1, tm, d), lambda bi, i: (bi, i, 0)),
        pl.BlockSpec((1, 1, d), lambda bi, i: mod_idx(bi, i, 0)),
        pl.BlockSpec((1, 1, d), lambda bi, i: mod_idx(bi, i, 1)),
        pl.BlockSpec((1, d), lambda bi, i: (0, 0)),
        pl.BlockSpec((d, n), lambda bi, i: (0, 0)),
        pl.BlockSpec((1, n), lambda bi, i: (0, 0)),
        pl.BlockSpec((MXU_TILE, MXU_TILE), lambda bi, i: (0, 0)),
    ]
    args = [x, mod3, mod3, g, w, gain, ones]
    if use_rope:
        in_specs += [pl.BlockSpec((tm, LANES), lambda bi, i: (i, 0))] * 2
        args += list(rope_tabs)
    return pl.pallas_call(
        functools.partial(_inproj_kernel, groups=groups, use_rope=use_rope),
        grid=(b, s // tm),
        in_specs=in_specs,
        out_specs=[pl.BlockSpec((1, tm, wd), lambda bi, i: (bi, i, 0)) for wd in out_widths],
        out_shape=[jax.ShapeDtypeStruct((b, s, wd), BF16) for wd in out_widths],
        compiler_params=pltpu.CompilerParams(dimension_semantics=("parallel", "arbitrary"),
                                             vmem_limit_bytes=VMEM_LIMIT),
        name="in_proj_rope" if use_rope else "in_proj_ctx",
    )(*args)


def _softmax_pv(s_parts, v_parts, extra_logit):
    m = functools.reduce(jnp.maximum, [jnp.max(s, axis=-1, keepdims=True) for s in s_parts])
    if extra_logit is not None:
        m = jnp.maximum(m, extra_logit)
    ps = [jnp.exp2(s - m).astype(BF16) for s in s_parts]
    acc = functools.reduce(lambda a, c: a + c, [_dot(p, v) for p, v in zip(ps, v_parts)])
    l = acc[:, LANES:LANES + 1]
    if extra_logit is not None:
        l = l + jnp.exp2(extra_logit - m)
    return acc[:, :LANES] * (1.0 / l)


def _with_ones(v):
    return jnp.concatenate([v, jnp.ones_like(v)], axis=1)


def _group_norm(o, g):
    ms = jnp.mean(o * o, axis=-1, keepdims=True)
    return ((o * lax.rsqrt(ms + EPS)) * g).astype(BF16)


def _attn_kernel(sink_ref, qa_ref, qb_ref, ka_ref, va_ref, kb_ref, vb_ref, kac_ref, vac_ref,
                 kbc_ref, vbc_ref, bias_ref, ga_ref, gb_ref, o_ref, *, seq):
    i = pl.program_id(1)
    rows = seq // GRID_W
    lane = lax.broadcasted_iota(jnp.int32, (QBLK, LANES), 1)
    lo = lane < HEAD_DIM
    half_masks = (lo.astype(F32).astype(BF16), jnp.logical_not(lo).astype(F32).astype(BF16))

    group = A_HEADS // A_KV_HEADS
    start_a = pl.multiple_of(jnp.clip((i - 1) * QBLK, 0, seq - A_SPAN), QBLK)
    d = (lax.broadcasted_iota(jnp.int32, (QBLK, A_SPAN), 1)
         - lax.broadcasted_iota(jnp.int32, (QBLK, A_SPAN), 0) + (start_a - i * QBLK))
    amask = jnp.where((d <= A_WINDOW) & (d >= -A_WINDOW), 0.0, NEG).astype(F32)
    amask = jnp.concatenate([amask] * group, axis=0)
    k_loc = ka_ref[0, pl.ds(start_a, A_SPAN), :]
    v_loc = _with_ones(va_ref[0, pl.ds(start_a, A_SPAN), :])
    k_ctx = kac_ref[0]
    v_ctx = _with_ones(vac_ref[0])
    q_tiles = [qa_ref[0, :, t * LANES:(t + 1) * LANES] for t in range(group)]
    per_kv = []
    for kv in range(A_KV_HEADS):
        qs = jnp.concatenate([q * half_masks[kv] for q in q_tiles], axis=0)
        s_loc = _dot_nt(qs, k_loc) + amask
        s_ctx = _dot_nt(qs, k_ctx)
        sink = jnp.concatenate(
            [jnp.full((QBLK, 1), sink_ref[kv * group + t] * LOG2E, F32) for t in range(group)], axis=0)
        per_kv.append(_softmax_pv([s_loc, s_ctx], [v_loc, v_ctx], sink))
    oa = [jnp.where(lo, per_kv[0][t * QBLK:(t + 1) * QBLK], per_kv[1][t * QBLK:(t + 1) * QBLK])
          for t in range(group)]
    o_ref[0, :, :A_WIDTH] = _group_norm(jnp.concatenate(oa, axis=1), ga_ref[...])

    start_b = pl.multiple_of(
        jnp.clip(NA_ROW_BLOCK * i - NA_KH // 2, 0, rows - B_KROWS) * GRID_W, GRID_W)
    ob = []
    for pair in range(B_HEADS // 2):
        cols = slice(pair * LANES, (pair + 1) * LANES)
        qp = qb_ref[0, :, cols]
        qs = jnp.concatenate([qp * half_masks[0], qp * half_masks[1]], axis=0)
        k_loc = kb_ref[0, pl.ds(start_b, B_SPAN), cols]
        v_loc = _with_ones(vb_ref[0, pl.ds(start_b, B_SPAN), cols])
        k_ctx = kbc_ref[0, :, cols]
        v_ctx = _with_ones(vbc_ref[0, :, cols])
        bias = bias_ref[0, 2 * pair:2 * pair + 2].reshape(2 * QBLK, B_SPAN)
        s_loc = _dot_nt(qs, k_loc) + bias
        s_ctx = _dot_nt(qs, k_ctx)
        o = _softmax_pv([s_loc, s_ctx], [v_loc, v_ctx], None)
        ob.append(jnp.where(lo, o[:QBLK], o[QBLK:]))
    o_ref[0, :, A_WIDTH:] = _group_norm(jnp.concatenate(ob, axis=1), gb_ref[...])


def _attn_call(sink, qa, qb, ka, va, kb, vb, kac, vac, kbc, vbc, bias, ga, gb):
    b, s, _ = qa.shape
    l = kac.shape[1]
    n_blk = s // QBLK

    def pat(i):
        return jnp.where(i < 2, i, jnp.where(i >= n_blk - 2, i - (n_blk - N_PATTERNS), 2))

    full = lambda wd, ln: pl.BlockSpec((1, ln, wd), lambda bi, i: (bi, 0, 0))
    blk = lambda wd: pl.BlockSpec((1, QBLK, wd), lambda bi, i: (bi, i, 0))
    return pl.pallas_call(
        functools.partial(_attn_kernel, seq=s),
        grid=(b, n_blk),
        in_specs=[
            pl.BlockSpec(memory_space=pltpu.SMEM),
            blk(A_WIDTH), blk(B_WIDTH),
            full(A_KV_WIDTH, s), full(A_KV_WIDTH, s), full(B_WIDTH, s), full(B_WIDTH, s),
            full(A_KV_WIDTH, l), full(A_KV_WIDTH, l), full(B_WIDTH, l), full(B_WIDTH, l),
            pl.BlockSpec((1, B_HEADS, QBLK, B_SPAN), lambda bi, i: (pat(i), 0, 0, 0)),
            pl.BlockSpec((1, A_WIDTH), lambda bi, i: (0, 0)),
            pl.BlockSpec((1, B_WIDTH), lambda bi, i: (0, 0)),
        ],
        out_specs=pl.BlockSpec((1, QBLK, A_WIDTH + B_WIDTH), lambda bi, i: (bi, i, 0)),
        out_shape=jax.ShapeDtypeStruct((b, s, A_WIDTH + B_WIDTH), BF16),
        compiler_params=pltpu.CompilerParams(dimension_semantics=("parallel", "arbitrary"),
                                             vmem_limit_bytes=VMEM_LIMIT),
        name="attn",
    )(sink, qa, qb, ka, va, kb, vb, kac, vac, kbc, vbc, bias, ga, gb)


def _ffn_kernel(x_ref, o_ref, g1_ref, sh2_ref, sc2_ref, g2_ref, n2_ref, wo_ref, wg_ref, wu_ref, wd_ref,
                out_ref):
    x1 = x_ref[0] + g1_ref[0] * _dot(o_ref[0], wo_ref[...])
    ms = jnp.mean(x1 * x1, axis=-1, keepdims=True)
    geff = n2_ref[...] * (1.0 + sc2_ref[0])
    h2 = ((x1 * lax.rsqrt(ms + EPS)) * geff + sh2_ref[0]).astype(BF16)
    gate = _dot(h2, wg_ref[...])
    up = _dot(h2, wu_ref[...])
    act = (_silu(gate) * up).astype(BF16)
    out_ref[0] = x1 + g2_ref[0] * _dot(act, wd_ref[...])


def _ffn_call(x, o, mod3, n2, wo, wg, wu, wd, tm):
    b, s, d = x.shape
    hid = wg.shape[1]
    modspec = lambda j: pl.BlockSpec((1, 1, d), lambda bi, i: (bi, 0, j))
    const = lambda shape: pl.BlockSpec(shape, lambda bi, i: (0, 0), pipeline_mode=pl.Buffered(1))
    return pl.pallas_call(
        _ffn_kernel,
        grid=(b, s // tm),
        in_specs=[
            pl.BlockSpec((1, tm, d), lambda bi, i: (bi, i, 0)),
            pl.BlockSpec((1, tm, o.shape[2]), lambda bi, i: (bi, i, 0)),
            modspec(2), modspec(3), modspec(4), modspec(5),
            pl.BlockSpec((1, d), lambda bi, i: (0, 0)),
            const((o.shape[2], d)), const((d, hid)), const((d, hid)), const((hid, d)),
        ],
        out_specs=pl.BlockSpec((1, tm, d), lambda bi, i: (bi, i, 0)),
        out_shape=jax.ShapeDtypeStruct((b, s, d), F32),
        compiler_params=pltpu.CompilerParams(dimension_semantics=("parallel", "arbitrary"),
                                             vmem_limit_bytes=VMEM_LIMIT),
        name="out_ffn",
    )(x, o, mod3, mod3, mod3, mod3, n2, wo, wg, wu, wd)


def _layer(x, ctx, mod, norm1_g, w_in, qn_a, kn_a, sink_a, qn_b, kn_b, rpb_b, on_a, on_b, w_out,
           norm2_g, w_gate, w_up, w_down, rope_tabs):
    b, s, d = x.shape
    rows = s // GRID_W
    group = A_HEADS // A_KV_HEADS
    perm_heads = np.array([h for t in range(group) for h in (t, t + group)])
    perm_cols = (perm_heads[:, None] * HEAD_DIM + np.arange(HEAD_DIM)[None, :]).reshape(-1)

    o_qa, o_ka, o_va, o_qb, o_kb, o_vb = np.cumsum(
        [0, A_WIDTH, A_KV_WIDTH, A_KV_WIDTH, B_WIDTH, B_WIDTH])
    w_in_l = jnp.concatenate([w_in[:, :A_WIDTH][:, perm_cols], w_in[:, A_WIDTH:]], axis=1).astype(BF16)
    w_in_c = jnp.concatenate([w_in[:, o_ka:o_qb], w_in[:, o_kb:]], axis=1).astype(BF16)
    scale = HEAD_DIM ** -0.5 * LOG2E
    ones_v = lambda n: jnp.ones((n,), F32)
    gain_l = jnp.concatenate([jnp.tile(qn_a, A_HEADS) * scale, jnp.tile(kn_a, A_KV_HEADS), ones_v(A_KV_WIDTH),
                              jnp.tile(qn_b, B_HEADS) * scale, jnp.tile(kn_b, B_HEADS), ones_v(B_WIDTH)])[None]
    gain_c = jnp.concatenate([jnp.tile(kn_a, A_KV_HEADS), ones_v(A_KV_WIDTH),
                              jnp.tile(kn_b, B_HEADS), ones_v(B_WIDTH)])[None]
    ones_bd = jnp.asarray(np.kron(np.eye(MXU_TILE // HEAD_DIM), np.ones((HEAD_DIM, HEAD_DIM))), BF16)

    mod3 = mod[:b].reshape(b, 1, -1)
    mod3_c = mod[b:b + 1].reshape(1, 1, -1)
    g1 = norm1_g[None]

    groups_l = [
        (o_qa, A_WIDTH, [(0, A_WIDTH, True, True, 0)]),
        (o_ka, 2 * A_KV_WIDTH, [(0, A_KV_WIDTH, True, True, 1), (A_KV_WIDTH, A_KV_WIDTH, False, False, 2)]),
        (o_qb, B_WIDTH, [(0, B_WIDTH, True, False, 3)]),
        (o_kb, B_WIDTH, [(0, B_WIDTH, True, False, 4)]),
        (o_vb, B_WIDTH, [(0, B_WIDTH, False, False, 5)]),
    ]
    qa, ka, va, qb, kb, vb = _inproj_call(
        x, mod3, True, g1, w_in_l, gain_l, ones_bd, rope_tabs, groups_l,
        [A_WIDTH, A_KV_WIDTH, A_KV_WIDTH, B_WIDTH, B_WIDTH, B_WIDTH], tm=512)
    groups_c = [
        (0, 2 * A_KV_WIDTH, [(0, A_KV_WIDTH, True, False, 0), (A_KV_WIDTH, A_KV_WIDTH, False, False, 1)]),
        (2 * A_KV_WIDTH, B_WIDTH, [(0, B_WIDTH, True, False, 2)]),
        (2 * A_KV_WIDTH + B_WIDTH, B_WIDTH, [(0, B_WIDTH, False, False, 3)]),
    ]
    kac, vac, kbc, vbc = _inproj_call(
        ctx, mod3_c, False, g1, w_in_c, gain_c, ones_bd, None, groups_c,
        [A_KV_WIDTH, A_KV_WIDTH, B_WIDTH, B_WIDTH], tm=ctx.shape[1])

    bias = _bias_call(rpb_b, rows)
    o = _attn_call(sink_a, qa, qb, ka, va, kb, vb, kac, vac, kbc, vbc, bias,
                   on_a[perm_cols][None], on_b[None])
    w_out_p = jnp.concatenate([w_out[:A_WIDTH][perm_cols], w_out[A_WIDTH:]], axis=0).astype(BF16)
    return _ffn_call(x, o, mod3, norm2_g[None], w_out_p, w_gate.astype(BF16), w_up.astype(BF16),
                     w_down.astype(BF16), tm=256)


def kernel(x, c, ctx, c_ctx, w_mod, b_mod, norm1_g, w_in, qn_a, kn_a, sink_a, qn_b, kn_b, rpb_b, on_a, on_b,
           w_out, norm2_g, w_gate, w_up, w_down):
    b, s, d = x.shape
    depth = w_mod.shape[0]
    assert depth == 1, "the context stream update is only needed when a later layer reads it"
    rope_tabs = _rope_call(s, tm=512)
    pad = jnp.zeros((16 - b - 1, d), F32)
    cc = jnp.concatenate([c, c_ctx[None], pad], axis=0)
    l = 0
    mod = _mod_call(cc, w_mod[l], b_mod[l][None])
    return _layer(x, ctx, mod, norm1_g[l], w_in[l], qn_a[l], kn_a[l], sink_a[l], qn_b[l], kn_b[l], rpb_b[l],
                  on_a[l], on_b[l], w_out[l], norm2_g[l], w_gate[l], w_up[l], w_down[l], rope_tabs)
```

```python
import functools

import numpy as np
import jax
import jax.numpy as jnp
from jax import lax
from jax.experimental import pallas as pl
from jax.experimental.pallas import tpu as pltpu

GRID_W = 64
HEAD_DIM = 64
A_HEADS = 8
A_KV_HEADS = 2
A_WINDOW = 128
B_HEADS = 8
NA_KH = 8
NA_KW = 16
NA_ROW_BLOCK = 2
N_MOD = 6
ROPE_BASE = 10000.0
EPS = 1e-6

A_WIDTH = A_HEADS * HEAD_DIM
A_KV_WIDTH = A_KV_HEADS * HEAD_DIM
B_WIDTH = B_HEADS * HEAD_DIM
QBLK = NA_ROW_BLOCK * GRID_W
A_SPAN = QBLK + 2 * A_WINDOW
B_KROWS = 10
B_SPAN = B_KROWS * GRID_W
N_PATTERNS = 5
ATTN_BLOCKS_PER_STEP = 2
NEG = -1e30
LOG2E = 1.4426950408889634

LANES = 128
MXU_TILE = 256
VMEM_LIMIT = 56 * 1024 * 1024

F32 = jnp.float32
BF16 = jnp.bfloat16


def _dot(a, b):
    return jnp.dot(a, b, preferred_element_type=F32)


def _dot_nt(a, b):
    return lax.dot_general(a, b, (((1,), (1,)), ((), ())), preferred_element_type=F32)


def _silu(x):
    return x / (1.0 + jnp.exp(-x))


def _mod_kernel(c_ref, w_ref, b_ref, o_ref):
    a = _silu(c_ref[...])
    o_ref[...] = jnp.dot(a, w_ref[...], precision=lax.Precision.HIGHEST,
                         preferred_element_type=F32) + b_ref[...]


def _mod_call(cc, w_mod, b_mod):
    rows, d = cc.shape
    n = w_mod.shape[1]
    return pl.pallas_call(
        _mod_kernel,
        grid=(n // d,),
        in_specs=[pl.BlockSpec((rows, d), lambda j: (0, 0)),
                  pl.BlockSpec((d, d), lambda j: (0, j)),
                  pl.BlockSpec((1, d), lambda j: (0, j))],
        out_specs=pl.BlockSpec((rows, d), lambda j: (0, j)),
        out_shape=jax.ShapeDtypeStruct((rows, n), F32),
        name="mod",
    )(cc, w_mod, b_mod)


def _rope_kernel(invf_ref, cos_ref, sin_ref, *, tm):
    t = pl.program_id(0) * tm + lax.broadcasted_iota(jnp.int32, (tm, LANES), 0)
    lane = lax.broadcasted_iota(jnp.int32, (tm, LANES), 1)
    l64 = lane & (HEAD_DIM - 1)
    pos = jnp.where(l64 < HEAD_DIM // 2, t // GRID_W, t % GRID_W).astype(F32)
    ang = pos * invf_ref[...]
    first = (lane & (HEAD_DIM // 2 - 1)) < HEAD_DIM // 4
    cos_ref[...] = jnp.cos(ang)
    sin_ref[...] = jnp.where(first, -jnp.sin(ang), jnp.sin(ang))


def _rope_call(seq, tm):
    quarter = HEAD_DIM // 4
    inv = (1.0 / (np.float32(ROPE_BASE) ** (np.arange(quarter, dtype=np.float32) / quarter))).astype(np.float32)
    invf = jnp.asarray(np.tile(inv, LANES // quarter)[None, :])
    return pl.pallas_call(
        functools.partial(_rope_kernel, tm=tm),
        grid=(seq // tm,),
        in_specs=[pl.BlockSpec((1, LANES), lambda i: (0, 0))],
        out_specs=[pl.BlockSpec((tm, LANES), lambda i: (i, 0))] * 2,
        out_shape=[jax.ShapeDtypeStruct((seq, LANES), F32)] * 2,
        name="rope_tables",
    )(invf)


def _b_patterns(rows):
    n_blk = rows // NA_ROW_BLOCK
    blocks = [0, 1, 2, n_blk - 2, n_blk - 1]
    return [(NA_ROW_BLOCK * i, _b_key_start(i, rows)) for i in blocks]


def _b_key_start(i, rows):
    return min(max(NA_ROW_BLOCK * i - NA_KH // 2, 0), rows - B_KROWS)


def _bias_kernel(rpb_ref, o_ref, *, patterns, rows):
    h = pl.program_id(0)
    n_dr, n_dc = 2 * NA_KH - 1, 2 * NA_KW - 1
    row = lax.broadcasted_iota(jnp.int32, (GRID_W, LANES), 0)
    lane = lax.broadcasted_iota(jnp.int32, (GRID_W, LANES), 1)
    kc = lane & (GRID_W - 1)
    dc = jnp.clip(kc - row + NA_KW - 1, 0, n_dc - 1)
    cs = jnp.clip(row - NA_KW // 2, 0, GRID_W - NA_KW)
    col_ok = (kc >= cs) & (kc < cs + NA_KW)
    lo = lane < GRID_W
    base = h * (n_dr * n_dc)
    per_dr = []
    for dr in range(n_dr):
        m = jnp.zeros((GRID_W, LANES), F32)
        for d in range(n_dc):
            m = jnp.where(dc == d, rpb_ref[base + dr * n_dc + d] * LOG2E, m)
        per_dr.append(m)
    neg = jnp.full((GRID_W, LANES), NEG, F32)
    for p, (r0, ks) in enumerate(patterns):
        for qr in range(NA_ROW_BLOCK):
            q_row = r0 + qr
            rs = min(max(q_row - NA_KH // 2, 0), rows - NA_KH)
            for kp in range(B_KROWS // 2):
                k_rows = (ks + 2 * kp, ks + 2 * kp + 1)
                ok = [rs <= k < rs + NA_KH for k in k_rows]
                drs = [min(max(k - q_row + NA_KH - 1, 0), n_dr - 1) for k in k_rows]
                if not any(ok):
                    piece = neg
                else:
                    vals = jnp.where(lo, per_dr[drs[0]], per_dr[drs[1]])
                    mask = col_ok
                    if not ok[1]:
                        mask = mask & lo
                    if not ok[0]:
                        mask = mask & jnp.logical_not(lo)
                    piece = jnp.where(mask, vals, neg)
                o_ref[p, 0, qr * GRID_W:(qr + 1) * GRID_W, kp * LANES:(kp + 1) * LANES] = piece


def _bias_call(rpb, rows):
    patterns = _b_patterns(rows)
    heads = rpb.shape[0]
    return pl.pallas_call(
        functools.partial(_bias_kernel, patterns=patterns, rows=rows),
        grid=(heads,),
        in_specs=[pl.BlockSpec(memory_space=pltpu.SMEM)],
        out_specs=pl.BlockSpec((N_PATTERNS, 1, QBLK, B_SPAN), lambda h: (0, h, 0, 0)),
        out_shape=jax.ShapeDtypeStruct((N_PATTERNS, heads, QBLK, B_SPAN), F32),
        name="na_bias",
    )(rpb.reshape(-1))


def _group_rms(p, ones_ref):
    sq = (p * p).astype(BF16)
    n = p.shape[1]
    if n <= MXU_TILE:
        ssq = _dot(sq, ones_ref[:n, :n])
    else:
        ssq = jnp.concatenate(
            [_dot(sq[:, c:c + MXU_TILE], ones_ref[...]) for c in range(0, n, MXU_TILE)], axis=1)
    return lax.rsqrt(ssq * (1.0 / HEAD_DIM) + EPS)


def _rope(x, cos, sin):
    lane = lax.broadcasted_iota(jnp.int32, (x.shape[0], LANES), 1)
    first = (lane & (HEAD_DIM // 2 - 1)) < HEAD_DIM // 4
    outs = []
    for c in range(0, x.shape[1], LANES):
        xt = x[:, c:c + LANES]
        sw = jnp.where(first, pltpu.roll(xt, LANES - HEAD_DIM // 4, 1), pltpu.roll(xt, HEAD_DIM // 4, 1))
        outs.append(xt * cos + sw * sin)
    return outs[0] if len(outs) == 1 else jnp.concatenate(outs, axis=1)


def _inproj_kernel(x_ref, sh_ref, sc_ref, g_ref, w_ref, gain_ref, ones_ref, *rest, groups, use_rope):
    if use_rope:
        cos_ref, sin_ref = rest[:2]
        out_refs = rest[2:]
        cos, sin = cos_ref[...], sin_ref[...]
    else:
        out_refs = rest
    x = x_ref[0]
    ms = jnp.mean(x * x, axis=-1, keepdims=True)
    geff = g_ref[...] * (1.0 + sc_ref[0])
    h = ((x * lax.rsqrt(ms + EPS)) * geff + sh_ref[0]).astype(BF16)
    for c0, width, subs in groups:
        p = _dot(h, w_ref[:, c0:c0 + width])
        for s0, sw, norm, rope, oi in subs:
            y = p[:, s0:s0 + sw]
            if norm:
                y = y * _group_rms(y, ones_ref) * gain_ref[:, c0 + s0:c0 + s0 + sw]
            if rope:
                y = _rope(y, cos, sin)
            out_refs[oi][0] = y.astype(BF16)


def _inproj_call(x, mod3, mod_batched, g, w, gain, ones, rope_tabs, groups, out_widths, tm):
    b, s, d = x.shape
    n = w.shape[1]
    use_rope = rope_tabs is not None
    mod_idx = (lambda bi, i, j: (bi, 0, j)) if mod_batched else (lambda bi, i, j: (0, 0, j))
    in_specs = [
        pl.BlockSpec((1, tm, d), lambda bi, i: (bi, i, 0)),
        pl.BlockSpec((1, 1, d), lambda bi, i: mod_idx(bi, i, 0)),
        pl.BlockSpec((1, 1, d), lambda bi, i: mod_idx(bi, i, 1)),
        pl.BlockSpec((1, d), lambda bi, i: (0, 0)),
        pl.BlockSpec((d, n), lambda bi, i: (0, 0)),
        pl.BlockSpec((1, n), lambda bi, i: (0, 0)),
        pl.BlockSpec((MXU_TILE, MXU_TILE), lambda bi, i: (0, 0)),
    ]
    args = [x, mod3, mod3, g, w, gain, ones]
    if use_rope:
        in_specs += [pl.BlockSpec((tm, LANES), lambda bi, i: (i, 0))] * 2
        args += list(rope_tabs)
    return pl.pallas_call(
        functools.partial(_inproj_kernel, groups=groups, use_rope=use_rope),
        grid=(b, s // tm),
        in_specs=in_specs,
        out_specs=[pl.BlockSpec((1, tm, wd), lambda bi, i: (bi, i, 0)) for wd in out_widths],
        out_shape=[jax.ShapeDtypeStruct((b, s, wd), BF16) for wd in out_widths],
        compiler_params=pltpu.CompilerParams(dimension_semantics=("parallel", "arbitrary"),
                                             vmem_limit_bytes=VMEM_LIMIT),
        name="in_proj_rope" if use_rope else "in_proj_ctx",
    )(*args)


def _attend(qs, k_loc, k_ctx, v_loc, v_ctx, add_loc, extra_logit):
    s_parts = [_dot_nt(qs, k_loc) + add_loc, _dot_nt(qs, k_ctx)]
    m = functools.reduce(jnp.maximum, [jnp.max(sp, axis=-1, keepdims=True) for sp in s_parts])
    if extra_logit is not None:
        m = jnp.maximum(m, extra_logit)
    with_ones = lambda v: jnp.concatenate([v, jnp.ones_like(v)], axis=1)
    acc = (_dot(jnp.exp2(s_parts[0] - m).astype(BF16), with_ones(v_loc))
           + _dot(jnp.exp2(s_parts[1] - m).astype(BF16), with_ones(v_ctx)))
    l = acc[:, LANES:LANES + 1]
    if extra_logit is not None:
        l = l + jnp.exp2(extra_logit - m)
    return acc[:, :LANES] * (1.0 / l)


def _group_norm(o, g):
    ms = jnp.mean(o * o, axis=-1, keepdims=True)
    return ((o * lax.rsqrt(ms + EPS)) * g).astype(BF16)


def _attn_kernel(sink_ref, qa_ref, qb_ref, ka_ref, va_ref, kb_ref, vb_ref, kac_ref, vac_ref,
                 kbc_ref, vbc_ref, ga_ref, gb_ref, *rest, seq, n_sub):
    bias_refs, o_ref = rest[:n_sub], rest[n_sub]
    for j in range(n_sub):
        _attn_block(pl.program_id(1) * n_sub + j, slice(j * QBLK, (j + 1) * QBLK), sink_ref, qa_ref, qb_ref,
                    ka_ref, va_ref, kb_ref, vb_ref, kac_ref, vac_ref, kbc_ref, vbc_ref, ga_ref, gb_ref,
                    bias_refs[j], o_ref, seq)


def _attn_block(i, qrows, sink_ref, qa_ref, qb_ref, ka_ref, va_ref, kb_ref, vb_ref, kac_ref, vac_ref,
                kbc_ref, vbc_ref, ga_ref, gb_ref, bias_ref, o_ref, seq):
    rows = seq // GRID_W
    lane = lax.broadcasted_iota(jnp.int32, (QBLK, LANES), 1)
    lo = lane < HEAD_DIM
    half_masks = (lo.astype(F32).astype(BF16), jnp.logical_not(lo).astype(F32).astype(BF16))

    group = A_HEADS // A_KV_HEADS
    start_a = pl.multiple_of(jnp.clip((i - 1) * QBLK, 0, seq - A_SPAN), QBLK)
    d = (lax.broadcasted_iota(jnp.int32, (QBLK, A_SPAN), 1)
         - lax.broadcasted_iota(jnp.int32, (QBLK, A_SPAN), 0) + (start_a - i * QBLK))
    amask = jnp.where((d <= A_WINDOW) & (d >= -A_WINDOW), 0.0, NEG).astype(F32)
    amask = jnp.concatenate([amask] * group, axis=0)
    k_loc = ka_ref[0, pl.ds(start_a, A_SPAN), :]
    v_loc = va_ref[0, pl.ds(start_a, A_SPAN), :]
    k_ctx = kac_ref[0]
    v_ctx = vac_ref[0]
    q_tiles = [qa_ref[0, qrows, t * LANES:(t + 1) * LANES] for t in range(group)]
    per_kv = []
    for kv in range(A_KV_HEADS):
        qs = jnp.concatenate([q * half_masks[kv] for q in q_tiles], axis=0)
        sink = jnp.concatenate(
            [jnp.full((QBLK, 1), sink_ref[kv * group + t] * LOG2E, F32) for t in range(group)], axis=0)
        per_kv.append(_attend(qs, k_loc, k_ctx, v_loc, v_ctx, amask, sink))
    oa = [jnp.where(lo, per_kv[0][t * QBLK:(t + 1) * QBLK], per_kv[1][t * QBLK:(t + 1) * QBLK])
          for t in range(group)]
    o_ref[0, qrows, :A_WIDTH] = _group_norm(jnp.concatenate(oa, axis=1), ga_ref[...])

    start_b = pl.multiple_of(
        jnp.clip(NA_ROW_BLOCK * i - NA_KH // 2, 0, rows - B_KROWS) * GRID_W, GRID_W)
    ob = []
    for pair in range(B_HEADS // 2):
        cols = slice(pair * LANES, (pair + 1) * LANES)
        qp = qb_ref[0, qrows, cols]
        qs = jnp.concatenate([qp * half_masks[0], qp * half_masks[1]], axis=0)
        k_loc = kb_ref[0, pl.ds(start_b, B_SPAN), cols]
        v_loc = vb_ref[0, pl.ds(start_b, B_SPAN), cols]
        k_ctx = kbc_ref[0, :, cols]
        v_ctx = vbc_ref[0, :, cols]
        bias = bias_ref[0, 2 * pair:2 * pair + 2].reshape(2 * QBLK, B_SPAN)
        o = _attend(qs, k_loc, k_ctx, v_loc, v_ctx, bias, None)
        ob.append(jnp.where(lo, o[:QBLK], o[QBLK:]))
    o_ref[0, qrows, A_WIDTH:] = _group_norm(jnp.concatenate(ob, axis=1), gb_ref[...])


def _attn_call(sink, qa, qb, ka, va, kb, vb, kac, vac, kbc, vbc, bias, ga, gb):
    b, s, _ = qa.shape
    l = kac.shape[1]
    n_blk = s // QBLK

    def pat(i):
        return jnp.where(i < 2, i, jnp.where(i >= n_blk - 2, i - (n_blk - N_PATTERNS), 2))

    n_sub = ATTN_BLOCKS_PER_STEP
    full = lambda wd, ln: pl.BlockSpec((1, ln, wd), lambda bi, i: (bi, 0, 0))
    blk = lambda wd: pl.BlockSpec((1, n_sub * QBLK, wd), lambda bi, i: (bi, i, 0))
    bias_spec = lambda j: pl.BlockSpec((1, B_HEADS, QBLK, B_SPAN), lambda bi, i: (pat(i * n_sub + j), 0, 0, 0))
    return pl.pallas_call(
        functools.partial(_attn_kernel, seq=s, n_sub=n_sub),
        grid=(b, n_blk // n_sub),
        in_specs=[
            pl.BlockSpec(memory_space=pltpu.SMEM),
            blk(A_WIDTH), blk(B_WIDTH),
            full(A_KV_WIDTH, s), full(A_KV_WIDTH, s), full(B_WIDTH, s), full(B_WIDTH, s),
            full(A_KV_WIDTH, l), full(A_KV_WIDTH, l), full(B_WIDTH, l), full(B_WIDTH, l),
            pl.BlockSpec((1, A_WIDTH), lambda bi, i: (0, 0)),
            pl.BlockSpec((1, B_WIDTH), lambda bi, i: (0, 0)),
        ] + [bias_spec(j) for j in range(n_sub)],
        out_specs=blk(A_WIDTH + B_WIDTH),
        out_shape=jax.ShapeDtypeStruct((b, s, A_WIDTH + B_WIDTH), BF16),
        compiler_params=pltpu.CompilerParams(dimension_semantics=("parallel", "arbitrary"),
                                             vmem_limit_bytes=VMEM_LIMIT),
        name="attn",
    )(sink, qa, qb, ka, va, kb, vb, kac, vac, kbc, vbc, ga, gb, *([bias] * n_sub))


def _ffn_kernel(x_ref, o_ref, g1_ref, sh2_ref, sc2_ref, g2_ref, n2_ref, wo_ref, wg_ref, wu_ref, wd_ref,
                out_ref):
    x1 = x_ref[0] + g1_ref[0] * _dot(o_ref[0], wo_ref[...])
    ms = jnp.mean(x1 * x1, axis=-1, keepdims=True)
    geff = n2_ref[...] * (1.0 + sc2_ref[0])
    h2 = ((x1 * lax.rsqrt(ms + EPS)) * geff + sh2_ref[0]).astype(BF16)
    gate = _dot(h2, wg_ref[...])
    up = _dot(h2, wu_ref[...])
    act = (_silu(gate) * up).astype(BF16)
    out_ref[0] = x1 + g2_ref[0] * _dot(act, wd_ref[...])


def _ffn_call(x, o, mod3, n2, wo, wg, wu, wd, tm):
    b, s, d = x.shape
    hid = wg.shape[1]
    modspec = lambda j: pl.BlockSpec((1, 1, d), lambda bi, i: (bi, 0, j))
    const = lambda shape: pl.BlockSpec(shape, lambda bi, i: (0, 0), pipeline_mode=pl.Buffered(1))
    return pl.pallas_call(
        _ffn_kernel,
        grid=(b, s // tm),
        in_specs=[
            pl.BlockSpec((1, tm, d), lambda bi, i: (bi, i, 0)),
            pl.BlockSpec((1, tm, o.shape[2]), lambda bi, i: (bi, i, 0)),
            modspec(2), modspec(3), modspec(4), modspec(5),
            pl.BlockSpec((1, d), lambda bi, i: (0, 0)),
            const((o.shape[2], d)), const((d, hid)), const((d, hid)), const((hid, d)),
        ],
        out_specs=pl.BlockSpec((1, tm, d), lambda bi, i: (bi, i, 0)),
        out_shape=jax.ShapeDtypeStruct((b, s, d), F32),
        compiler_params=pltpu.CompilerParams(dimension_semantics=("parallel", "arbitrary"),
                                             vmem_limit_bytes=VMEM_LIMIT),
        name="out_ffn",
    )(x, o, mod3, mod3, mod3, mod3, n2, wo, wg, wu, wd)


def _layer(x, ctx, mod, norm1_g, w_in, qn_a, kn_a, sink_a, qn_b, kn_b, rpb_b, on_a, on_b, w_out,
           norm2_g, w_gate, w_up, w_down, rope_tabs):
    b, s, d = x.shape
    rows = s // GRID_W
    group = A_HEADS // A_KV_HEADS
    perm_heads = np.array([h for t in range(group) for h in (t, t + group)])
    perm_cols = (perm_heads[:, None] * HEAD_DIM + np.arange(HEAD_DIM)[None, :]).reshape(-1)

    o_qa, o_ka, o_va, o_qb, o_kb, o_vb = np.cumsum(
        [0, A_WIDTH, A_KV_WIDTH, A_KV_WIDTH, B_WIDTH, B_WIDTH])
    w_in_l = jnp.concatenate([w_in[:, :A_WIDTH][:, perm_cols], w_in[:, A_WIDTH:]], axis=1).astype(BF16)
    w_in_c = jnp.concatenate([w_in[:, o_ka:o_qb], w_in[:, o_kb:]], axis=1).astype(BF16)
    scale = HEAD_DIM ** -0.5 * LOG2E
    ones_v = lambda n: jnp.ones((n,), F32)
    gain_l = jnp.concatenate([jnp.tile(qn_a, A_HEADS) * scale, jnp.tile(kn_a, A_KV_HEADS), ones_v(A_KV_WIDTH),
                              jnp.tile(qn_b, B_HEADS) * scale, jnp.tile(kn_b, B_HEADS), ones_v(B_WIDTH)])[None]
    gain_c = jnp.concatenate([jnp.tile(kn_a, A_KV_HEADS), ones_v(A_KV_WIDTH),
                              jnp.tile(kn_b, B_HEADS), ones_v(B_WIDTH)])[None]
    ones_bd = jnp.asarray(np.kron(np.eye(MXU_TILE // HEAD_DIM), np.ones((HEAD_DIM, HEAD_DIM))), BF16)

    mod3 = mod[:b].reshape(b, 1, -1)
    mod3_c = mod[b:b + 1].reshape(1, 1, -1)
    g1 = norm1_g[None]

    groups_l = [
        (o_qa, A_WIDTH, [(0, A_WIDTH, True, True, 0)]),
        (o_ka, 2 * A_KV_WIDTH, [(0, A_KV_WIDTH, True, True, 1), (A_KV_WIDTH, A_KV_WIDTH, False, False, 2)]),
        (o_qb, B_WIDTH, [(0, B_WIDTH, True, False, 3)]),
        (o_kb, B_WIDTH, [(0, B_WIDTH, True, False, 4)]),
        (o_vb, B_WIDTH, [(0, B_WIDTH, False, False, 5)]),
    ]
    qa, ka, va, qb, kb, vb = _inproj_call(
        x, mod3, True, g1, w_in_l, gain_l, ones_bd, rope_tabs, groups_l,
        [A_WIDTH, A_KV_WIDTH, A_KV_WIDTH, B_WIDTH, B_WIDTH, B_WIDTH], tm=512)
    groups_c = [
        (0, 2 * A_KV_WIDTH, [(0, A_KV_WIDTH, True, False, 0), (A_KV_WIDTH, A_KV_WIDTH, False, False, 1)]),
        (2 * A_KV_WIDTH, B_WIDTH, [(0, B_WIDTH, True, False, 2)]),
        (2 * A_KV_WIDTH + B_WIDTH, B_WIDTH, [(0, B_WIDTH, False, False, 3)]),
    ]
    kac, vac, kbc, vbc = _inproj_call(
        ctx, mod3_c, False, g1, w_in_c, gain_c, ones_bd, None, groups_c,
        [A_KV_WIDTH, A_KV_WIDTH, B_WIDTH, B_WIDTH], tm=ctx.shape[1])

    bias = _bias_call(rpb_b, rows)
    o = _attn_call(sink_a, qa, qb, ka, va, kb, vb, kac, vac, kbc, vbc, bias,
                   on_a[perm_cols][None], on_b[None])
    w_out_p = jnp.concatenate([w_out[:A_WIDTH][perm_cols], w_out[A_WIDTH:]], axis=0).astype(BF16)
    return _ffn_call(x, o, mod3, norm2_g[None], w_out_p, w_gate.astype(BF16), w_up.astype(BF16),
                     w_down.astype(BF16), tm=256)


def kernel(x, c, ctx, c_ctx, w_mod, b_mod, norm1_g, w_in, qn_a, kn_a, sink_a, qn_b, kn_b, rpb_b, on_a, on_b,
           w_out, norm2_g, w_gate, w_up, w_down):
    b, s, d = x.shape
    depth = w_mod.shape[0]
    assert depth == 1, "the context stream update is only needed when a later layer reads it"
    rope_tabs = _rope_call(s, tm=512)
    pad = jnp.zeros((16 - b - 1, d), F32)
    cc = jnp.concatenate([c, c_ctx[None], pad], axis=0)
    l = 0
    mod = _mod_call(cc, w_mod[l], b_mod[l][None])
    return _layer(x, ctx, mod, norm1_g[l], w_in[l], qn_a[l], kn_a[l], sink_a[l], qn_b[l], kn_b[l], rpb_b[l],
                  on_a[l], on_b[l], w_out[l], norm2_g[l], w_gate[l], w_up[l], w_down[l], rope_tabs)
```

```python
import functools

import numpy as np
import jax
import jax.numpy as jnp
from jax import lax
from jax.experimental import pallas as pl
from jax.experimental.pallas import tpu as pltpu

GRID_W = 64
HEAD_DIM = 64
A_HEADS = 8
A_KV_HEADS = 2
A_WINDOW = 128
B_HEADS = 8
NA_KH = 8
NA_KW = 16
NA_ROW_BLOCK = 2
N_MOD = 6
ROPE_BASE = 10000.0
EPS = 1e-6

A_WIDTH = A_HEADS * HEAD_DIM
A_KV_WIDTH = A_KV_HEADS * HEAD_DIM
B_WIDTH = B_HEADS * HEAD_DIM
QBLK = NA_ROW_BLOCK * GRID_W
A_SPAN = QBLK + 2 * A_WINDOW
B_KROWS = 10
B_SPAN = B_KROWS * GRID_W
N_PATTERNS = 5
ATTN_BLOCKS_PER_STEP = 2
ATTN_LOOKAHEAD = 2
NEG = -1e30
LOG2E = 1.4426950408889634

LANES = 128
BF16_ROWS = 16
MXU_TILE = 256
VMEM_LIMIT = 56 * 1024 * 1024

F32 = jnp.float32
BF16 = jnp.bfloat16


def _dot(a, b):
    return jnp.dot(a, b, preferred_element_type=F32)


def _dot_nt(a, b):
    return lax.dot_general(a, b, (((1,), (1,)), ((), ())), preferred_element_type=F32)


def _silu(x):
    return x / (1.0 + jnp.exp(-x))


def _mod_kernel(c_ref, w_ref, b_ref, o_ref):
    a = _silu(c_ref[...])
    o_ref[...] = jnp.dot(a, w_ref[...], precision=lax.Precision.HIGHEST,
                         preferred_element_type=F32) + b_ref[...]


def _mod_call(cc, w_mod, b_mod):
    rows, d = cc.shape
    n = w_mod.shape[1]
    return pl.pallas_call(
        _mod_kernel,
        grid=(n // d,),
        in_specs=[pl.BlockSpec((rows, d), lambda j: (0, 0)),
                  pl.BlockSpec((d, d), lambda j: (0, j)),
                  pl.BlockSpec((1, d), lambda j: (0, j))],
        out_specs=pl.BlockSpec((rows, d), lambda j: (0, j)),
        out_shape=jax.ShapeDtypeStruct((rows, n), F32),
        name="mod",
    )(cc, w_mod, b_mod)


def _rope_kernel(invf_ref, cos_ref, sin_ref, *, tm):
    t = pl.program_id(0) * tm + lax.broadcasted_iota(jnp.int32, (tm, LANES), 0)
    lane = lax.broadcasted_iota(jnp.int32, (tm, LANES), 1)
    l64 = lane & (HEAD_DIM - 1)
    pos = jnp.where(l64 < HEAD_DIM // 2, t // GRID_W, t % GRID_W).astype(F32)
    ang = pos * invf_ref[...]
    first = (lane & (HEAD_DIM // 2 - 1)) < HEAD_DIM // 4
    cos_ref[...] = jnp.cos(ang)
    sin_ref[...] = jnp.where(first, -jnp.sin(ang), jnp.sin(ang))


def _rope_call(seq, tm):
    quarter = HEAD_DIM // 4
    inv = (1.0 / (np.float32(ROPE_BASE) ** (np.arange(quarter, dtype=np.float32) / quarter))).astype(np.float32)
    invf = jnp.asarray(np.tile(inv, LANES // quarter)[None, :])
    return pl.pallas_call(
        functools.partial(_rope_kernel, tm=tm),
        grid=(seq // tm,),
        in_specs=[pl.BlockSpec((1, LANES), lambda i: (0, 0))],
        out_specs=[pl.BlockSpec((tm, LANES), lambda i: (i, 0))] * 2,
        out_shape=[jax.ShapeDtypeStruct((seq, LANES), F32)] * 2,
        name="rope_tables",
    )(invf)


def _b_patterns(rows):
    n_blk = rows // NA_ROW_BLOCK
    blocks = [0, 1, 2, n_blk - 2, n_blk - 1]
    return [(NA_ROW_BLOCK * i, _b_key_start(i, rows)) for i in blocks]


def _b_key_start(i, rows):
    return min(max(NA_ROW_BLOCK * i - NA_KH // 2, 0), rows - B_KROWS)


def _bias_kernel(rpb_ref, o_ref, *, patterns, rows):
    h = pl.program_id(0)
    n_dr, n_dc = 2 * NA_KH - 1, 2 * NA_KW - 1
    kc = lax.broadcasted_iota(jnp.int32, (GRID_W, LANES), 0)
    lane = lax.broadcasted_iota(jnp.int32, (GRID_W, LANES), 1)
    qc = lane & (GRID_W - 1)
    dc = jnp.clip(kc - qc + NA_KW - 1, 0, n_dc - 1)
    cs = jnp.clip(qc - NA_KW // 2, 0, GRID_W - NA_KW)
    col_ok = (kc >= cs) & (kc < cs + NA_KW)
    lo = lane < GRID_W
    base = h * (n_dr * n_dc)
    per_dr = []
    for dr in range(n_dr):
        m = jnp.zeros((GRID_W, LANES), F32)
        for d in range(n_dc):
            m = jnp.where(dc == d, rpb_ref[base + dr * n_dc + d] * LOG2E, m)
        per_dr.append(m)
    neg = jnp.full((GRID_W, LANES), NEG, F32)
    for p, (r0, ks) in enumerate(patterns):
        q_rows = [r0 + qr for qr in range(NA_ROW_BLOCK)]
        rs = [min(max(q - NA_KH // 2, 0), rows - NA_KH) for q in q_rows]
        for kr in range(B_KROWS):
            k_row = ks + kr
            ok = [r <= k_row < r + NA_KH for r in rs]
            drs = [min(max(k_row - q + NA_KH - 1, 0), n_dr - 1) for q in q_rows]
            if not any(ok):
                piece = neg
            else:
                vals = jnp.where(lo, per_dr[drs[0]], per_dr[drs[1]])
                mask = col_ok
                if not ok[1]:
                    mask = mask & lo
                if not ok[0]:
                    mask = mask & jnp.logical_not(lo)
                piece = jnp.where(mask, vals, neg)
            o_ref[p, 0, kr * GRID_W:(kr + 1) * GRID_W, :] = piece


def _bias_call(rpb, rows):
    patterns = _b_patterns(rows)
    heads = rpb.shape[0]
    return pl.pallas_call(
        functools.partial(_bias_kernel, patterns=patterns, rows=rows),
        grid=(heads,),
        in_specs=[pl.BlockSpec(memory_space=pltpu.SMEM)],
        out_specs=pl.BlockSpec((N_PATTERNS, 1, B_SPAN, QBLK), lambda h: (0, h // 2, 0, h % 2)),
        out_shape=jax.ShapeDtypeStruct((N_PATTERNS, heads // 2, B_SPAN, 2 * QBLK), F32),
        name="na_bias",
    )(rpb.reshape(-1))


def _group_rms(p, ones_ref):
    sq = (p * p).astype(BF16)
    n = p.shape[1]
    if n <= MXU_TILE:
        ssq = _dot(sq, ones_ref[:n, :n])
    else:
        ssq = jnp.concatenate(
            [_dot(sq[:, c:c + MXU_TILE], ones_ref[...]) for c in range(0, n, MXU_TILE)], axis=1)
    return lax.rsqrt(ssq * (1.0 / HEAD_DIM) + EPS)


def _rope(x, cos, sin):
    lane = lax.broadcasted_iota(jnp.int32, (x.shape[0], LANES), 1)
    first = (lane & (HEAD_DIM // 2 - 1)) < HEAD_DIM // 4
    outs = []
    for c in range(0, x.shape[1], LANES):
        xt = x[:, c:c + LANES]
        sw = jnp.where(first, pltpu.roll(xt, LANES - HEAD_DIM // 4, 1), pltpu.roll(xt, HEAD_DIM // 4, 1))
        outs.append(xt * cos + sw * sin)
    return outs[0] if len(outs) == 1 else jnp.concatenate(outs, axis=1)


def _inproj_kernel(x_ref, sh_ref, sc_ref, g_ref, w_ref, gain_ref, ones_ref, wva_ref, wvb_ref, *rest,
                   groups, use_rope):
    if use_rope:
        cos, sin = rest[0][...], rest[1][...]
        rest = rest[2:]
    n_std = sum(len(subs) for _, _, subs in groups)
    out_refs, vt_refs = rest[:n_std], rest[n_std:]
    x = x_ref[0]
    ms = jnp.mean(x * x, axis=-1, keepdims=True)
    geff = g_ref[...] * (1.0 + sc_ref[0])
    h = ((x * lax.rsqrt(ms + EPS)) * geff + sh_ref[0]).astype(BF16)
    for c0, width, subs in groups:
        p = _dot(h, w_ref[:, c0:c0 + width])
        for s0, sw, rope, oi in subs:
            y = p[:, s0:s0 + sw]
            y = y * _group_rms(y, ones_ref) * gain_ref[:, c0 + s0:c0 + s0 + sw]
            if rope:
                y = _rope(y, cos, sin)
            out_refs[oi][0] = y.astype(BF16)
    for wv_ref, vt_ref in zip((wva_ref, wvb_ref), vt_refs):
        vt = _dot_nt(wv_ref[...], h).astype(BF16)
        for j in range(vt.shape[1] // LANES):
            vt_ref[0, j] = vt[:, j * LANES:(j + 1) * LANES]


def _inproj_call(x, mod3, mod_batched, g, w, gain, ones, wva_t, wvb_t, rope_tabs, groups, out_widths, tm):
    b, s, d = x.shape
    n = w.shape[1]
    use_rope = rope_tabs is not None
    mod_idx = (lambda bi, i, j: (bi, 0, j)) if mod_batched else (lambda bi, i, j: (0, 0, j))
    const = lambda arr: pl.BlockSpec(arr.shape, lambda bi, i: (0, 0))
    in_specs = [
        pl.BlockSpec((1, tm, d), lambda bi, i: (bi, i, 0)),
        pl.BlockSpec((1, 1, d), lambda bi, i: mod_idx(bi, i, 0)),
        pl.BlockSpec((1, 1, d), lambda bi, i: mod_idx(bi, i, 1)),
        const(g), const(w), const(gain), const(ones), const(wva_t), const(wvb_t),
    ]
    args = [x, mod3, mod3, g, w, gain, ones, wva_t, wvb_t]
    if use_rope:
        in_specs += [pl.BlockSpec((tm, LANES), lambda bi, i: (i, 0))] * 2
        args += list(rope_tabs)
    vt_dims = [wva_t.shape[0], wvb_t.shape[0]]
    return pl.pallas_call(
        functools.partial(_inproj_kernel, groups=groups, use_rope=use_rope),
        grid=(b, s // tm),
        in_specs=in_specs,
        out_specs=([pl.BlockSpec((1, tm, wd), lambda bi, i: (bi, i, 0)) for wd in out_widths]
                   + [pl.BlockSpec((1, tm // LANES, vd, LANES), lambda bi, i: (bi, i, 0, 0)) for vd in vt_dims]),
        out_shape=([jax.ShapeDtypeStruct((b, s, wd), BF16) for wd in out_widths]
                   + [jax.ShapeDtypeStruct((b, s // LANES, vd, LANES), BF16) for vd in vt_dims]),
        compiler_params=pltpu.CompilerParams(dimension_semantics=("parallel", "arbitrary"),
                                             vmem_limit_bytes=VMEM_LIMIT),
        name="in_proj_rope" if use_rope else "in_proj_ctx",
    )(*args)


def _scores_t(qs, k_loc, k_ctx, add_loc):
    return _dot_nt(k_loc, qs) + add_loc, _dot_nt(k_ctx, qs)


def _softmax_pv_t(s_loc, s_ctx, vt_loc, vt_ctx, extra_logit):
    m = jnp.maximum(jnp.max(s_loc, axis=0, keepdims=True), jnp.max(s_ctx, axis=0, keepdims=True))
    if extra_logit is not None:
        m = jnp.maximum(m, extra_logit)
    n_v = vt_loc.shape[0]
    with_ones = lambda vt: jnp.concatenate([vt, jnp.ones((BF16_ROWS, vt.shape[1]), BF16)], axis=0)
    acc = (_dot(with_ones(vt_loc), jnp.exp2(s_loc - m).astype(BF16))
           + _dot(with_ones(vt_ctx), jnp.exp2(s_ctx - m).astype(BF16)))
    l = acc[n_v:n_v + 1]
    if extra_logit is not None:
        l = l + jnp.exp2(extra_logit - m)
    return acc[:n_v] * (1.0 / l)


def _group_norm_t(o_t, g_t):
    ms = jnp.mean(o_t * o_t, axis=0, keepdims=True)
    return ((o_t * lax.rsqrt(ms + EPS)) * g_t).T.astype(BF16)


def _lane_tiles(ref, first, n, rows=slice(None)):
    return jnp.concatenate([ref[0, first + j, rows, :] for j in range(n)], axis=1)


def _attn_kernel(sink_ref, qa_ref, qb_ref, ka_ref, vat_ref, kb_ref, vbt_ref, kac_ref, vact_ref,
                 kbc_ref, vbct_ref, ga_ref, gb_ref, *rest, seq, n_sub):
    bias_refs, o_ref = rest[:n_sub], rest[n_sub]
    groups = []
    for j in range(n_sub):
        groups += _attn_block_groups(
            pl.program_id(1) * n_sub + j, slice(j * QBLK, (j + 1) * QBLK), sink_ref, qa_ref, qb_ref, ka_ref,
            vat_ref, kb_ref, vbt_ref, kac_ref, vact_ref, kbc_ref, vbct_ref, ga_ref, gb_ref, bias_refs[j], o_ref, seq)
    pending = []
    for scores, finish in groups:
        pending.append((finish, scores()))
        if len(pending) > ATTN_LOOKAHEAD:
            fin, s = pending.pop(0)
            fin(*s)
    for fin, s in pending:
        fin(*s)


def _attn_block_groups(i, qrows, sink_ref, qa_ref, qb_ref, ka_ref, vat_ref, kb_ref, vbt_ref, kac_ref, vact_ref,
                       kbc_ref, vbct_ref, ga_ref, gb_ref, bias_ref, o_ref, seq):
    rows = seq // GRID_W
    stages = []
    n_ctx_tiles = kac_ref.shape[1] // LANES
    lane = lax.broadcasted_iota(jnp.int32, (QBLK, LANES), 1)
    lo = lane < HEAD_DIM
    half_masks = (lo.astype(F32).astype(BF16), jnp.logical_not(lo).astype(F32).astype(BF16))

    group = A_HEADS // A_KV_HEADS
    blk_a = jnp.clip(i - 1, 0, seq // QBLK - A_SPAN // QBLK)
    start_a = pl.multiple_of(blk_a * QBLK, QBLK)
    d = (lax.broadcasted_iota(jnp.int32, (A_SPAN, QBLK), 0)
         - lax.broadcasted_iota(jnp.int32, (A_SPAN, QBLK), 1) + (start_a - i * QBLK))
    amask = jnp.where((d <= A_WINDOW) & (d >= -A_WINDOW), 0.0, NEG).astype(F32)
    amask = jnp.concatenate([amask] * group, axis=1)
    k_loc = ka_ref[0, pl.ds(start_a, A_SPAN), :]
    k_ctx = kac_ref[0]
    q_tiles = [qa_ref[0, qrows, t * LANES:(t + 1) * LANES] for t in range(group)]
    heads_a = []

    def scores_a(kv):
        qs = jnp.concatenate([q * half_masks[kv] for q in q_tiles], axis=0)
        return _scores_t(qs, k_loc, k_ctx, amask)

    def finish_a(kv, s_loc, s_ctx):
        dims = slice(kv * HEAD_DIM, (kv + 1) * HEAD_DIM)
        sink = jnp.concatenate(
            [jnp.full((1, QBLK), sink_ref[kv * group + t] * LOG2E, F32) for t in range(group)], axis=1)
        o_t = _softmax_pv_t(s_loc, s_ctx, _lane_tiles(vat_ref, blk_a, A_SPAN // LANES, dims),
                            _lane_tiles(vact_ref, 0, n_ctx_tiles, dims), sink)
        heads_a.extend(o_t[:, t * QBLK:(t + 1) * QBLK] for t in range(group))
        if kv == A_KV_HEADS - 1:
            o_ref[0, qrows, :A_WIDTH] = _group_norm_t(jnp.concatenate(heads_a, axis=0), ga_ref[...])

    stages += [(functools.partial(scores_a, kv), functools.partial(finish_a, kv)) for kv in range(A_KV_HEADS)]

    blk_b = jnp.clip(NA_ROW_BLOCK * i - NA_KH // 2, 0, rows - B_KROWS) // NA_ROW_BLOCK
    start_b = pl.multiple_of(blk_b * QBLK, QBLK)
    heads_b = []

    def scores_b(pair):
        cols = slice(pair * LANES, (pair + 1) * LANES)
        qp = qb_ref[0, qrows, cols]
        qs = jnp.concatenate([qp * half_masks[0], qp * half_masks[1]], axis=0)
        return _scores_t(qs, kb_ref[0, pl.ds(start_b, B_SPAN), cols], kbc_ref[0, :, cols], bias_ref[0, pair])

    def finish_b(pair, s_loc, s_ctx):
        cols = slice(pair * LANES, (pair + 1) * LANES)
        o_t = _softmax_pv_t(s_loc, s_ctx, _lane_tiles(vbt_ref, blk_b, B_SPAN // LANES, cols),
                            _lane_tiles(vbct_ref, 0, n_ctx_tiles, cols), None)
        heads_b.extend([o_t[:HEAD_DIM, :QBLK], o_t[HEAD_DIM:, QBLK:]])
        if pair == B_HEADS // 2 - 1:
            o_ref[0, qrows, A_WIDTH:] = _group_norm_t(jnp.concatenate(heads_b, axis=0), gb_ref[...])

    stages += [(functools.partial(scores_b, p), functools.partial(finish_b, p)) for p in range(B_HEADS // 2)]
    return stages


def _attn_call(sink, qa, qb, ka, vat, kb, vbt, kac, vact, kbc, vbct, bias, ga_t, gb_t):
    b, s, _ = qa.shape
    n_blk = s // QBLK
    assert (NA_KH // 2) % NA_ROW_BLOCK == 0 and (s // GRID_W - B_KROWS) % NA_ROW_BLOCK == 0

    def pat(i):
        return jnp.where(i < 2, i, jnp.where(i >= n_blk - 2, i - (n_blk - N_PATTERNS), 2))

    n_sub = ATTN_BLOCKS_PER_STEP
    full = lambda arr: pl.BlockSpec((1,) + arr.shape[1:], lambda bi, i: (bi,) + (0,) * (arr.ndim - 1))
    const = lambda arr: pl.BlockSpec(arr.shape, lambda bi, i: (0,) * arr.ndim)
    blk = lambda wd: pl.BlockSpec((1, n_sub * QBLK, wd), lambda bi, i: (bi, i, 0))
    bias_spec = lambda j: pl.BlockSpec((1,) + bias.shape[1:], lambda bi, i: (pat(i * n_sub + j), 0, 0, 0))
    return pl.pallas_call(
        functools.partial(_attn_kernel, seq=s, n_sub=n_sub),
        grid=(b, n_blk // n_sub),
        in_specs=[
            pl.BlockSpec(memory_space=pltpu.SMEM),
            blk(A_WIDTH), blk(B_WIDTH),
            full(ka), full(vat), full(kb), full(vbt), full(kac), full(vact), full(kbc), full(vbct),
            const(ga_t), const(gb_t),
        ] + [bias_spec(j) for j in range(n_sub)],
        out_specs=blk(A_WIDTH + B_WIDTH),
        out_shape=jax.ShapeDtypeStruct((b, s, A_WIDTH + B_WIDTH), BF16),
        compiler_params=pltpu.CompilerParams(dimension_semantics=("parallel", "arbitrary"),
                                             vmem_limit_bytes=VMEM_LIMIT),
        name="attn",
    )(sink, qa, qb, ka, vat, kb, vbt, kac, vact, kbc, vbct, ga_t, gb_t, *([bias] * n_sub))


def _ffn_kernel(x_ref, o_ref, g1_ref, sh2_ref, sc2_ref, g2_ref, n2_ref, wo_ref, wg_ref, wu_ref, wd_ref,
                out_ref):
    x1 = x_ref[0] + g1_ref[0] * _dot(o_ref[0], wo_ref[...])
    ms = jnp.mean(x1 * x1, axis=-1, keepdims=True)
    geff = n2_ref[...] * (1.0 + sc2_ref[0])
    h2 = ((x1 * lax.rsqrt(ms + EPS)) * geff + sh2_ref[0]).astype(BF16)
    gate = _dot(h2, wg_ref[...])
    up = _dot(h2, wu_ref[...])
    act = (_silu(gate) * up).astype(BF16)
    out_ref[0] = x1 + g2_ref[0] * _dot(act, wd_ref[...])


def _ffn_call(x, o, mod3, n2, wo, wg, wu, wd, tm):
    b, s, d = x.shape
    hid = wg.shape[1]
    modspec = lambda j: pl.BlockSpec((1, 1, d), lambda bi, i: (bi, 0, j))
    const = lambda shape: pl.BlockSpec(shape, lambda bi, i: (0, 0), pipeline_mode=pl.Buffered(1))
    return pl.pallas_call(
        _ffn_kernel,
        grid=(b, s // tm),
        in_specs=[
            pl.BlockSpec((1, tm, d), lambda bi, i: (bi, i, 0)),
            pl.BlockSpec((1, tm, o.shape[2]), lambda bi, i: (bi, i, 0)),
            modspec(2), modspec(3), modspec(4), modspec(5),
            pl.BlockSpec((1, d), lambda bi, i: (0, 0)),
            const((o.shape[2], d)), const((d, hid)), const((d, hid)), const((hid, d)),
        ],
        out_specs=pl.BlockSpec((1, tm, d), lambda bi, i: (bi, i, 0)),
        out_shape=jax.ShapeDtypeStruct((b, s, d), F32),
        compiler_params=pltpu.CompilerParams(dimension_semantics=("parallel", "arbitrary"),
                                             vmem_limit_bytes=VMEM_LIMIT),
        name="out_ffn",
    )(x, o, mod3, mod3, mod3, mod3, n2, wo, wg, wu, wd)


def _layer(x, ctx, mod, norm1_g, w_in, qn_a, kn_a, sink_a, qn_b, kn_b, rpb_b, on_a, on_b, w_out,
           norm2_g, w_gate, w_up, w_down, rope_tabs):
    b, s, d = x.shape
    rows = s // GRID_W
    group = A_HEADS // A_KV_HEADS
    perm_heads = np.array([h for t in range(group) for h in (t, t + group)])
    perm_cols = (perm_heads[:, None] * HEAD_DIM + np.arange(HEAD_DIM)[None, :]).reshape(-1)

    o_qa, o_ka, o_va, o_qb, o_kb, o_vb, o_end = np.cumsum(
        [0, A_WIDTH, A_KV_WIDTH, A_KV_WIDTH, B_WIDTH, B_WIDTH, B_WIDTH])
    w_qa, w_ka, w_va = w_in[:, o_qa:o_ka][:, perm_cols], w_in[:, o_ka:o_va], w_in[:, o_va:o_qb]
    w_qb, w_kb, w_vb = w_in[:, o_qb:o_kb], w_in[:, o_kb:o_vb], w_in[:, o_vb:o_end]
    w_l = jnp.concatenate([w_qa, w_ka, w_qb, w_kb], axis=1).astype(BF16)
    w_c = jnp.concatenate([w_ka, w_kb], axis=1).astype(BF16)
    wva_t, wvb_t = w_va.T.astype(BF16), w_vb.T.astype(BF16)
    scale = HEAD_DIM ** -0.5 * LOG2E
    gain_l = jnp.concatenate([jnp.tile(qn_a, A_HEADS) * scale, jnp.tile(kn_a, A_KV_HEADS),
                              jnp.tile(qn_b, B_HEADS) * scale, jnp.tile(kn_b, B_HEADS)])[None]
    gain_c = jnp.concatenate([jnp.tile(kn_a, A_KV_HEADS), jnp.tile(kn_b, B_HEADS)])[None]
    ones_bd = jnp.asarray(np.kron(np.eye(MXU_TILE // HEAD_DIM), np.ones((HEAD_DIM, HEAD_DIM))), BF16)

    mod3 = mod[:b].reshape(b, 1, -1)
    mod3_c = mod[b:b + 1].reshape(1, 1, -1)
    g1 = norm1_g[None]

    qka = A_WIDTH + A_KV_WIDTH
    groups_l = [
        (0, qka, [(0, A_WIDTH, True, 0), (A_WIDTH, A_KV_WIDTH, True, 1)]),
        (qka, B_WIDTH, [(0, B_WIDTH, False, 2)]),
        (qka + B_WIDTH, B_WIDTH, [(0, B_WIDTH, False, 3)]),
    ]
    qa, ka, qb, kb, vat, vbt = _inproj_call(
        x, mod3, True, g1, w_l, gain_l, ones_bd, wva_t, wvb_t, rope_tabs, groups_l,
        [A_WIDTH, A_KV_WIDTH, B_WIDTH, B_WIDTH], tm=512)
    groups_c = [(0, A_KV_WIDTH + B_WIDTH, [(0, A_KV_WIDTH, False, 0), (A_KV_WIDTH, B_WIDTH, False, 1)])]
    kac, kbc, vact, vbct = _inproj_call(
        ctx, mod3_c, False, g1, w_c, gain_c, ones_bd, wva_t, wvb_t, None, groups_c,
        [A_KV_WIDTH, B_WIDTH], tm=ctx.shape[1])

    bias = _bias_call(rpb_b, rows)
    lanes_of = lambda g: jnp.broadcast_to(g[:, None], (g.shape[0], LANES))
    o = _attn_call(sink_a, qa, qb, ka, vat, kb, vbt, kac, vact, kbc, vbct, bias, lanes_of(on_a), lanes_of(on_b))
    return _ffn_call(x, o, mod3, norm2_g[None], w_out.astype(BF16), w_gate.astype(BF16), w_up.astype(BF16),
                     w_down.astype(BF16), tm=256)


def kernel(x, c, ctx, c_ctx, w_mod, b_mod, norm1_g, w_in, qn_a, kn_a, sink_a, qn_b, kn_b, rpb_b, on_a, on_b,
           w_out, norm2_g, w_gate, w_up, w_down):
    b, s, d = x.shape
    depth = w_mod.shape[0]
    assert depth == 1, "the context stream update is only needed when a later layer reads it"
    rope_tabs = _rope_call(s, tm=512)
    pad = jnp.zeros((16 - b - 1, d), F32)
    cc = jnp.concatenate([c, c_ctx[None], pad], axis=0)
    l = 0
    mod = _mod_call(cc, w_mod[l], b_mod[l][None])
    return _layer(x, ctx, mod, norm1_g[l], w_in[l], qn_a[l], kn_a[l], sink_a[l], qn_b[l], kn_b[l], rpb_b[l],
                  on_a[l], on_b[l], w_out[l], norm2_g[l], w_gate[l], w_up[l], w_down[l], rope_tabs)
```

```python
import functools

import numpy as np
import jax
import jax.numpy as jnp
from jax import lax
from jax.experimental import pallas as pl
from jax.experimental.pallas import tpu as pltpu

GRID_W = 64
HEAD_DIM = 64
A_HEADS = 8
A_KV_HEADS = 2
A_WINDOW = 128
B_HEADS = 8
NA_KH = 8
NA_KW = 16
NA_ROW_BLOCK = 2
N_MOD = 6
ROPE_BASE = 10000.0
EPS = 1e-6

A_WIDTH = A_HEADS * HEAD_DIM
A_KV_WIDTH = A_KV_HEADS * HEAD_DIM
B_WIDTH = B_HEADS * HEAD_DIM
QBLK = NA_ROW_BLOCK * GRID_W
A_SPAN = QBLK + 2 * A_WINDOW
B_KROWS = 10
B_SPAN = B_KROWS * GRID_W
N_PATTERNS = 5
ATTN_BLOCKS_PER_STEP = 4
ATTN_LOOKAHEAD = 2
NEG = -1e30
LOG2E = 1.4426950408889634

LANES = 128
BF16_ROWS = 16
MXU_TILE = 256
VMEM_LIMIT = 56 * 1024 * 1024

F32 = jnp.float32
BF16 = jnp.bfloat16


def _dot(a, b):
    return jnp.dot(a, b, preferred_element_type=F32)


def _dot_nt(a, b):
    return lax.dot_general(a, b, (((1,), (1,)), ((), ())), preferred_element_type=F32)


def _silu(x):
    return x / (1.0 + jnp.exp(-x))


def _mod_kernel(c_ref, w_ref, b_ref, o_ref):
    a = _silu(c_ref[...])
    o_ref[...] = jnp.dot(a, w_ref[...], precision=lax.Precision.HIGHEST,
                         preferred_element_type=F32) + b_ref[...]


def _mod_call(cc, w_mod, b_mod):
    rows, d = cc.shape
    n = w_mod.shape[1]
    return pl.pallas_call(
        _mod_kernel,
        grid=(n // d,),
        in_specs=[pl.BlockSpec((rows, d), lambda j: (0, 0)),
                  pl.BlockSpec((d, d), lambda j: (0, j)),
                  pl.BlockSpec((1, d), lambda j: (0, j))],
        out_specs=pl.BlockSpec((rows, d), lambda j: (0, j)),
        out_shape=jax.ShapeDtypeStruct((rows, n), F32),
        name="mod",
    )(cc, w_mod, b_mod)


def _rope_kernel(invf_ref, cos_ref, sin_ref, *, tm):
    t = pl.program_id(0) * tm + lax.broadcasted_iota(jnp.int32, (tm, LANES), 0)
    lane = lax.broadcasted_iota(jnp.int32, (tm, LANES), 1)
    l64 = lane & (HEAD_DIM - 1)
    pos = jnp.where(l64 < HEAD_DIM // 2, t // GRID_W, t % GRID_W).astype(F32)
    ang = pos * invf_ref[...]
    first = (lane & (HEAD_DIM // 2 - 1)) < HEAD_DIM // 4
    cos_ref[...] = jnp.cos(ang)
    sin_ref[...] = jnp.where(first, -jnp.sin(ang), jnp.sin(ang))


def _rope_call(seq, tm):
    quarter = HEAD_DIM // 4
    inv = (1.0 / (np.float32(ROPE_BASE) ** (np.arange(quarter, dtype=np.float32) / quarter))).astype(np.float32)
    invf = jnp.asarray(np.tile(inv, LANES // quarter)[None, :])
    return pl.pallas_call(
        functools.partial(_rope_kernel, tm=tm),
        grid=(seq // tm,),
        in_specs=[pl.BlockSpec((1, LANES), lambda i: (0, 0))],
        out_specs=[pl.BlockSpec((tm, LANES), lambda i: (i, 0))] * 2,
        out_shape=[jax.ShapeDtypeStruct((seq, LANES), F32)] * 2,
        name="rope_tables",
    )(invf)


def _b_patterns(rows):
    n_blk = rows // NA_ROW_BLOCK
    blocks = [0, 1, 2, n_blk - 2, n_blk - 1]
    return [(NA_ROW_BLOCK * i, _b_key_start(i, rows)) for i in blocks]


def _b_key_start(i, rows):
    return min(max(NA_ROW_BLOCK * i - NA_KH // 2, 0), rows - B_KROWS)


def _bias_kernel(rpb_ref, o_ref, *, patterns, rows):
    h = pl.program_id(0)
    n_dr, n_dc = 2 * NA_KH - 1, 2 * NA_KW - 1
    kc = lax.broadcasted_iota(jnp.int32, (GRID_W, LANES), 0)
    lane = lax.broadcasted_iota(jnp.int32, (GRID_W, LANES), 1)
    qc = lane & (GRID_W - 1)
    dc = jnp.clip(kc - qc + NA_KW - 1, 0, n_dc - 1)
    cs = jnp.clip(qc - NA_KW // 2, 0, GRID_W - NA_KW)
    col_ok = (kc >= cs) & (kc < cs + NA_KW)
    lo = lane < GRID_W
    base = h * (n_dr * n_dc)
    per_dr = []
    for dr in range(n_dr):
        m = jnp.zeros((GRID_W, LANES), F32)
        for d in range(n_dc):
            m = jnp.where(dc == d, rpb_ref[base + dr * n_dc + d] * LOG2E, m)
        per_dr.append(m)
    neg = jnp.full((GRID_W, LANES), NEG, F32)
    for p, (r0, ks) in enumerate(patterns):
        q_rows = [r0 + qr for qr in range(NA_ROW_BLOCK)]
        rs = [min(max(q - NA_KH // 2, 0), rows - NA_KH) for q in q_rows]
        for kr in range(B_KROWS):
            k_row = ks + kr
            ok = [r <= k_row < r + NA_KH for r in rs]
            drs = [min(max(k_row - q + NA_KH - 1, 0), n_dr - 1) for q in q_rows]
            if not any(ok):
                piece = neg
            else:
                vals = jnp.where(lo, per_dr[drs[0]], per_dr[drs[1]])
                mask = col_ok
                if not ok[1]:
                    mask = mask & lo
                if not ok[0]:
                    mask = mask & jnp.logical_not(lo)
                piece = jnp.where(mask, vals, neg)
            o_ref[p, 0, kr * GRID_W:(kr + 1) * GRID_W, :] = piece


def _bias_call(rpb, rows):
    patterns = _b_patterns(rows)
    heads = rpb.shape[0]
    return pl.pallas_call(
        functools.partial(_bias_kernel, patterns=patterns, rows=rows),
        grid=(heads,),
        in_specs=[pl.BlockSpec(memory_space=pltpu.SMEM)],
        out_specs=pl.BlockSpec((N_PATTERNS, 1, B_SPAN, QBLK), lambda h: (0, h // 2, 0, h % 2)),
        out_shape=jax.ShapeDtypeStruct((N_PATTERNS, heads // 2, B_SPAN, 2 * QBLK), F32),
        name="na_bias",
    )(rpb.reshape(-1))


def _group_rms(p, ones_ref):
    sq = (p * p).astype(BF16)
    n = p.shape[1]
    if n <= MXU_TILE:
        ssq = _dot(sq, ones_ref[:n, :n])
    else:
        ssq = jnp.concatenate(
            [_dot(sq[:, c:c + MXU_TILE], ones_ref[...]) for c in range(0, n, MXU_TILE)], axis=1)
    return lax.rsqrt(ssq * (1.0 / HEAD_DIM) + EPS)


def _rope(x, cos, sin):
    lane = lax.broadcasted_iota(jnp.int32, (x.shape[0], LANES), 1)
    first = (lane & (HEAD_DIM // 2 - 1)) < HEAD_DIM // 4
    outs = []
    for c in range(0, x.shape[1], LANES):
        xt = x[:, c:c + LANES]
        sw = jnp.where(first, pltpu.roll(xt, LANES - HEAD_DIM // 4, 1), pltpu.roll(xt, HEAD_DIM // 4, 1))
        outs.append(xt * cos + sw * sin)
    return outs[0] if len(outs) == 1 else jnp.concatenate(outs, axis=1)


def _inproj_kernel(x_ref, sh_ref, sc_ref, g_ref, w_ref, gain_ref, ones_ref, wva_ref, wvb_ref, *rest,
                   groups, use_rope):
    if use_rope:
        cos, sin = rest[0][...], rest[1][...]
        rest = rest[2:]
    n_std = sum(len(subs) for _, _, subs in groups)
    out_refs, vt_refs = rest[:n_std], rest[n_std:]
    x = x_ref[0]
    ms = jnp.mean(x * x, axis=-1, keepdims=True)
    geff = g_ref[...] * (1.0 + sc_ref[0])
    h = ((x * lax.rsqrt(ms + EPS)) * geff + sh_ref[0]).astype(BF16)
    for c0, width, subs in groups:
        p = _dot(h, w_ref[:, c0:c0 + width])
        for s0, sw, rope, oi in subs:
            y = p[:, s0:s0 + sw]
            y = y * _group_rms(y, ones_ref) * gain_ref[:, c0 + s0:c0 + s0 + sw]
            if rope:
                y = _rope(y, cos, sin)
            out_refs[oi][0] = y.astype(BF16)
    for wv_ref, vt_ref in zip((wva_ref, wvb_ref), vt_refs):
        vt = _dot_nt(wv_ref[...], h).astype(BF16)
        for j in range(vt.shape[1] // LANES):
            vt_ref[0, j] = vt[:, j * LANES:(j + 1) * LANES]


def _inproj_call(x, mod3, mod_batched, g, w, gain, ones, wva_t, wvb_t, rope_tabs, groups, out_widths, tm):
    b, s, d = x.shape
    n = w.shape[1]
    use_rope = rope_tabs is not None
    mod_idx = (lambda bi, i, j: (bi, 0, j)) if mod_batched else (lambda bi, i, j: (0, 0, j))
    const = lambda arr: pl.BlockSpec(arr.shape, lambda bi, i: (0, 0))
    in_specs = [
        pl.BlockSpec((1, tm, d), lambda bi, i: (bi, i, 0)),
        pl.BlockSpec((1, 1, d), lambda bi, i: mod_idx(bi, i, 0)),
        pl.BlockSpec((1, 1, d), lambda bi, i: mod_idx(bi, i, 1)),
        const(g), const(w), const(gain), const(ones), const(wva_t), const(wvb_t),
    ]
    args = [x, mod3, mod3, g, w, gain, ones, wva_t, wvb_t]
    if use_rope:
        in_specs += [pl.BlockSpec((tm, LANES), lambda bi, i: (i, 0))] * 2
        args += list(rope_tabs)
    vt_dims = [wva_t.shape[0], wvb_t.shape[0]]
    return pl.pallas_call(
        functools.partial(_inproj_kernel, groups=groups, use_rope=use_rope),
        grid=(b, s // tm),
        in_specs=in_specs,
        out_specs=([pl.BlockSpec((1, tm, wd), lambda bi, i: (bi, i, 0)) for wd in out_widths]
                   + [pl.BlockSpec((1, tm // LANES, vd, LANES), lambda bi, i: (bi, i, 0, 0)) for vd in vt_dims]),
        out_shape=([jax.ShapeDtypeStruct((b, s, wd), BF16) for wd in out_widths]
                   + [jax.ShapeDtypeStruct((b, s // LANES, vd, LANES), BF16) for vd in vt_dims]),
        compiler_params=pltpu.CompilerParams(dimension_semantics=("parallel", "arbitrary"),
                                             vmem_limit_bytes=VMEM_LIMIT),
        name="in_proj_rope" if use_rope else "in_proj_ctx",
    )(*args)


def _scores_t(qs, k_loc, k_ctx, add_loc):
    return _dot_nt(k_loc, qs) + add_loc, _dot_nt(k_ctx, qs)


def _softmax_pv_t(s_loc, s_ctx, vt_loc, vt_ctx, extra_logit):
    m = jnp.maximum(jnp.max(s_loc, axis=0, keepdims=True), jnp.max(s_ctx, axis=0, keepdims=True))
    if extra_logit is not None:
        m = jnp.maximum(m, extra_logit)
    n_v = vt_loc.shape[0]
    with_ones = lambda vt: jnp.concatenate([vt, jnp.ones((BF16_ROWS, vt.shape[1]), BF16)], axis=0)
    acc = (_dot(with_ones(vt_loc), jnp.exp2(s_loc - m).astype(BF16))
           + _dot(with_ones(vt_ctx), jnp.exp2(s_ctx - m).astype(BF16)))
    l = acc[n_v:n_v + 1]
    if extra_logit is not None:
        l = l + jnp.exp2(extra_logit - m)
    return acc[:n_v] * (1.0 / l)


def _group_norm_t(o_t, g_t):
    ms = jnp.mean(o_t * o_t, axis=0, keepdims=True)
    return ((o_t * lax.rsqrt(ms + EPS)) * g_t).T.astype(BF16)


def _lane_tiles(ref, first, n, rows=slice(None)):
    return jnp.concatenate([ref[0, first + j, rows, :] for j in range(n)], axis=1)


def _attn_kernel(sink_ref, qa_ref, qb_ref, ka_ref, vat_ref, kb_ref, vbt_ref, kac_ref, vact_ref,
                 kbc_ref, vbct_ref, ga_ref, gb_ref, *rest, seq, n_sub):
    bias_refs, o_ref = rest[:n_sub], rest[n_sub]
    groups = []
    for j in range(n_sub):
        groups += _attn_block_groups(
            pl.program_id(1) * n_sub + j, slice(j * QBLK, (j + 1) * QBLK), sink_ref, qa_ref, qb_ref, ka_ref,
            vat_ref, kb_ref, vbt_ref, kac_ref, vact_ref, kbc_ref, vbct_ref, ga_ref, gb_ref, bias_refs[j], o_ref, seq)
    pending = []
    for scores, finish in groups:
        pending.append((finish, scores()))
        if len(pending) > ATTN_LOOKAHEAD:
            fin, s = pending.pop(0)
            fin(*s)
    for fin, s in pending:
        fin(*s)


def _attn_block_groups(i, qrows, sink_ref, qa_ref, qb_ref, ka_ref, vat_ref, kb_ref, vbt_ref, kac_ref, vact_ref,
                       kbc_ref, vbct_ref, ga_ref, gb_ref, bias_ref, o_ref, seq):
    rows = seq // GRID_W
    stages = []
    n_ctx_tiles = kac_ref.shape[1] // LANES
    lane = lax.broadcasted_iota(jnp.int32, (QBLK, LANES), 1)
    lo = lane < HEAD_DIM
    half_masks = (lo.astype(F32).astype(BF16), jnp.logical_not(lo).astype(F32).astype(BF16))

    group = A_HEADS // A_KV_HEADS
    blk_a = jnp.clip(i - 1, 0, seq // QBLK - A_SPAN // QBLK)
    start_a = pl.multiple_of(blk_a * QBLK, QBLK)
    d = (lax.broadcasted_iota(jnp.int32, (A_SPAN, QBLK), 0)
         - lax.broadcasted_iota(jnp.int32, (A_SPAN, QBLK), 1) + (start_a - i * QBLK))
    amask = jnp.where((d <= A_WINDOW) & (d >= -A_WINDOW), 0.0, NEG).astype(F32)
    amask = jnp.concatenate([amask] * group, axis=1)
    k_loc = ka_ref[0, pl.ds(start_a, A_SPAN), :]
    k_ctx = kac_ref[0]
    q_tiles = [qa_ref[0, qrows, t * LANES:(t + 1) * LANES] for t in range(group)]
    heads_a = []

    def scores_a(kv):
        qs = jnp.concatenate([q * half_masks[kv] for q in q_tiles], axis=0)
        return _scores_t(qs, k_loc, k_ctx, amask)

    def finish_a(kv, s_loc, s_ctx):
        dims = slice(kv * HEAD_DIM, (kv + 1) * HEAD_DIM)
        sink = jnp.concatenate(
            [jnp.full((1, QBLK), sink_ref[kv * group + t] * LOG2E, F32) for t in range(group)], axis=1)
        o_t = _softmax_pv_t(s_loc, s_ctx, _lane_tiles(vat_ref, blk_a, A_SPAN // LANES, dims),
                            _lane_tiles(vact_ref, 0, n_ctx_tiles, dims), sink)
        heads_a.extend(o_t[:, t * QBLK:(t + 1) * QBLK] for t in range(group))
        if kv == A_KV_HEADS - 1:
            o_ref[0, qrows, :A_WIDTH] = _group_norm_t(jnp.concatenate(heads_a, axis=0), ga_ref[...])

    stages += [(functools.partial(scores_a, kv), functools.partial(finish_a, kv)) for kv in range(A_KV_HEADS)]

    blk_b = jnp.clip(NA_ROW_BLOCK * i - NA_KH // 2, 0, rows - B_KROWS) // NA_ROW_BLOCK
    start_b = pl.multiple_of(blk_b * QBLK, QBLK)
    heads_b = []

    def scores_b(pair):
        cols = slice(pair * LANES, (pair + 1) * LANES)
        qp = qb_ref[0, qrows, cols]
        qs = jnp.concatenate([qp * half_masks[0], qp * half_masks[1]], axis=0)
        return _scores_t(qs, kb_ref[0, pl.ds(start_b, B_SPAN), cols], kbc_ref[0, :, cols], bias_ref[0, pair])

    def finish_b(pair, s_loc, s_ctx):
        cols = slice(pair * LANES, (pair + 1) * LANES)
        o_t = _softmax_pv_t(s_loc, s_ctx, _lane_tiles(vbt_ref, blk_b, B_SPAN // LANES, cols),
                            _lane_tiles(vbct_ref, 0, n_ctx_tiles, cols), None)
        heads_b.extend([o_t[:HEAD_DIM, :QBLK], o_t[HEAD_DIM:, QBLK:]])
        if pair == B_HEADS // 2 - 1:
            o_ref[0, qrows, A_WIDTH:] = _group_norm_t(jnp.concatenate(heads_b, axis=0), gb_ref[...])

    stages += [(functools.partial(scores_b, p), functools.partial(finish_b, p)) for p in range(B_HEADS // 2)]
    return stages


def _attn_call(sink, qa, qb, ka, vat, kb, vbt, kac, vact, kbc, vbct, bias, ga_t, gb_t):
    b, s, _ = qa.shape
    n_blk = s // QBLK
    assert (NA_KH // 2) % NA_ROW_BLOCK == 0 and (s // GRID_W - B_KROWS) % NA_ROW_BLOCK == 0

    def pat(i):
        return jnp.where(i < 2, i, jnp.where(i >= n_blk - 2, i - (n_blk - N_PATTERNS), 2))

    n_sub = ATTN_BLOCKS_PER_STEP
    full = lambda arr: pl.BlockSpec((1,) + arr.shape[1:], lambda bi, i: (bi,) + (0,) * (arr.ndim - 1))
    const = lambda arr: pl.BlockSpec(arr.shape, lambda bi, i: (0,) * arr.ndim)
    blk = lambda wd: pl.BlockSpec((1, n_sub * QBLK, wd), lambda bi, i: (bi, i, 0))
    bias_spec = lambda j: pl.BlockSpec((1,) + bias.shape[1:], lambda bi, i: (pat(i * n_sub + j), 0, 0, 0))
    return pl.pallas_call(
        functools.partial(_attn_kernel, seq=s, n_sub=n_sub),
        grid=(b, n_blk // n_sub),
        in_specs=[
            pl.BlockSpec(memory_space=pltpu.SMEM),
            blk(A_WIDTH), blk(B_WIDTH),
            full(ka), full(vat), full(kb), full(vbt), full(kac), full(vact), full(kbc), full(vbct),
            const(ga_t), const(gb_t),
        ] + [bias_spec(j) for j in range(n_sub)],
        out_specs=blk(A_WIDTH + B_WIDTH),
        out_shape=jax.ShapeDtypeStruct((b, s, A_WIDTH + B_WIDTH), BF16),
        compiler_params=pltpu.CompilerParams(dimension_semantics=("parallel", "arbitrary"),
                                             vmem_limit_bytes=VMEM_LIMIT),
        name="attn",
    )(sink, qa, qb, ka, vat, kb, vbt, kac, vact, kbc, vbct, ga_t, gb_t, *([bias] * n_sub))


def _ffn_kernel(x_ref, o_ref, g1_ref, sh2_ref, sc2_ref, g2_ref, n2_ref, wo_ref, wg_ref, wu_ref, wd_ref,
                out_ref):
    x1 = x_ref[0] + g1_ref[0] * _dot(o_ref[0], wo_ref[...])
    ms = jnp.mean(x1 * x1, axis=-1, keepdims=True)
    geff = n2_ref[...] * (1.0 + sc2_ref[0])
    h2 = ((x1 * lax.rsqrt(ms + EPS)) * geff + sh2_ref[0]).astype(BF16)
    gate = _dot(h2, wg_ref[...])
    up = _dot(h2, wu_ref[...])
    act = (_silu(gate) * up).astype(BF16)
    out_ref[0] = x1 + g2_ref[0] * _dot(act, wd_ref[...])


def _ffn_call(x, o, mod3, n2, wo, wg, wu, wd, tm):
    b, s, d = x.shape
    hid = wg.shape[1]
    modspec = lambda j: pl.BlockSpec((1, 1, d), lambda bi, i: (bi, 0, j))
    const = lambda shape: pl.BlockSpec(shape, lambda bi, i: (0, 0), pipeline_mode=pl.Buffered(1))
    return pl.pallas_call(
        _ffn_kernel,
        grid=(b, s // tm),
        in_specs=[
            pl.BlockSpec((1, tm, d), lambda bi, i: (bi, i, 0)),
            pl.BlockSpec((1, tm, o.shape[2]), lambda bi, i: (bi, i, 0)),
            modspec(2), modspec(3), modspec(4), modspec(5),
            pl.BlockSpec((1, d), lambda bi, i: (0, 0)),
            const((o.shape[2], d)), const((d, hid)), const((d, hid)), const((hid, d)),
        ],
        out_specs=pl.BlockSpec((1, tm, d), lambda bi, i: (bi, i, 0)),
        out_shape=jax.ShapeDtypeStruct((b, s, d), F32),
        compiler_params=pltpu.CompilerParams(dimension_semantics=("parallel", "arbitrary"),
                                             vmem_limit_bytes=VMEM_LIMIT),
        name="out_ffn",
    )(x, o, mod3, mod3, mod3, mod3, n2, wo, wg, wu, wd)


def _layer(x, ctx, mod, norm1_g, w_in, qn_a, kn_a, sink_a, qn_b, kn_b, rpb_b, on_a, on_b, w_out,
           norm2_g, w_gate, w_up, w_down, rope_tabs):
    b, s, d = x.shape
    rows = s // GRID_W
    group = A_HEADS // A_KV_HEADS
    perm_heads = np.array([h for t in range(group) for h in (t, t + group)])
    perm_cols = (perm_heads[:, None] * HEAD_DIM + np.arange(HEAD_DIM)[None, :]).reshape(-1)

    o_qa, o_ka, o_va, o_qb, o_kb, o_vb, o_end = np.cumsum(
        [0, A_WIDTH, A_KV_WIDTH, A_KV_WIDTH, B_WIDTH, B_WIDTH, B_WIDTH])
    w_qa, w_ka, w_va = w_in[:, o_qa:o_ka][:, perm_cols], w_in[:, o_ka:o_va], w_in[:, o_va:o_qb]
    w_qb, w_kb, w_vb = w_in[:, o_qb:o_kb], w_in[:, o_kb:o_vb], w_in[:, o_vb:o_end]
    w_l = jnp.concatenate([w_qa, w_ka, w_qb, w_kb], axis=1).astype(BF16)
    w_c = jnp.concatenate([w_ka, w_kb], axis=1).astype(BF16)
    wva_t, wvb_t = w_va.T.astype(BF16), w_vb.T.astype(BF16)
    scale = HEAD_DIM ** -0.5 * LOG2E
    gain_l = jnp.concatenate([jnp.tile(qn_a, A_HEADS) * scale, jnp.tile(kn_a, A_KV_HEADS),
                              jnp.tile(qn_b, B_HEADS) * scale, jnp.tile(kn_b, B_HEADS)])[None]
    gain_c = jnp.concatenate([jnp.tile(kn_a, A_KV_HEADS), jnp.tile(kn_b, B_HEADS)])[None]
    ones_bd = jnp.asarray(np.kron(np.eye(MXU_TILE // HEAD_DIM), np.ones((HEAD_DIM, HEAD_DIM))), BF16)

    mod3 = mod[:b].reshape(b, 1, -1)
    mod3_c = mod[b:b + 1].reshape(1, 1, -1)
    g1 = norm1_g[None]

    qka = A_WIDTH + A_KV_WIDTH
    groups_l = [
        (0, qka, [(0, A_WIDTH, True, 0), (A_WIDTH, A_KV_WIDTH, True, 1)]),
        (qka, B_WIDTH, [(0, B_WIDTH, False, 2)]),
        (qka + B_WIDTH, B_WIDTH, [(0, B_WIDTH, False, 3)]),
    ]
    qa, ka, qb, kb, vat, vbt = _inproj_call(
        x, mod3, True, g1, w_l, gain_l, ones_bd, wva_t, wvb_t, rope_tabs, groups_l,
        [A_WIDTH, A_KV_WIDTH, B_WIDTH, B_WIDTH], tm=512)
    groups_c = [(0, A_KV_WIDTH + B_WIDTH, [(0, A_KV_WIDTH, False, 0), (A_KV_WIDTH, B_WIDTH, False, 1)])]
    kac, kbc, vact, vbct = _inproj_call(
        ctx, mod3_c, False, g1, w_c, gain_c, ones_bd, wva_t, wvb_t, None, groups_c,
        [A_KV_WIDTH, B_WIDTH], tm=ctx.shape[1])

    bias = _bias_call(rpb_b, rows)
    lanes_of = lambda g: jnp.broadcast_to(g[:, None], (g.shape[0], LANES))
    o = _attn_call(sink_a, qa, qb, ka, vat, kb, vbt, kac, vact, kbc, vbct, bias, lanes_of(on_a), lanes_of(on_b))
    return _ffn_call(x, o, mod3, norm2_g[None], w_out.astype(BF16), w_gate.astype(BF16), w_up.astype(BF16),
                     w_down.astype(BF16), tm=256)


def kernel(x, c, ctx, c_ctx, w_mod, b_mod, norm1_g, w_in, qn_a, kn_a, sink_a, qn_b, kn_b, rpb_b, on_a, on_b,
           w_out, norm2_g, w_gate, w_up, w_down):
    b, s, d = x.shape
    depth = w_mod.shape[0]
    assert depth == 1, "the context stream update is only needed when a later layer reads it"
    rope_tabs = _rope_call(s, tm=512)
    pad = jnp.zeros((16 - b - 1, d), F32)
    cc = jnp.concatenate([c, c_ctx[None], pad], axis=0)
    l = 0
    mod = _mod_call(cc, w_mod[l], b_mod[l][None])
    return _layer(x, ctx, mod, norm1_g[l], w_in[l], qn_a[l], kn_a[l], sink_a[l], qn_b[l], kn_b[l], rpb_b[l],
                  on_a[l], on_b[l], w_out[l], norm2_g[l], w_gate[l], w_up[l], w_down[l], rope_tabs)
```

```python
import functools

import numpy as np
import jax
import jax.numpy as jnp
from jax import lax
from jax.experimental import pallas as pl
from jax.experimental.pallas import tpu as pltpu

GRID_W = 64
HEAD_DIM = 64
A_HEADS = 8
A_KV_HEADS = 2
A_WINDOW = 128
B_HEADS = 8
NA_KH = 8
NA_KW = 16
NA_ROW_BLOCK = 2
N_MOD = 6
ROPE_BASE = 10000.0
EPS = 1e-6

A_WIDTH = A_HEADS * HEAD_DIM
A_KV_WIDTH = A_KV_HEADS * HEAD_DIM
B_WIDTH = B_HEADS * HEAD_DIM
QBLK = NA_ROW_BLOCK * GRID_W
A_SPAN = QBLK + 2 * A_WINDOW
B_KROWS = 10
B_SPAN = B_KROWS * GRID_W
N_PATTERNS = 5
ATTN_BLOCKS_PER_STEP = 4
INPROJ_ROWS = 1024
INPROJ_SUB_ROWS = 512
FFN_ROWS = 512
FFN_SUB_ROWS = 256
ATTN_LOOKAHEAD = 2
NEG = -1e30
LOG2E = 1.4426950408889634

LANES = 128
BF16_ROWS = 16
MXU_TILE = 256
VMEM_LIMIT = 56 * 1024 * 1024

F32 = jnp.float32
BF16 = jnp.bfloat16


def _dot(a, b):
    return jnp.dot(a, b, preferred_element_type=F32)


def _dot_nt(a, b):
    return lax.dot_general(a, b, (((1,), (1,)), ((), ())), preferred_element_type=F32)


def _silu(x):
    return x / (1.0 + jnp.exp(-x))


def _mod_kernel(c_ref, w_ref, b_ref, o_ref):
    a = _silu(c_ref[...])
    o_ref[...] = jnp.dot(a, w_ref[...], precision=lax.Precision.HIGHEST,
                         preferred_element_type=F32) + b_ref[...]


def _mod_call(cc, w_mod, b_mod):
    rows, d = cc.shape
    n = w_mod.shape[1]
    return pl.pallas_call(
        _mod_kernel,
        grid=(n // d,),
        in_specs=[pl.BlockSpec((rows, d), lambda j: (0, 0)),
                  pl.BlockSpec((d, d), lambda j: (0, j)),
                  pl.BlockSpec((1, d), lambda j: (0, j))],
        out_specs=pl.BlockSpec((rows, d), lambda j: (0, j)),
        out_shape=jax.ShapeDtypeStruct((rows, n), F32),
        name="mod",
    )(cc, w_mod, b_mod)


def _rope_kernel(invf_ref, cos_ref, sin_ref, *, tm):
    t = pl.program_id(0) * tm + lax.broadcasted_iota(jnp.int32, (tm, LANES), 0)
    lane = lax.broadcasted_iota(jnp.int32, (tm, LANES), 1)
    l64 = lane & (HEAD_DIM - 1)
    pos = jnp.where(l64 < HEAD_DIM // 2, t // GRID_W, t % GRID_W).astype(F32)
    ang = pos * invf_ref[...]
    first = (lane & (HEAD_DIM // 2 - 1)) < HEAD_DIM // 4
    cos_ref[...] = jnp.cos(ang)
    sin_ref[...] = jnp.where(first, -jnp.sin(ang), jnp.sin(ang))


def _rope_call(seq, tm):
    quarter = HEAD_DIM // 4
    inv = (1.0 / (np.float32(ROPE_BASE) ** (np.arange(quarter, dtype=np.float32) / quarter))).astype(np.float32)
    invf = jnp.asarray(np.tile(inv, LANES // quarter)[None, :])
    return pl.pallas_call(
        functools.partial(_rope_kernel, tm=tm),
        grid=(seq // tm,),
        in_specs=[pl.BlockSpec((1, LANES), lambda i: (0, 0))],
        out_specs=[pl.BlockSpec((tm, LANES), lambda i: (i, 0))] * 2,
        out_shape=[jax.ShapeDtypeStruct((seq, LANES), F32)] * 2,
        name="rope_tables",
    )(invf)


def _b_patterns(rows):
    n_blk = rows // NA_ROW_BLOCK
    blocks = [0, 1, 2, n_blk - 2, n_blk - 1]
    return [(NA_ROW_BLOCK * i, _b_key_start(i, rows)) for i in blocks]


def _b_key_start(i, rows):
    return min(max(NA_ROW_BLOCK * i - NA_KH // 2, 0), rows - B_KROWS)


def _bias_kernel(rpb_ref, o_ref, *, patterns, rows):
    h = pl.program_id(0)
    n_dr, n_dc = 2 * NA_KH - 1, 2 * NA_KW - 1
    kc = lax.broadcasted_iota(jnp.int32, (GRID_W, LANES), 0)
    lane = lax.broadcasted_iota(jnp.int32, (GRID_W, LANES), 1)
    qc = lane & (GRID_W - 1)
    dc = jnp.clip(kc - qc + NA_KW - 1, 0, n_dc - 1)
    cs = jnp.clip(qc - NA_KW // 2, 0, GRID_W - NA_KW)
    col_ok = (kc >= cs) & (kc < cs + NA_KW)
    lo = lane < GRID_W
    base = h * (n_dr * n_dc)
    per_dr = []
    for dr in range(n_dr):
        m = jnp.zeros((GRID_W, LANES), F32)
        for d in range(n_dc):
            m = jnp.where(dc == d, rpb_ref[base + dr * n_dc + d] * LOG2E, m)
        per_dr.append(m)
    neg = jnp.full((GRID_W, LANES), NEG, F32)
    for p, (r0, ks) in enumerate(patterns):
        q_rows = [r0 + qr for qr in range(NA_ROW_BLOCK)]
        rs = [min(max(q - NA_KH // 2, 0), rows - NA_KH) for q in q_rows]
        for kr in range(B_KROWS):
            k_row = ks + kr
            ok = [r <= k_row < r + NA_KH for r in rs]
            drs = [min(max(k_row - q + NA_KH - 1, 0), n_dr - 1) for q in q_rows]
            if not any(ok):
                piece = neg
            else:
                vals = jnp.where(lo, per_dr[drs[0]], per_dr[drs[1]])
                mask = col_ok
                if not ok[1]:
                    mask = mask & lo
                if not ok[0]:
                    mask = mask & jnp.logical_not(lo)
                piece = jnp.where(mask, vals, neg)
            o_ref[p, 0, kr * GRID_W:(kr + 1) * GRID_W, :] = piece


def _bias_call(rpb, rows):
    patterns = _b_patterns(rows)
    heads = rpb.shape[0]
    return pl.pallas_call(
        functools.partial(_bias_kernel, patterns=patterns, rows=rows),
        grid=(heads,),
        in_specs=[pl.BlockSpec(memory_space=pltpu.SMEM)],
        out_specs=pl.BlockSpec((N_PATTERNS, 1, B_SPAN, QBLK), lambda h: (0, h // 2, 0, h % 2)),
        out_shape=jax.ShapeDtypeStruct((N_PATTERNS, heads // 2, B_SPAN, 2 * QBLK), F32),
        name="na_bias",
    )(rpb.reshape(-1))


def _group_rms(p, ones_ref):
    sq = (p * p).astype(BF16)
    n = p.shape[1]
    if n <= MXU_TILE:
        ssq = _dot(sq, ones_ref[:n, :n])
    else:
        ssq = jnp.concatenate(
            [_dot(sq[:, c:c + MXU_TILE], ones_ref[...]) for c in range(0, n, MXU_TILE)], axis=1)
    return lax.rsqrt(ssq * (1.0 / HEAD_DIM) + EPS)


def _rope(x, cos, sin):
    lane = lax.broadcasted_iota(jnp.int32, (x.shape[0], LANES), 1)
    first = (lane & (HEAD_DIM // 2 - 1)) < HEAD_DIM // 4
    outs = []
    for c in range(0, x.shape[1], LANES):
        xt = x[:, c:c + LANES]
        sw = jnp.where(first, pltpu.roll(xt, LANES - HEAD_DIM // 4, 1), pltpu.roll(xt, HEAD_DIM // 4, 1))
        outs.append(xt * cos + sw * sin)
    return outs[0] if len(outs) == 1 else jnp.concatenate(outs, axis=1)


def _inproj_kernel(x_ref, sh_ref, sc_ref, g_ref, w_ref, gain_ref, ones_ref, wva_ref, wvb_ref, *rest,
                   groups, use_rope):
    if use_rope:
        cos, sin = rest[0][...], rest[1][...]
        rest = rest[2:]
    n_std = sum(len(subs) for _, _, subs in groups)
    out_refs, vt_refs = rest[:n_std], rest[n_std:]
    geff = g_ref[...] * (1.0 + sc_ref[0])
    shift = sh_ref[0]
    tm = x_ref.shape[1]
    sub = min(tm, INPROJ_SUB_ROWS)
    h_cache = {}

    def h_of(r0):
        if r0 not in h_cache:
            x = x_ref[0, r0:r0 + sub, :]
            ms = jnp.mean(x * x, axis=-1, keepdims=True)
            h_cache[r0] = ((x * lax.rsqrt(ms + EPS)) * geff + shift).astype(BF16)
        return h_cache[r0]

    def project(r0, c0, width, subs):
        return _dot(h_of(r0), w_ref[:, c0:c0 + width])

    def finish(p, r0, c0, width, subs):
        for s0, sw, rope, oi in subs:
            y = p[:, s0:s0 + sw]
            y = y * _group_rms(y, ones_ref) * gain_ref[:, c0 + s0:c0 + s0 + sw]
            if rope:
                y = _rope(y, cos[r0:r0 + sub], sin[r0:r0 + sub])
            out_refs[oi][0, r0:r0 + sub, :] = y.astype(BF16)

    def project_vt(r0, wv_ref, vt_ref):
        return _dot_nt(wv_ref[...], h_of(r0))

    def finish_vt(vt, r0, wv_ref, vt_ref):
        for j in range(sub // LANES):
            vt_ref[0, r0 // LANES + j] = vt[:, j * LANES:(j + 1) * LANES].astype(BF16)

    stages = []
    for r0 in range(0, tm, sub):
        stages += [(project, finish, (r0,) + g) for g in groups]
        stages += [(project_vt, finish_vt, (r0,) + a) for a in zip((wva_ref, wvb_ref), vt_refs)]
    pending = None
    for first, second, args in stages:
        res = first(*args)
        if pending is not None:
            pending[0](pending[1], *pending[2])
        pending = (second, res, args)
    pending[0](pending[1], *pending[2])


def _inproj_call(x, mod3, mod_batched, g, w, gain, ones, wva_t, wvb_t, rope_tabs, groups, out_widths, tm):
    b, s, d = x.shape
    n = w.shape[1]
    use_rope = rope_tabs is not None
    mod_idx = (lambda bi, i, j: (bi, 0, j)) if mod_batched else (lambda bi, i, j: (0, 0, j))
    const = lambda arr: pl.BlockSpec(arr.shape, lambda bi, i: (0, 0))
    in_specs = [
        pl.BlockSpec((1, tm, d), lambda bi, i: (bi, i, 0)),
        pl.BlockSpec((1, 1, d), lambda bi, i: mod_idx(bi, i, 0)),
        pl.BlockSpec((1, 1, d), lambda bi, i: mod_idx(bi, i, 1)),
        const(g), const(w), const(gain), const(ones), const(wva_t), const(wvb_t),
    ]
    args = [x, mod3, mod3, g, w, gain, ones, wva_t, wvb_t]
    if use_rope:
        in_specs += [pl.BlockSpec((tm, LANES), lambda bi, i: (i, 0))] * 2
        args += list(rope_tabs)
    vt_dims = [wva_t.shape[0], wvb_t.shape[0]]
    return pl.pallas_call(
        functools.partial(_inproj_kernel, groups=groups, use_rope=use_rope),
        grid=(b, s // tm),
        in_specs=in_specs,
        out_specs=([pl.BlockSpec((1, tm, wd), lambda bi, i: (bi, i, 0)) for wd in out_widths]
                   + [pl.BlockSpec((1, tm // LANES, vd, LANES), lambda bi, i: (bi, i, 0, 0)) for vd in vt_dims]),
        out_shape=([jax.ShapeDtypeStruct((b, s, wd), BF16) for wd in out_widths]
                   + [jax.ShapeDtypeStruct((b, s // LANES, vd, LANES), BF16) for vd in vt_dims]),
        compiler_params=pltpu.CompilerParams(dimension_semantics=("parallel", "arbitrary"),
                                             vmem_limit_bytes=VMEM_LIMIT),
        name="in_proj_rope" if use_rope else "in_proj_ctx",
    )(*args)


def _scores_t(qs, k_loc, k_ctx, add_loc):
    return _dot_nt(k_loc, qs) + add_loc, _dot_nt(k_ctx, qs)


def _softmax_pv_t(s_loc, s_ctx, vt_loc, vt_ctx, extra_logit):
    m = jnp.maximum(jnp.max(s_loc, axis=0, keepdims=True), jnp.max(s_ctx, axis=0, keepdims=True))
    if extra_logit is not None:
        m = jnp.maximum(m, extra_logit)
    n_v = vt_loc.shape[0]
    with_ones = lambda vt: jnp.concatenate([vt, jnp.ones((BF16_ROWS, vt.shape[1]), BF16)], axis=0)
    acc = (_dot(with_ones(vt_loc), jnp.exp2(s_loc - m).astype(BF16))
           + _dot(with_ones(vt_ctx), jnp.exp2(s_ctx - m).astype(BF16)))
    l = acc[n_v:n_v + 1]
    if extra_logit is not None:
        l = l + jnp.exp2(extra_logit - m)
    return acc[:n_v] * (1.0 / l)


def _group_norm_t(o_t, g_t):
    ms = jnp.mean(o_t * o_t, axis=0, keepdims=True)
    return ((o_t * lax.rsqrt(ms + EPS)) * g_t).T.astype(BF16)


def _lane_tiles(ref, first, n, rows=slice(None)):
    return jnp.concatenate([ref[0, first + j, rows, :] for j in range(n)], axis=1)


def _attn_kernel(sink_ref, qa_ref, qb_ref, ka_ref, vat_ref, kb_ref, vbt_ref, kac_ref, vact_ref,
                 kbc_ref, vbct_ref, ga_ref, gb_ref, *rest, seq, n_sub):
    bias_refs, o_ref = rest[:n_sub], rest[n_sub]
    groups = []
    for j in range(n_sub):
        groups += _attn_block_groups(
            pl.program_id(1) * n_sub + j, slice(j * QBLK, (j + 1) * QBLK), sink_ref, qa_ref, qb_ref, ka_ref,
            vat_ref, kb_ref, vbt_ref, kac_ref, vact_ref, kbc_ref, vbct_ref, ga_ref, gb_ref, bias_refs[j], o_ref, seq)
    pending = []
    for scores, finish in groups:
        pending.append((finish, scores()))
        if len(pending) > ATTN_LOOKAHEAD:
            fin, s = pending.pop(0)
            fin(*s)
    for fin, s in pending:
        fin(*s)


def _attn_block_groups(i, qrows, sink_ref, qa_ref, qb_ref, ka_ref, vat_ref, kb_ref, vbt_ref, kac_ref, vact_ref,
                       kbc_ref, vbct_ref, ga_ref, gb_ref, bias_ref, o_ref, seq):
    rows = seq // GRID_W
    stages = []
    n_ctx_tiles = kac_ref.shape[1] // LANES
    lane = lax.broadcasted_iota(jnp.int32, (QBLK, LANES), 1)
    lo = lane < HEAD_DIM
    half_masks = (lo.astype(F32).astype(BF16), jnp.logical_not(lo).astype(F32).astype(BF16))

    group = A_HEADS // A_KV_HEADS
    blk_a = jnp.clip(i - 1, 0, seq // QBLK - A_SPAN // QBLK)
    start_a = pl.multiple_of(blk_a * QBLK, QBLK)
    d = (lax.broadcasted_iota(jnp.int32, (A_SPAN, QBLK), 0)
         - lax.broadcasted_iota(jnp.int32, (A_SPAN, QBLK), 1) + (start_a - i * QBLK))
    amask = jnp.where((d <= A_WINDOW) & (d >= -A_WINDOW), 0.0, NEG).astype(F32)
    amask = jnp.concatenate([amask] * group, axis=1)
    k_loc = ka_ref[0, pl.ds(start_a, A_SPAN), :]
    k_ctx = kac_ref[0]
    q_tiles = [qa_ref[0, qrows, t * LANES:(t + 1) * LANES] for t in range(group)]
    heads_a = []

    def scores_a(kv):
        qs = jnp.concatenate([q * half_masks[kv] for q in q_tiles], axis=0)
        return _scores_t(qs, k_loc, k_ctx, amask)

    def finish_a(kv, s_loc, s_ctx):
        dims = slice(kv * HEAD_DIM, (kv + 1) * HEAD_DIM)
        sink = jnp.concatenate(
            [jnp.full((1, QBLK), sink_ref[kv * group + t] * LOG2E, F32) for t in range(group)], axis=1)
        o_t = _softmax_pv_t(s_loc, s_ctx, _lane_tiles(vat_ref, blk_a, A_SPAN // LANES, dims),
                            _lane_tiles(vact_ref, 0, n_ctx_tiles, dims), sink)
        heads_a.extend(o_t[:, t * QBLK:(t + 1) * QBLK] for t in range(group))
        if kv == A_KV_HEADS - 1:
            o_ref[0, qrows, :A_WIDTH] = _group_norm_t(jnp.concatenate(heads_a, axis=0), ga_ref[...])

    stages += [(functools.partial(scores_a, kv), functools.partial(finish_a, kv)) for kv in range(A_KV_HEADS)]

    blk_b = jnp.clip(NA_ROW_BLOCK * i - NA_KH // 2, 0, rows - B_KROWS) // NA_ROW_BLOCK
    start_b = pl.multiple_of(blk_b * QBLK, QBLK)
    heads_b = []

    def scores_b(pair):
        cols = slice(pair * LANES, (pair + 1) * LANES)
        qp = qb_ref[0, qrows, cols]
        qs = jnp.concatenate([qp * half_masks[0], qp * half_masks[1]], axis=0)
        return _scores_t(qs, kb_ref[0, pl.ds(start_b, B_SPAN), cols], kbc_ref[0, :, cols], bias_ref[0, pair])

    def finish_b(pair, s_loc, s_ctx):
        cols = slice(pair * LANES, (pair + 1) * LANES)
        o_t = _softmax_pv_t(s_loc, s_ctx, _lane_tiles(vbt_ref, blk_b, B_SPAN // LANES, cols),
                            _lane_tiles(vbct_ref, 0, n_ctx_tiles, cols), None)
        heads_b.extend([o_t[:HEAD_DIM, :QBLK], o_t[HEAD_DIM:, QBLK:]])
        if pair == B_HEADS // 2 - 1:
            o_ref[0, qrows, A_WIDTH:] = _group_norm_t(jnp.concatenate(heads_b, axis=0), gb_ref[...])

    stages += [(functools.partial(scores_b, p), functools.partial(finish_b, p)) for p in range(B_HEADS // 2)]
    return stages


def _attn_call(sink, qa, qb, ka, vat, kb, vbt, kac, vact, kbc, vbct, bias, ga_t, gb_t):
    b, s, _ = qa.shape
    n_blk = s // QBLK
    assert (NA_KH // 2) % NA_ROW_BLOCK == 0 and (s // GRID_W - B_KROWS) % NA_ROW_BLOCK == 0

    def pat(i):
        return jnp.where(i < 2, i, jnp.where(i >= n_blk - 2, i - (n_blk - N_PATTERNS), 2))

    n_sub = ATTN_BLOCKS_PER_STEP
    full = lambda arr: pl.BlockSpec((1,) + arr.shape[1:], lambda bi, i: (bi,) + (0,) * (arr.ndim - 1))
    const = lambda arr: pl.BlockSpec(arr.shape, lambda bi, i: (0,) * arr.ndim)
    blk = lambda wd: pl.BlockSpec((1, n_sub * QBLK, wd), lambda bi, i: (bi, i, 0))
    bias_spec = lambda j: pl.BlockSpec((1,) + bias.shape[1:], lambda bi, i: (pat(i * n_sub + j), 0, 0, 0))
    return pl.pallas_call(
        functools.partial(_attn_kernel, seq=s, n_sub=n_sub),
        grid=(b, n_blk // n_sub),
        in_specs=[
            pl.BlockSpec(memory_space=pltpu.SMEM),
            blk(A_WIDTH), blk(B_WIDTH),
            full(ka), full(vat), full(kb), full(vbt), full(kac), full(vact), full(kbc), full(vbct),
            const(ga_t), const(gb_t),
        ] + [bias_spec(j) for j in range(n_sub)],
        out_specs=blk(A_WIDTH + B_WIDTH),
        out_shape=jax.ShapeDtypeStruct((b, s, A_WIDTH + B_WIDTH), BF16),
        compiler_params=pltpu.CompilerParams(dimension_semantics=("parallel", "arbitrary"),
                                             vmem_limit_bytes=VMEM_LIMIT),
        name="attn",
    )(sink, qa, qb, ka, vat, kb, vbt, kac, vact, kbc, vbct, ga_t, gb_t, *([bias] * n_sub))


def _ffn_kernel(x_ref, o_ref, g1_ref, sh2_ref, sc2_ref, g2_ref, n2_ref, wo_ref, wg_ref, wu_ref, wd_ref,
                out_ref):
    geff = n2_ref[...] * (1.0 + sc2_ref[0])
    tm = x_ref.shape[1]
    sub = min(tm, FFN_SUB_ROWS)

    def stage_rows(r0):
        rows = slice(r0, r0 + sub)
        y = _dot(o_ref[0, rows, :], wo_ref[...])
        yield
        x1 = x_ref[0, rows, :] + g1_ref[0] * y
        ms = jnp.mean(x1 * x1, axis=-1, keepdims=True)
        h2 = ((x1 * lax.rsqrt(ms + EPS)) * geff + sh2_ref[0]).astype(BF16)
        gate = _dot(h2, wg_ref[...])
        up = _dot(h2, wu_ref[...])
        yield
        act = (_silu(gate) * up).astype(BF16)
        down = _dot(act, wd_ref[...])
        yield
        out_ref[0, rows, :] = x1 + g2_ref[0] * down

    live = [stage_rows(r0) for r0 in range(0, tm, sub)]
    while live:
        for gen in list(live):
            if next(gen, StopIteration) is StopIteration:
                live.remove(gen)


def _ffn_call(x, o, mod3, n2, wo, wg, wu, wd, tm):
    b, s, d = x.shape
    hid = wg.shape[1]
    modspec = lambda j: pl.BlockSpec((1, 1, d), lambda bi, i: (bi, 0, j))
    const = lambda shape: pl.BlockSpec(shape, lambda bi, i: (0, 0), pipeline_mode=pl.Buffered(1))
    return pl.pallas_call(
        _ffn_kernel,
        grid=(b, s // tm),
        in_specs=[
            pl.BlockSpec((1, tm, d), lambda bi, i: (bi, i, 0)),
            pl.BlockSpec((1, tm, o.shape[2]), lambda bi, i: (bi, i, 0)),
            modspec(2), modspec(3), modspec(4), modspec(5),
            pl.BlockSpec((1, d), lambda bi, i: (0, 0)),
            const((o.shape[2], d)), const((d, hid)), const((d, hid)), const((hid, d)),
        ],
        out_specs=pl.BlockSpec((1, tm, d), lambda bi, i: (bi, i, 0)),
        out_shape=jax.ShapeDtypeStruct((b, s, d), F32),
        compiler_params=pltpu.CompilerParams(dimension_semantics=("parallel", "arbitrary"),
                                             vmem_limit_bytes=VMEM_LIMIT),
        name="out_ffn",
    )(x, o, mod3, mod3, mod3, mod3, n2, wo, wg, wu, wd)


def _layer(x, ctx, mod, norm1_g, w_in, qn_a, kn_a, sink_a, qn_b, kn_b, rpb_b, on_a, on_b, w_out,
           norm2_g, w_gate, w_up, w_down, rope_tabs):
    b, s, d = x.shape
    rows = s // GRID_W
    group = A_HEADS // A_KV_HEADS
    perm_heads = np.array([h for t in range(group) for h in (t, t + group)])
    perm_cols = (perm_heads[:, None] * HEAD_DIM + np.arange(HEAD_DIM)[None, :]).reshape(-1)

    o_qa, o_ka, o_va, o_qb, o_kb, o_vb, o_end = np.cumsum(
        [0, A_WIDTH, A_KV_WIDTH, A_KV_WIDTH, B_WIDTH, B_WIDTH, B_WIDTH])
    w_qa, w_ka, w_va = w_in[:, o_qa:o_ka][:, perm_cols], w_in[:, o_ka:o_va], w_in[:, o_va:o_qb]
    w_qb, w_kb, w_vb = w_in[:, o_qb:o_kb], w_in[:, o_kb:o_vb], w_in[:, o_vb:o_end]
    w_l = jnp.concatenate([w_qa, w_ka, w_qb, w_kb], axis=1).astype(BF16)
    w_c = jnp.concatenate([w_ka, w_kb], axis=1).astype(BF16)
    wva_t, wvb_t = w_va.T.astype(BF16), w_vb.T.astype(BF16)
    scale = HEAD_DIM ** -0.5 * LOG2E
    gain_l = jnp.concatenate([jnp.tile(qn_a, A_HEADS) * scale, jnp.tile(kn_a, A_KV_HEADS),
                              jnp.tile(qn_b, B_HEADS) * scale, jnp.tile(kn_b, B_HEADS)])[None]
    gain_c = jnp.concatenate([jnp.tile(kn_a, A_KV_HEADS), jnp.tile(kn_b, B_HEADS)])[None]
    ones_bd = jnp.asarray(np.kron(np.eye(MXU_TILE // HEAD_DIM), np.ones((HEAD_DIM, HEAD_DIM))), BF16)

    mod3 = mod[:b].reshape(b, 1, -1)
    mod3_c = mod[b:b + 1].reshape(1, 1, -1)
    g1 = norm1_g[None]

    qka = A_WIDTH + A_KV_WIDTH
    groups_l = [
        (0, qka, [(0, A_WIDTH, True, 0), (A_WIDTH, A_KV_WIDTH, True, 1)]),
        (qka, B_WIDTH, [(0, B_WIDTH, False, 2)]),
        (qka + B_WIDTH, B_WIDTH, [(0, B_WIDTH, False, 3)]),
    ]
    qa, ka, qb, kb, vat, vbt = _inproj_call(
        x, mod3, True, g1, w_l, gain_l, ones_bd, wva_t, wvb_t, rope_tabs, groups_l,
        [A_WIDTH, A_KV_WIDTH, B_WIDTH, B_WIDTH], tm=INPROJ_ROWS)
    groups_c = [(0, A_KV_WIDTH + B_WIDTH, [(0, A_KV_WIDTH, False, 0), (A_KV_WIDTH, B_WIDTH, False, 1)])]
    kac, kbc, vact, vbct = _inproj_call(
        ctx, mod3_c, False, g1, w_c, gain_c, ones_bd, wva_t, wvb_t, None, groups_c,
        [A_KV_WIDTH, B_WIDTH], tm=ctx.shape[1])

    bias = _bias_call(rpb_b, rows)
    lanes_of = lambda g: jnp.broadcast_to(g[:, None], (g.shape[0], LANES))
    o = _attn_call(sink_a, qa, qb, ka, vat, kb, vbt, kac, vact, kbc, vbct, bias, lanes_of(on_a), lanes_of(on_b))
    return _ffn_call(x, o, mod3, norm2_g[None], w_out.astype(BF16), w_gate.astype(BF16), w_up.astype(BF16),
                     w_down.astype(BF16), tm=FFN_ROWS)


def kernel(x, c, ctx, c_ctx, w_mod, b_mod, norm1_g, w_in, qn_a, kn_a, sink_a, qn_b, kn_b, rpb_b, on_a, on_b,
           w_out, norm2_g, w_gate, w_up, w_down):
    b, s, d = x.shape
    depth = w_mod.shape[0]
    assert depth == 1, "the context stream update is only needed when a later layer reads it"
    rope_tabs = _rope_call(s, tm=512)
    pad = jnp.zeros((16 - b - 1, d), F32)
    cc = jnp.concatenate([c, c_ctx[None], pad], axis=0)
    l = 0
    mod = _mod_call(cc, w_mod[l], b_mod[l][None])
    return _layer(x, ctx, mod, norm1_g[l], w_in[l], qn_a[l], kn_a[l], sink_a[l], qn_b[l], kn_b[l], rpb_b[l],
                  on_a[l], on_b[l], w_out[l], norm2_g[l], w_gate[l], w_up[l], w_down[l], rope_tabs)
```

```python
import functools

import numpy as np
import jax
import jax.numpy as jnp
from jax import lax
from jax.experimental import pallas as pl
from jax.experimental.pallas import tpu as pltpu

GRID_W = 64
HEAD_DIM = 64
A_HEADS = 8
A_KV_HEADS = 2
A_WINDOW = 128
B_HEADS = 8
NA_KH = 8
NA_KW = 16
NA_ROW_BLOCK = 2
N_MOD = 6
ROPE_BASE = 10000.0
EPS = 1e-6

A_WIDTH = A_HEADS * HEAD_DIM
A_KV_WIDTH = A_KV_HEADS * HEAD_DIM
B_WIDTH = B_HEADS * HEAD_DIM
QBLK = NA_ROW_BLOCK * GRID_W
A_SPAN = QBLK + 2 * A_WINDOW
B_KROWS = 10
B_SPAN = B_KROWS * GRID_W
N_PATTERNS = 5
ATTN_BLOCKS_PER_STEP = 8
INPROJ_ROWS = 1024
INPROJ_SUB_ROWS = 512
FFN_ROWS = 512
FFN_SUB_ROWS = 256
A_STACK = 4
ATTN_PV_LAG = 1
ATTN_LOOKAHEAD = 1
NEG = -1e30
LOG2E = 1.4426950408889634

LANES = 128
BF16_ROWS = 16
MXU_TILE = 256
VMEM_LIMIT = 56 * 1024 * 1024

F32 = jnp.float32
BF16 = jnp.bfloat16


def _dot(a, b):
    return jnp.dot(a, b, preferred_element_type=F32)


def _dot_nt(a, b):
    return lax.dot_general(a, b, (((1,), (1,)), ((), ())), preferred_element_type=F32)


def _silu(x):
    return x / (1.0 + jnp.exp(-x))


def _mod_kernel(c_ref, w_ref, b_ref, o_ref):
    a = _silu(c_ref[...])
    o_ref[...] = jnp.dot(a, w_ref[...], precision=lax.Precision.HIGHEST,
                         preferred_element_type=F32) + b_ref[...]


def _mod_call(cc, w_mod, b_mod):
    rows, d = cc.shape
    n = w_mod.shape[1]
    return pl.pallas_call(
        _mod_kernel,
        grid=(n // d,),
        in_specs=[pl.BlockSpec((rows, d), lambda j: (0, 0)),
                  pl.BlockSpec((d, d), lambda j: (0, j)),
                  pl.BlockSpec((1, d), lambda j: (0, j))],
        out_specs=pl.BlockSpec((rows, d), lambda j: (0, j)),
        out_shape=jax.ShapeDtypeStruct((rows, n), F32),
        name="mod",
    )(cc, w_mod, b_mod)


def _rope_kernel(invf_ref, cos_ref, sin_ref, *, tm):
    t = pl.program_id(0) * tm + lax.broadcasted_iota(jnp.int32, (tm, LANES), 0)
    lane = lax.broadcasted_iota(jnp.int32, (tm, LANES), 1)
    l64 = lane & (HEAD_DIM - 1)
    pos = jnp.where(l64 < HEAD_DIM // 2, t // GRID_W, t % GRID_W).astype(F32)
    ang = pos * invf_ref[...]
    first = (lane & (HEAD_DIM // 2 - 1)) < HEAD_DIM // 4
    cos_ref[...] = jnp.cos(ang)
    sin_ref[...] = jnp.where(first, -jnp.sin(ang), jnp.sin(ang))


def _rope_call(seq, tm):
    quarter = HEAD_DIM // 4
    inv = (1.0 / (np.float32(ROPE_BASE) ** (np.arange(quarter, dtype=np.float32) / quarter))).astype(np.float32)
    invf = jnp.asarray(np.tile(inv, LANES // quarter)[None, :])
    return pl.pallas_call(
        functools.partial(_rope_kernel, tm=tm),
        grid=(seq // tm,),
        in_specs=[pl.BlockSpec((1, LANES), lambda i: (0, 0))],
        out_specs=[pl.BlockSpec((tm, LANES), lambda i: (i, 0))] * 2,
        out_shape=[jax.ShapeDtypeStruct((seq, LANES), F32)] * 2,
        name="rope_tables",
    )(invf)


def _b_patterns(rows):
    n_blk = rows // NA_ROW_BLOCK
    blocks = [0, 1, 2, n_blk - 2, n_blk - 1]
    return [(NA_ROW_BLOCK * i, _b_key_start(i, rows)) for i in blocks]


def _b_key_start(i, rows):
    return min(max(NA_ROW_BLOCK * i - NA_KH // 2, 0), rows - B_KROWS)


def _bias_kernel(rpb_ref, o_ref, *, patterns, rows):
    h = pl.program_id(0)
    n_dr, n_dc = 2 * NA_KH - 1, 2 * NA_KW - 1
    kc = lax.broadcasted_iota(jnp.int32, (GRID_W, LANES), 0)
    lane = lax.broadcasted_iota(jnp.int32, (GRID_W, LANES), 1)
    qc = lane & (GRID_W - 1)
    dc = jnp.clip(kc - qc + NA_KW - 1, 0, n_dc - 1)
    cs = jnp.clip(qc - NA_KW // 2, 0, GRID_W - NA_KW)
    col_ok = (kc >= cs) & (kc < cs + NA_KW)
    lo = lane < GRID_W
    base = h * (n_dr * n_dc)
    per_dr = []
    for dr in range(n_dr):
        m = jnp.zeros((GRID_W, LANES), F32)
        for d in range(n_dc):
            m = jnp.where(dc == d, rpb_ref[base + dr * n_dc + d] * LOG2E, m)
        per_dr.append(m)
    neg = jnp.full((GRID_W, LANES), NEG, F32)
    for p, (r0, ks) in enumerate(patterns):
        q_rows = [r0 + qr for qr in range(NA_ROW_BLOCK)]
        rs = [min(max(q - NA_KH // 2, 0), rows - NA_KH) for q in q_rows]
        for kr in range(B_KROWS):
            k_row = ks + kr
            ok = [r <= k_row < r + NA_KH for r in rs]
            drs = [min(max(k_row - q + NA_KH - 1, 0), n_dr - 1) for q in q_rows]
            if not any(ok):
                piece = neg
            else:
                vals = jnp.where(lo, per_dr[drs[0]], per_dr[drs[1]])
                mask = col_ok
                if not ok[1]:
                    mask = mask & lo
                if not ok[0]:
                    mask = mask & jnp.logical_not(lo)
                piece = jnp.where(mask, vals, neg)
            o_ref[p, 0, kr * GRID_W:(kr + 1) * GRID_W, :] = piece


def _bias_call(rpb, rows):
    patterns = _b_patterns(rows)
    heads = rpb.shape[0]
    return pl.pallas_call(
        functools.partial(_bias_kernel, patterns=patterns, rows=rows),
        grid=(heads,),
        in_specs=[pl.BlockSpec(memory_space=pltpu.SMEM)],
        out_specs=pl.BlockSpec((N_PATTERNS, 1, B_SPAN, QBLK), lambda h: (0, h // 2, 0, h % 2)),
        out_shape=jax.ShapeDtypeStruct((N_PATTERNS, heads // 2, B_SPAN, 2 * QBLK), F32),
        name="na_bias",
    )(rpb.reshape(-1))


def _group_rms(p, ones_ref):
    sq = (p * p).astype(BF16)
    n = p.shape[1]
    if n <= MXU_TILE:
        ssq = _dot(sq, ones_ref[:n, :n])
    else:
        ssq = jnp.concatenate(
            [_dot(sq[:, c:c + MXU_TILE], ones_ref[...]) for c in range(0, n, MXU_TILE)], axis=1)
    return lax.rsqrt(ssq * (1.0 / HEAD_DIM) + EPS)


def _rope(x, cos, sin):
    lane = lax.broadcasted_iota(jnp.int32, (x.shape[0], LANES), 1)
    first = (lane & (HEAD_DIM // 2 - 1)) < HEAD_DIM // 4
    outs = []
    for c in range(0, x.shape[1], LANES):
        xt = x[:, c:c + LANES]
        sw = jnp.where(first, pltpu.roll(xt, LANES - HEAD_DIM // 4, 1), pltpu.roll(xt, HEAD_DIM // 4, 1))
        outs.append(xt * cos + sw * sin)
    return outs[0] if len(outs) == 1 else jnp.concatenate(outs, axis=1)


def _inproj_kernel(x_ref, sh_ref, sc_ref, g_ref, w_ref, gain_ref, ones_ref, wva_ref, wvb_ref, *rest,
                   groups, use_rope):
    if use_rope:
        cos, sin = rest[0][...], rest[1][...]
        rest = rest[2:]
    n_std = sum(len(subs) for _, _, subs in groups)
    out_refs, vt_refs = rest[:n_std], rest[n_std:]
    geff = g_ref[...] * (1.0 + sc_ref[0])
    shift = sh_ref[0]
    tm = x_ref.shape[1]
    sub = min(tm, INPROJ_SUB_ROWS)
    h_cache = {}

    def h_of(r0):
        if r0 not in h_cache:
            x = x_ref[0, r0:r0 + sub, :]
            ms = jnp.mean(x * x, axis=-1, keepdims=True)
            h_cache[r0] = ((x * lax.rsqrt(ms + EPS)) * geff + shift).astype(BF16)
        return h_cache[r0]

    def project(r0, c0, width, subs):
        return _dot(h_of(r0), w_ref[:, c0:c0 + width])

    def finish(p, r0, c0, width, subs):
        for s0, sw, rope, oi in subs:
            y = p[:, s0:s0 + sw]
            y = y * _group_rms(y, ones_ref) * gain_ref[:, c0 + s0:c0 + s0 + sw]
            if rope:
                y = _rope(y, cos[r0:r0 + sub], sin[r0:r0 + sub])
            out_refs[oi][0, r0:r0 + sub, :] = y.astype(BF16)

    def project_vt(r0, wv_ref, vt_ref):
        return _dot_nt(wv_ref[...], h_of(r0))

    def finish_vt(vt, r0, wv_ref, vt_ref):
        for j in range(sub // LANES):
            vt_ref[0, r0 // LANES + j] = vt[:, j * LANES:(j + 1) * LANES].astype(BF16)

    stages = []
    for r0 in range(0, tm, sub):
        stages += [(project, finish, (r0,) + g) for g in groups]
        stages += [(project_vt, finish_vt, (r0,) + a) for a in zip((wva_ref, wvb_ref), vt_refs)]
    pending = None
    for first, second, args in stages:
        res = first(*args)
        if pending is not None:
            pending[0](pending[1], *pending[2])
        pending = (second, res, args)
    pending[0](pending[1], *pending[2])


def _inproj_call(x, mod3, mod_batched, g, w, gain, ones, wva_t, wvb_t, rope_tabs, groups, out_widths, tm):
    b, s, d = x.shape
    n = w.shape[1]
    use_rope = rope_tabs is not None
    mod_idx = (lambda bi, i, j: (bi, 0, j)) if mod_batched else (lambda bi, i, j: (0, 0, j))
    const = lambda arr: pl.BlockSpec(arr.shape, lambda bi, i: (0, 0))
    in_specs = [
        pl.BlockSpec((1, tm, d), lambda bi, i: (bi, i, 0)),
        pl.BlockSpec((1, 1, d), lambda bi, i: mod_idx(bi, i, 0)),
        pl.BlockSpec((1, 1, d), lambda bi, i: mod_idx(bi, i, 1)),
        const(g), const(w), const(gain), const(ones), const(wva_t), const(wvb_t),
    ]
    args = [x, mod3, mod3, g, w, gain, ones, wva_t, wvb_t]
    if use_rope:
        in_specs += [pl.BlockSpec((tm, LANES), lambda bi, i: (i, 0))] * 2
        args += list(rope_tabs)
    vt_dims = [wva_t.shape[0], wvb_t.shape[0]]
    return pl.pallas_call(
        functools.partial(_inproj_kernel, groups=groups, use_rope=use_rope),
        grid=(b, s // tm),
        in_specs=in_specs,
        out_specs=([pl.BlockSpec((1, tm, wd), lambda bi, i: (bi, i, 0)) for wd in out_widths]
                   + [pl.BlockSpec((1, tm // LANES, vd, LANES), lambda bi, i: (bi, i, 0, 0)) for vd in vt_dims]),
        out_shape=([jax.ShapeDtypeStruct((b, s, wd), BF16) for wd in out_widths]
                   + [jax.ShapeDtypeStruct((b, s // LANES, vd, LANES), BF16) for vd in vt_dims]),
        compiler_params=pltpu.CompilerParams(dimension_semantics=("parallel", "arbitrary"),
                                             vmem_limit_bytes=VMEM_LIMIT),
        name="in_proj_rope" if use_rope else "in_proj_ctx",
    )(*args)


def _scores_t(qs, k_loc, k_ctx, add_loc):
    return _dot_nt(k_loc, qs) + add_loc, _dot_nt(k_ctx, qs)


def _softmax_t(s_loc, s_ctx, extra_logit):
    m = jnp.maximum(jnp.max(s_loc, axis=0, keepdims=True), jnp.max(s_ctx, axis=0, keepdims=True))
    if extra_logit is not None:
        m = jnp.maximum(m, extra_logit)
    return jnp.exp2(s_loc - m).astype(BF16), jnp.exp2(s_ctx - m).astype(BF16), m


def _pv_t(p_loc, p_ctx, m, vt_loc, vt_ctx, extra_logit):
    n_v = vt_loc.shape[0]
    with_ones = lambda vt: jnp.concatenate([vt, jnp.ones((BF16_ROWS, vt.shape[1]), BF16)], axis=0)
    acc = _dot(with_ones(vt_loc), p_loc) + _dot(with_ones(vt_ctx), p_ctx)
    l = acc[n_v:n_v + 1]
    if extra_logit is not None:
        l = l + jnp.exp2(extra_logit - m)
    return acc[:n_v] * (1.0 / l)


def _group_norm_t(o_t, g_t):
    ms = jnp.mean(o_t * o_t, axis=0, keepdims=True)
    return ((o_t * lax.rsqrt(ms + EPS)) * g_t).T.astype(BF16)


def _lane_tiles(ref, first, n, rows=slice(None)):
    return jnp.concatenate([ref[0, first + j, rows, :] for j in range(n)], axis=1)


def _attn_kernel(sink_ref, qa_ref, qb_ref, ka_ref, vat_ref, kb_ref, vbt_ref, kac_ref, vact_ref,
                 kbc_ref, vbct_ref, ga_ref, gb_ref, bias_ref, o_ref, *, seq, n_sub):
    groups = []
    for j in range(n_sub):
        groups += _attn_block_groups(
            pl.program_id(1) * n_sub + j, slice(j * QBLK, (j + 1) * QBLK), sink_ref, qa_ref, qb_ref, ka_ref,
            vat_ref, kb_ref, vbt_ref, kac_ref, vact_ref, kbc_ref, vbct_ref, ga_ref, gb_ref, bias_ref, o_ref, seq)
    scored, soft = [], []
    for scores, finish in groups:
        scored.append(finish(*scores()))
        if len(scored) > ATTN_LOOKAHEAD:
            gen = scored.pop(0)
            next(gen)
            soft.append(gen)
        if len(soft) > ATTN_PV_LAG:
            next(soft.pop(0), None)
    for gen in scored:
        next(gen)
        soft.append(gen)
    for gen in soft:
        next(gen, None)


def _attn_block_groups(i, qrows, sink_ref, qa_ref, qb_ref, ka_ref, vat_ref, kb_ref, vbt_ref, kac_ref, vact_ref,
                       kbc_ref, vbct_ref, ga_ref, gb_ref, bias_ref, o_ref, seq):
    rows = seq // GRID_W
    stages = []
    n_ctx_tiles = kac_ref.shape[1] // LANES
    lane = lax.broadcasted_iota(jnp.int32, (QBLK, LANES), 1)
    lo = lane < HEAD_DIM
    half_masks = (lo.astype(F32).astype(BF16), jnp.logical_not(lo).astype(F32).astype(BF16))

    group = A_HEADS // A_KV_HEADS
    blk_a = jnp.clip(i - 1, 0, seq // QBLK - A_SPAN // QBLK)
    start_a = pl.multiple_of(blk_a * QBLK, QBLK)
    d = (lax.broadcasted_iota(jnp.int32, (A_SPAN, QBLK), 0)
         - lax.broadcasted_iota(jnp.int32, (A_SPAN, QBLK), 1) + (start_a - i * QBLK))
    amask = jnp.where((d <= A_WINDOW) & (d >= -A_WINDOW), 0.0, NEG).astype(F32)
    amask = jnp.concatenate([amask] * A_STACK, axis=1)
    k_loc = ka_ref[0, pl.ds(start_a, A_SPAN), :]
    k_ctx = kac_ref[0]
    q_tiles = [qa_ref[0, qrows, t * LANES:(t + 1) * LANES] for t in range(group)]
    heads_a = []

    def scores_a(kv, t0):
        qs = jnp.concatenate([q * half_masks[kv] for q in q_tiles[t0:t0 + A_STACK]], axis=0)
        return _scores_t(qs, k_loc, k_ctx, amask)

    def finish_a(kv, t0, s_loc, s_ctx):
        dims = slice(kv * HEAD_DIM, (kv + 1) * HEAD_DIM)
        sink = jnp.concatenate(
            [jnp.full((1, QBLK), sink_ref[kv * group + t0 + t] * LOG2E, F32) for t in range(A_STACK)], axis=1)
        p_loc, p_ctx, m = _softmax_t(s_loc, s_ctx, sink)
        yield
        o_t = _pv_t(p_loc, p_ctx, m, _lane_tiles(vat_ref, blk_a, A_SPAN // LANES, dims),
                    _lane_tiles(vact_ref, 0, n_ctx_tiles, dims), sink)
        heads_a.extend(o_t[:, t * QBLK:(t + 1) * QBLK] for t in range(A_STACK))
        if len(heads_a) == A_HEADS:
            o_ref[0, qrows, :A_WIDTH] = _group_norm_t(jnp.concatenate(heads_a, axis=0), ga_ref[...])

    stages += [(functools.partial(scores_a, kv, t0), functools.partial(finish_a, kv, t0))
               for kv in range(A_KV_HEADS) for t0 in range(0, group, A_STACK)]

    blk_b = jnp.clip(NA_ROW_BLOCK * i - NA_KH // 2, 0, rows - B_KROWS) // NA_ROW_BLOCK
    start_b = pl.multiple_of(blk_b * QBLK, QBLK)
    heads_b = []
    n_blk = seq // QBLK
    pattern = jnp.where(i < 2, i, jnp.where(i >= n_blk - 2, i - (n_blk - N_PATTERNS), 2))

    def scores_b(pair):
        cols = slice(pair * LANES, (pair + 1) * LANES)
        qp = qb_ref[0, qrows, cols]
        qs = jnp.concatenate([qp * half_masks[0], qp * half_masks[1]], axis=0)
        return _scores_t(qs, kb_ref[0, pl.ds(start_b, B_SPAN), cols], kbc_ref[0, :, cols], bias_ref[pattern, pair])

    def finish_b(pair, s_loc, s_ctx):
        cols = slice(pair * LANES, (pair + 1) * LANES)
        p_loc, p_ctx, m = _softmax_t(s_loc, s_ctx, None)
        yield
        o_t = _pv_t(p_loc, p_ctx, m, _lane_tiles(vbt_ref, blk_b, B_SPAN // LANES, cols),
                    _lane_tiles(vbct_ref, 0, n_ctx_tiles, cols), None)
        heads_b.extend([o_t[:HEAD_DIM, :QBLK], o_t[HEAD_DIM:, QBLK:]])
        if pair == B_HEADS // 2 - 1:
            o_ref[0, qrows, A_WIDTH:] = _group_norm_t(jnp.concatenate(heads_b, axis=0), gb_ref[...])

    stages += [(functools.partial(scores_b, p), functools.partial(finish_b, p)) for p in range(B_HEADS // 2)]
    return stages


def _attn_call(sink, qa, qb, ka, vat, kb, vbt, kac, vact, kbc, vbct, bias, ga_t, gb_t):
    b, s, _ = qa.shape
    n_blk = s // QBLK
    assert (NA_KH // 2) % NA_ROW_BLOCK == 0 and (s // GRID_W - B_KROWS) % NA_ROW_BLOCK == 0

    n_sub = ATTN_BLOCKS_PER_STEP
    full = lambda arr: pl.BlockSpec((1,) + arr.shape[1:], lambda bi, i: (bi,) + (0,) * (arr.ndim - 1))
    const = lambda arr, **kw: pl.BlockSpec(arr.shape, lambda bi, i: (0,) * arr.ndim, **kw)
    blk = lambda wd: pl.BlockSpec((1, n_sub * QBLK, wd), lambda bi, i: (bi, i, 0))
    return pl.pallas_call(
        functools.partial(_attn_kernel, seq=s, n_sub=n_sub),
        grid=(b, n_blk // n_sub),
        in_specs=[
            pl.BlockSpec(memory_space=pltpu.SMEM),
            blk(A_WIDTH), blk(B_WIDTH),
            full(ka), full(vat), full(kb), full(vbt), full(kac), full(vact), full(kbc), full(vbct),
            const(ga_t), const(gb_t), const(bias, pipeline_mode=pl.Buffered(1)),
        ],
        out_specs=blk(A_WIDTH + B_WIDTH),
        out_shape=jax.ShapeDtypeStruct((b, s, A_WIDTH + B_WIDTH), BF16),
        compiler_params=pltpu.CompilerParams(dimension_semantics=("parallel", "arbitrary"),
                                             vmem_limit_bytes=VMEM_LIMIT),
        name="attn",
    )(sink, qa, qb, ka, vat, kb, vbt, kac, vact, kbc, vbct, ga_t, gb_t, bias)


def _ffn_kernel(x_ref, o_ref, g1_ref, sh2_ref, sc2_ref, g2_ref, n2_ref, wo_ref, wg_ref, wu_ref, wd_ref,
                out_ref):
    geff = n2_ref[...] * (1.0 + sc2_ref[0])
    tm = x_ref.shape[1]
    sub = min(tm, FFN_SUB_ROWS)

    def stage_rows(r0):
        rows = slice(r0, r0 + sub)
        y = _dot(o_ref[0, rows, :], wo_ref[...])
        yield
        x1 = x_ref[0, rows, :] + g1_ref[0] * y
        ms = jnp.mean(x1 * x1, axis=-1, keepdims=True)
        h2 = ((x1 * lax.rsqrt(ms + EPS)) * geff + sh2_ref[0]).astype(BF16)
        gate = _dot(h2, wg_ref[...])
        up = _dot(h2, wu_ref[...])
        yield
        act = (_silu(gate) * up).astype(BF16)
        down = _dot(act, wd_ref[...])
        yield
        out_ref[0, rows, :] = x1 + g2_ref[0] * down

    live = [stage_rows(r0) for r0 in range(0, tm, sub)]
    while live:
        for gen in list(live):
            if next(gen, StopIteration) is StopIteration:
                live.remove(gen)


def _ffn_call(x, o, mod3, n2, wo, wg, wu, wd, tm):
    b, s, d = x.shape
    hid = wg.shape[1]
    modspec = lambda j: pl.BlockSpec((1, 1, d), lambda bi, i: (bi, 0, j))
    const = lambda shape: pl.BlockSpec(shape, lambda bi, i: (0, 0), pipeline_mode=pl.Buffered(1))
    return pl.pallas_call(
        _ffn_kernel,
        grid=(b, s // tm),
        in_specs=[
            pl.BlockSpec((1, tm, d), lambda bi, i: (bi, i, 0)),
            pl.BlockSpec((1, tm, o.shape[2]), lambda bi, i: (bi, i, 0)),
            modspec(2), modspec(3), modspec(4), modspec(5),
            pl.BlockSpec((1, d), lambda bi, i: (0, 0)),
            const((o.shape[2], d)), const((d, hid)), const((d, hid)), const((hid, d)),
        ],
        out_specs=pl.BlockSpec((1, tm, d), lambda bi, i: (bi, i, 0)),
        out_shape=jax.ShapeDtypeStruct((b, s, d), F32),
        compiler_params=pltpu.CompilerParams(dimension_semantics=("parallel", "arbitrary"),
                                             vmem_limit_bytes=VMEM_LIMIT),
        name="out_ffn",
    )(x, o, mod3, mod3, mod3, mod3, n2, wo, wg, wu, wd)


def _layer(x, ctx, mod, norm1_g, w_in, qn_a, kn_a, sink_a, qn_b, kn_b, rpb_b, on_a, on_b, w_out,
           norm2_g, w_gate, w_up, w_down, rope_tabs):
    b, s, d = x.shape
    rows = s // GRID_W
    group = A_HEADS // A_KV_HEADS
    perm_heads = np.array([h for t in range(group) for h in (t, t + group)])
    perm_cols = (perm_heads[:, None] * HEAD_DIM + np.arange(HEAD_DIM)[None, :]).reshape(-1)

    o_qa, o_ka, o_va, o_qb, o_kb, o_vb, o_end = np.cumsum(
        [0, A_WIDTH, A_KV_WIDTH, A_KV_WIDTH, B_WIDTH, B_WIDTH, B_WIDTH])
    w_qa, w_ka, w_va = w_in[:, o_qa:o_ka][:, perm_cols], w_in[:, o_ka:o_va], w_in[:, o_va:o_qb]
    w_qb, w_kb, w_vb = w_in[:, o_qb:o_kb], w_in[:, o_kb:o_vb], w_in[:, o_vb:o_end]
    w_l = jnp.concatenate([w_qa, w_ka, w_qb, w_kb], axis=1).astype(BF16)
    w_c = jnp.concatenate([w_ka, w_kb], axis=1).astype(BF16)
    wva_t, wvb_t = w_va.T.astype(BF16), w_vb.T.astype(BF16)
    scale = HEAD_DIM ** -0.5 * LOG2E
    gain_l = jnp.concatenate([jnp.tile(qn_a, A_HEADS) * scale, jnp.tile(kn_a, A_KV_HEADS),
                              jnp.tile(qn_b, B_HEADS) * scale, jnp.tile(kn_b, B_HEADS)])[None]
    gain_c = jnp.concatenate([jnp.tile(kn_a, A_KV_HEADS), jnp.tile(kn_b, B_HEADS)])[None]
    ones_bd = jnp.asarray(np.kron(np.eye(MXU_TILE // HEAD_DIM), np.ones((HEAD_DIM, HEAD_DIM))), BF16)

    mod3 = mod[:b].reshape(b, 1, -1)
    mod3_c = mod[b:b + 1].reshape(1, 1, -1)
    g1 = norm1_g[None]

    qka = A_WIDTH + A_KV_WIDTH
    groups_l = [
        (0, qka, [(0, A_WIDTH, True, 0), (A_WIDTH, A_KV_WIDTH, True, 1)]),
        (qka, B_WIDTH, [(0, B_WIDTH, False, 2)]),
        (qka + B_WIDTH, B_WIDTH, [(0, B_WIDTH, False, 3)]),
    ]
    qa, ka, qb, kb, vat, vbt = _inproj_call(
        x, mod3, True, g1, w_l, gain_l, ones_bd, wva_t, wvb_t, rope_tabs, groups_l,
        [A_WIDTH, A_KV_WIDTH, B_WIDTH, B_WIDTH], tm=INPROJ_ROWS)
    groups_c = [(0, A_KV_WIDTH + B_WIDTH, [(0, A_KV_WIDTH, False, 0), (A_KV_WIDTH, B_WIDTH, False, 1)])]
    kac, kbc, vact, vbct = _inproj_call(
        ctx, mod3_c, False, g1, w_c, gain_c, ones_bd, wva_t, wvb_t, None, groups_c,
        [A_KV_WIDTH, B_WIDTH], tm=ctx.shape[1])

    bias = _bias_call(rpb_b, rows)
    lanes_of = lambda g: jnp.broadcast_to(g[:, None], (g.shape[0], LANES))
    o = _attn_call(sink_a, qa, qb, ka, vat, kb, vbt, kac, vact, kbc, vbct, bias, lanes_of(on_a), lanes_of(on_b))
    return _ffn_call(x, o, mod3, norm2_g[None], w_out.astype(BF16), w_gate.astype(BF16), w_up.astype(BF16),
                     w_down.astype(BF16), tm=FFN_ROWS)


def kernel(x, c, ctx, c_ctx, w_mod, b_mod, norm1_g, w_in, qn_a, kn_a, sink_a, qn_b, kn_b, rpb_b, on_a, on_b,
           w_out, norm2_g, w_gate, w_up, w_down):
    b, s, d = x.shape
    depth = w_mod.shape[0]
    assert depth == 1, "the context stream update is only needed when a later layer reads it"
    rope_tabs = _rope_call(s, tm=512)
    pad = jnp.zeros((16 - b - 1, d), F32)
    cc = jnp.concatenate([c, c_ctx[None], pad], axis=0)
    l = 0
    mod = _mod_call(cc, w_mod[l], b_mod[l][None])
    return _layer(x, ctx, mod, norm1_g[l], w_in[l], qn_a[l], kn_a[l], sink_a[l], qn_b[l], kn_b[l], rpb_b[l],
                  on_a[l], on_b[l], w_out[l], norm2_g[l], w_gate[l], w_up[l], w_down[l], rope_tabs)
```

```python
import functools

import numpy as np
import jax
import jax.numpy as jnp
from jax import lax
from jax.experimental import pallas as pl
from jax.experimental.pallas import tpu as pltpu

GRID_W = 64
HEAD_DIM = 64
A_HEADS = 8
A_KV_HEADS = 2
A_WINDOW = 128
B_HEADS = 8
NA_KH = 8
NA_KW = 16
NA_ROW_BLOCK = 2
N_MOD = 6
ROPE_BASE = 10000.0
EPS = 1e-6

A_WIDTH = A_HEADS * HEAD_DIM
A_KV_WIDTH = A_KV_HEADS * HEAD_DIM
B_WIDTH = B_HEADS * HEAD_DIM
QBLK = NA_ROW_BLOCK * GRID_W
A_SPAN = QBLK + 2 * A_WINDOW
B_KROWS = 10
B_SPAN = B_KROWS * GRID_W
N_PATTERNS = 5
ATTN_BLOCKS_PER_STEP = 8
INPROJ_ROWS = 1024
INPROJ_SUB_ROWS = 512
FFN_ROWS = 512
FFN_SUB_ROWS = 256
A_STACK = 4
ATTN_PV_LAG = 1
ATTN_LOOKAHEAD = 1
NEG = -1e30
LOG2E = 1.4426950408889634

LANES = 128
BF16_ROWS = 16
MXU_TILE = 256
VMEM_LIMIT = 56 * 1024 * 1024

F32 = jnp.float32
BF16 = jnp.bfloat16


def _dot(a, b):
    return jnp.dot(a, b, preferred_element_type=F32)


def _dot_nt(a, b):
    return lax.dot_general(a, b, (((1,), (1,)), ((), ())), preferred_element_type=F32)


def _silu(x):
    return x / (1.0 + jnp.exp(-x))


def _mod_kernel(c_ref, w_ref, b_ref, o_ref):
    split = lambda v: (v.astype(BF16), (v - v.astype(BF16).astype(F32)).astype(BF16))
    a_hi, a_lo = split(_silu(c_ref[...]))
    w_hi, w_lo = split(w_ref[...])
    rows = a_hi.shape[0]
    head = _dot(jnp.concatenate([a_hi, a_lo], axis=0), w_hi)
    o_ref[...] = head[:rows] + head[rows:] + _dot(a_hi, w_lo) + b_ref[...]


def _mod_call(cc, w_mod, b_mod):
    rows, d = cc.shape
    n = w_mod.shape[1]
    return pl.pallas_call(
        _mod_kernel,
        grid=(n // d,),
        in_specs=[pl.BlockSpec((rows, d), lambda j: (0, 0)),
                  pl.BlockSpec((d, d), lambda j: (0, j)),
                  pl.BlockSpec((1, d), lambda j: (0, j))],
        out_specs=pl.BlockSpec((rows, d), lambda j: (0, j)),
        out_shape=jax.ShapeDtypeStruct((rows, n), F32),
        name="mod",
    )(cc, w_mod, b_mod)


def _rope_kernel(invf_ref, cos_ref, sin_ref, *, rows):
    n_pos = max(rows, GRID_W)
    pos = lax.broadcasted_iota(jnp.int32, (n_pos, LANES), 0).astype(F32)
    lane = lax.broadcasted_iota(jnp.int32, (GRID_W, LANES), 1)
    row_lanes = (lane & (HEAD_DIM - 1)) < HEAD_DIM // 2
    first = (lane & (HEAD_DIM // 2 - 1)) < HEAD_DIM // 4
    ang = pos * invf_ref[...]
    cos_p, sin_p = jnp.cos(ang), jnp.sin(ang)
    cos_c = cos_p[:GRID_W]
    sin_c = jnp.where(first, -sin_p[:GRID_W], sin_p[:GRID_W])
    for r in range(rows):
        tok = slice(r * GRID_W, (r + 1) * GRID_W)
        cos_ref[tok, :] = jnp.where(row_lanes, cos_p[r:r + 1], cos_c)
        sin_ref[tok, :] = jnp.where(row_lanes, jnp.where(first, -sin_p[r:r + 1], sin_p[r:r + 1]), sin_c)


def _rope_call(seq):
    quarter = HEAD_DIM // 4
    inv = (1.0 / (np.float32(ROPE_BASE) ** (np.arange(quarter, dtype=np.float32) / quarter))).astype(np.float32)
    invf = jnp.asarray(np.tile(inv, LANES // quarter)[None, :])
    return pl.pallas_call(
        functools.partial(_rope_kernel, rows=seq // GRID_W),
        grid=(1,),
        in_specs=[pl.BlockSpec((1, LANES), lambda i: (0, 0))],
        out_specs=[pl.BlockSpec((seq, LANES), lambda i: (0, 0))] * 2,
        out_shape=[jax.ShapeDtypeStruct((seq, LANES), F32)] * 2,
        name="rope_tables",
    )(invf)


def _b_patterns(rows):
    n_blk = rows // NA_ROW_BLOCK
    blocks = [0, 1, 2, n_blk - 2, n_blk - 1]
    return [(NA_ROW_BLOCK * i, _b_key_start(i, rows)) for i in blocks]


def _b_key_start(i, rows):
    return min(max(NA_ROW_BLOCK * i - NA_KH // 2, 0), rows - B_KROWS)


def _bias_kernel(rpb_ref, o_ref, *, patterns, rows):
    h = pl.program_id(0)
    n_dr, n_dc = 2 * NA_KH - 1, 2 * NA_KW - 1
    kc = lax.broadcasted_iota(jnp.int32, (GRID_W, LANES), 0)
    lane = lax.broadcasted_iota(jnp.int32, (GRID_W, LANES), 1)
    qc = lane & (GRID_W - 1)
    dc = jnp.clip(kc - qc + NA_KW - 1, 0, n_dc - 1)
    cs = jnp.clip(qc - NA_KW // 2, 0, GRID_W - NA_KW)
    col_ok = (kc >= cs) & (kc < cs + NA_KW)
    lo = lane < GRID_W
    base = h * (n_dr * n_dc)
    per_dr = []
    for dr in range(n_dr):
        m = jnp.zeros((GRID_W, LANES), F32)
        for d in range(n_dc):
            m = jnp.where(dc == d, rpb_ref[base + dr * n_dc + d] * LOG2E, m)
        per_dr.append(m)
    neg = jnp.full((GRID_W, LANES), NEG, F32)
    for p, (r0, ks) in enumerate(patterns):
        q_rows = [r0 + qr for qr in range(NA_ROW_BLOCK)]
        rs = [min(max(q - NA_KH // 2, 0), rows - NA_KH) for q in q_rows]
        for kr in range(B_KROWS):
            k_row = ks + kr
            ok = [r <= k_row < r + NA_KH for r in rs]
            drs = [min(max(k_row - q + NA_KH - 1, 0), n_dr - 1) for q in q_rows]
            if not any(ok):
                piece = neg
            else:
                vals = jnp.where(lo, per_dr[drs[0]], per_dr[drs[1]])
                mask = col_ok
                if not ok[1]:
                    mask = mask & lo
                if not ok[0]:
                    mask = mask & jnp.logical_not(lo)
                piece = jnp.where(mask, vals, neg)
            o_ref[p, 0, kr * GRID_W:(kr + 1) * GRID_W, :] = piece


def _bias_call(rpb, rows):
    patterns = _b_patterns(rows)
    heads = rpb.shape[0]
    return pl.pallas_call(
        functools.partial(_bias_kernel, patterns=patterns, rows=rows),
        grid=(heads,),
        in_specs=[pl.BlockSpec(memory_space=pltpu.SMEM)],
        out_specs=pl.BlockSpec((N_PATTERNS, 1, B_SPAN, QBLK), lambda h: (0, h // 2, 0, h % 2)),
        out_shape=jax.ShapeDtypeStruct((N_PATTERNS, heads // 2, B_SPAN, 2 * QBLK), F32),
        name="na_bias",
    )(rpb.reshape(-1))


def _group_rms(p, ones_ref):
    sq = (p * p).astype(BF16)
    n = p.shape[1]
    if n <= MXU_TILE:
        ssq = _dot(sq, ones_ref[:n, :n])
    else:
        ssq = jnp.concatenate(
            [_dot(sq[:, c:c + MXU_TILE], ones_ref[...]) for c in range(0, n, MXU_TILE)], axis=1)
    return lax.rsqrt(ssq * (1.0 / HEAD_DIM) + EPS)


def _rope(x, cos, sin):
    lane = lax.broadcasted_iota(jnp.int32, (x.shape[0], LANES), 1)
    first = (lane & (HEAD_DIM // 2 - 1)) < HEAD_DIM // 4
    outs = []
    for c in range(0, x.shape[1], LANES):
        xt = x[:, c:c + LANES]
        sw = jnp.where(first, pltpu.roll(xt, LANES - HEAD_DIM // 4, 1), pltpu.roll(xt, HEAD_DIM // 4, 1))
        outs.append(xt * cos + sw * sin)
    return outs[0] if len(outs) == 1 else jnp.concatenate(outs, axis=1)


def _inproj_kernel(x_ref, sh_ref, sc_ref, g_ref, w_ref, gain_ref, ones_ref, wva_ref, wvb_ref, *rest,
                   groups, use_rope):
    if use_rope:
        cos, sin = rest[0][...], rest[1][...]
        rest = rest[2:]
    n_std = sum(len(subs) for _, _, subs in groups)
    out_refs, vt_refs = rest[:n_std], rest[n_std:]
    geff = g_ref[...] * (1.0 + sc_ref[0])
    shift = sh_ref[0]
    tm = x_ref.shape[1]
    sub = min(tm, INPROJ_SUB_ROWS)
    h_cache = {}

    def h_of(r0):
        if r0 not in h_cache:
            x = x_ref[0, r0:r0 + sub, :]
            ms = jnp.mean(x * x, axis=-1, keepdims=True)
            h_cache[r0] = ((x * lax.rsqrt(ms + EPS)) * geff + shift).astype(BF16)
        return h_cache[r0]

    def project(r0, c0, width, subs):
        return _dot(h_of(r0), w_ref[:, c0:c0 + width])

    def finish(p, r0, c0, width, subs):
        for s0, sw, rope, oi in subs:
            y = p[:, s0:s0 + sw]
            y = y * _group_rms(y, ones_ref) * gain_ref[:, c0 + s0:c0 + s0 + sw]
            if rope:
                y = _rope(y, cos[r0:r0 + sub], sin[r0:r0 + sub])
            out_refs[oi][0, r0:r0 + sub, :] = y.astype(BF16)

    def project_vt(r0, wv_ref, vt_ref):
        return _dot_nt(wv_ref[...], h_of(r0))

    def finish_vt(vt, r0, wv_ref, vt_ref):
        for j in range(sub // LANES):
            vt_ref[0, r0 // LANES + j] = vt[:, j * LANES:(j + 1) * LANES].astype(BF16)

    stages = []
    for r0 in range(0, tm, sub):
        stages += [(project, finish, (r0,) + g) for g in groups]
        stages += [(project_vt, finish_vt, (r0,) + a) for a in zip((wva_ref, wvb_ref), vt_refs)]
    pending = None
    for first, second, args in stages:
        res = first(*args)
        if pending is not None:
            pending[0](pending[1], *pending[2])
        pending = (second, res, args)
    pending[0](pending[1], *pending[2])


def _inproj_call(x, mod3, mod_batched, g, w, gain, ones, wva_t, wvb_t, rope_tabs, groups, out_widths, tm):
    b, s, d = x.shape
    n = w.shape[1]
    use_rope = rope_tabs is not None
    mod_idx = (lambda bi, i, j: (bi, 0, j)) if mod_batched else (lambda bi, i, j: (0, 0, j))
    const = lambda arr: pl.BlockSpec(arr.shape, lambda bi, i: (0, 0))
    in_specs = [
        pl.BlockSpec((1, tm, d), lambda bi, i: (bi, i, 0)),
        pl.BlockSpec((1, 1, d), lambda bi, i: mod_idx(bi, i, 0)),
        pl.BlockSpec((1, 1, d), lambda bi, i: mod_idx(bi, i, 1)),
        const(g), const(w), const(gain), const(ones), const(wva_t), const(wvb_t),
    ]
    args = [x, mod3, mod3, g, w, gain, ones, wva_t, wvb_t]
    if use_rope:
        in_specs += [pl.BlockSpec((tm, LANES), lambda bi, i: (i, 0))] * 2
        args += list(rope_tabs)
    vt_dims = [wva_t.shape[0], wvb_t.shape[0]]
    return pl.pallas_call(
        functools.partial(_inproj_kernel, groups=groups, use_rope=use_rope),
        grid=(b, s // tm),
        in_specs=in_specs,
        out_specs=([pl.BlockSpec((1, tm, wd), lambda bi, i: (bi, i, 0)) for wd in out_widths]
                   + [pl.BlockSpec((1, tm // LANES, vd, LANES), lambda bi, i: (bi, i, 0, 0)) for vd in vt_dims]),
        out_shape=([jax.ShapeDtypeStruct((b, s, wd), BF16) for wd in out_widths]
                   + [jax.ShapeDtypeStruct((b, s // LANES, vd, LANES), BF16) for vd in vt_dims]),
        compiler_params=pltpu.CompilerParams(dimension_semantics=("parallel", "arbitrary"),
                                             vmem_limit_bytes=VMEM_LIMIT),
        name="in_proj_rope" if use_rope else "in_proj_ctx",
    )(*args)


def _scores_t(qs, k_loc, k_ctx, add_loc):
    return _dot_nt(k_loc, qs) + add_loc, _dot_nt(k_ctx, qs)


def _softmax_t(s_loc, s_ctx, extra_logit):
    m = jnp.maximum(jnp.max(s_loc, axis=0, keepdims=True), jnp.max(s_ctx, axis=0, keepdims=True))
    if extra_logit is not None:
        m = jnp.maximum(m, extra_logit)
    return jnp.exp2(s_loc - m).astype(BF16), jnp.exp2(s_ctx - m).astype(BF16), m


def _pv_t(p_loc, p_ctx, m, vt_loc, vt_ctx, extra_logit):
    n_v = vt_loc.shape[0]
    with_ones = lambda vt: jnp.concatenate([vt, jnp.ones((BF16_ROWS, vt.shape[1]), BF16)], axis=0)
    acc = _dot(with_ones(vt_loc), p_loc) + _dot(with_ones(vt_ctx), p_ctx)
    l = acc[n_v:n_v + 1]
    if extra_logit is not None:
        l = l + jnp.exp2(extra_logit - m)
    return acc[:n_v] * (1.0 / l)


def _group_norm_t(o_t, g_t):
    ms = jnp.mean(o_t * o_t, axis=0, keepdims=True)
    return ((o_t * lax.rsqrt(ms + EPS)) * g_t).T.astype(BF16)


def _lane_tiles(ref, first, n, rows=slice(None)):
    return jnp.concatenate([ref[0, first + j, rows, :] for j in range(n)], axis=1)


def _attn_kernel(sink_ref, qa_ref, qb_ref, ka_ref, vat_ref, kb_ref, vbt_ref, kac_ref, vact_ref,
                 kbc_ref, vbct_ref, ga_ref, gb_ref, bias_ref, o_ref, *, seq, n_sub):
    groups = []
    for j in range(n_sub):
        groups += _attn_block_groups(
            pl.program_id(1) * n_sub + j, slice(j * QBLK, (j + 1) * QBLK), sink_ref, qa_ref, qb_ref, ka_ref,
            vat_ref, kb_ref, vbt_ref, kac_ref, vact_ref, kbc_ref, vbct_ref, ga_ref, gb_ref, bias_ref, o_ref, seq)
    scored, soft = [], []
    for scores, finish in groups:
        scored.append(finish(*scores()))
        if len(scored) > ATTN_LOOKAHEAD:
            gen = scored.pop(0)
            next(gen)
            soft.append(gen)
        if len(soft) > ATTN_PV_LAG:
            next(soft.pop(0), None)
    for gen in scored:
        next(gen)
        soft.append(gen)
    for gen in soft:
        next(gen, None)


def _attn_block_groups(i, qrows, sink_ref, qa_ref, qb_ref, ka_ref, vat_ref, kb_ref, vbt_ref, kac_ref, vact_ref,
                       kbc_ref, vbct_ref, ga_ref, gb_ref, bias_ref, o_ref, seq):
    rows = seq // GRID_W
    stages = []
    n_ctx_tiles = kac_ref.shape[1] // LANES
    lane = lax.broadcasted_iota(jnp.int32, (QBLK, LANES), 1)
    lo = lane < HEAD_DIM
    half_masks = (lo.astype(F32).astype(BF16), jnp.logical_not(lo).astype(F32).astype(BF16))

    group = A_HEADS // A_KV_HEADS
    blk_a = jnp.clip(i - 1, 0, seq // QBLK - A_SPAN // QBLK)
    start_a = pl.multiple_of(blk_a * QBLK, QBLK)
    d = (lax.broadcasted_iota(jnp.int32, (A_SPAN, QBLK), 0)
         - lax.broadcasted_iota(jnp.int32, (A_SPAN, QBLK), 1) + (start_a - i * QBLK))
    amask = jnp.where((d <= A_WINDOW) & (d >= -A_WINDOW), 0.0, NEG).astype(F32)
    amask = jnp.concatenate([amask] * A_STACK, axis=1)
    k_loc = ka_ref[0, pl.ds(start_a, A_SPAN), :]
    k_ctx = kac_ref[0]
    q_tiles = [qa_ref[0, qrows, t * LANES:(t + 1) * LANES] for t in range(group)]
    heads_a = []

    def scores_a(kv, t0):
        qs = jnp.concatenate([q * half_masks[kv] for q in q_tiles[t0:t0 + A_STACK]], axis=0)
        return _scores_t(qs, k_loc, k_ctx, amask)

    def finish_a(kv, t0, s_loc, s_ctx):
        dims = slice(kv * HEAD_DIM, (kv + 1) * HEAD_DIM)
        sink = jnp.concatenate(
            [jnp.full((1, QBLK), sink_ref[kv * group + t0 + t] * LOG2E, F32) for t in range(A_STACK)], axis=1)
        p_loc, p_ctx, m = _softmax_t(s_loc, s_ctx, sink)
        yield
        o_t = _pv_t(p_loc, p_ctx, m, _lane_tiles(vat_ref, blk_a, A_SPAN // LANES, dims),
                    _lane_tiles(vact_ref, 0, n_ctx_tiles, dims), sink)
        heads_a.extend(o_t[:, t * QBLK:(t + 1) * QBLK] for t in range(A_STACK))
        if len(heads_a) == A_HEADS:
            o_ref[0, qrows, :A_WIDTH] = _group_norm_t(jnp.concatenate(heads_a, axis=0), ga_ref[...])

    stages += [(functools.partial(scores_a, kv, t0), functools.partial(finish_a, kv, t0))
               for kv in range(A_KV_HEADS) for t0 in range(0, group, A_STACK)]

    blk_b = jnp.clip(NA_ROW_BLOCK * i - NA_KH // 2, 0, rows - B_KROWS) // NA_ROW_BLOCK
    start_b = pl.multiple_of(blk_b * QBLK, QBLK)
    heads_b = []
    n_blk = seq // QBLK
    pattern = jnp.where(i < 2, i, jnp.where(i >= n_blk - 2, i - (n_blk - N_PATTERNS), 2))

    def scores_b(pair):
        cols = slice(pair * LANES, (pair + 1) * LANES)
        qp = qb_ref[0, qrows, cols]
        qs = jnp.concatenate([qp * half_masks[0], qp * half_masks[1]], axis=0)
        return _scores_t(qs, kb_ref[0, pl.ds(start_b, B_SPAN), cols], kbc_ref[0, :, cols], bias_ref[pattern, pair])

    def finish_b(pair, s_loc, s_ctx):
        cols = slice(pair * LANES, (pair + 1) * LANES)
        p_loc, p_ctx, m = _softmax_t(s_loc, s_ctx, None)
        yield
        o_t = _pv_t(p_loc, p_ctx, m, _lane_tiles(vbt_ref, blk_b, B_SPAN // LANES, cols),
                    _lane_tiles(vbct_ref, 0, n_ctx_tiles, cols), None)
        heads_b.extend([o_t[:HEAD_DIM, :QBLK], o_t[HEAD_DIM:, QBLK:]])
        if pair == B_HEADS // 2 - 1:
            o_ref[0, qrows, A_WIDTH:] = _group_norm_t(jnp.concatenate(heads_b, axis=0), gb_ref[...])

    stages += [(functools.partial(scores_b, p), functools.partial(finish_b, p)) for p in range(B_HEADS // 2)]
    return stages


def _attn_call(sink, qa, qb, ka, vat, kb, vbt, kac, vact, kbc, vbct, bias, ga_t, gb_t):
    b, s, _ = qa.shape
    n_blk = s // QBLK
    assert (NA_KH // 2) % NA_ROW_BLOCK == 0 and (s // GRID_W - B_KROWS) % NA_ROW_BLOCK == 0

    n_sub = ATTN_BLOCKS_PER_STEP
    full = lambda arr: pl.BlockSpec((1,) + arr.shape[1:], lambda bi, i: (bi,) + (0,) * (arr.ndim - 1))
    const = lambda arr, **kw: pl.BlockSpec(arr.shape, lambda bi, i: (0,) * arr.ndim, **kw)
    blk = lambda wd: pl.BlockSpec((1, n_sub * QBLK, wd), lambda bi, i: (bi, i, 0))
    return pl.pallas_call(
        functools.partial(_attn_kernel, seq=s, n_sub=n_sub),
        grid=(b, n_blk // n_sub),
        in_specs=[
            pl.BlockSpec(memory_space=pltpu.SMEM),
            blk(A_WIDTH), blk(B_WIDTH),
            full(ka), full(vat), full(kb), full(vbt), full(kac), full(vact), full(kbc), full(vbct),
            const(ga_t), const(gb_t), const(bias, pipeline_mode=pl.Buffered(1)),
        ],
        out_specs=blk(A_WIDTH + B_WIDTH),
        out_shape=jax.ShapeDtypeStruct((b, s, A_WIDTH + B_WIDTH), BF16),
        compiler_params=pltpu.CompilerParams(dimension_semantics=("parallel", "arbitrary"),
                                             vmem_limit_bytes=VMEM_LIMIT),
        name="attn",
    )(sink, qa, qb, ka, vat, kb, vbt, kac, vact, kbc, vbct, ga_t, gb_t, bias)


def _ffn_kernel(x_ref, o_ref, g1_ref, sh2_ref, sc2_ref, g2_ref, n2_ref, wo_ref, wg_ref, wu_ref, wd_ref,
                out_ref):
    geff = n2_ref[...] * (1.0 + sc2_ref[0])
    tm = x_ref.shape[1]
    sub = min(tm, FFN_SUB_ROWS)

    def stage_rows(r0):
        rows = slice(r0, r0 + sub)
        y = _dot(o_ref[0, rows, :], wo_ref[...])
        yield
        x1 = x_ref[0, rows, :] + g1_ref[0] * y
        ms = jnp.mean(x1 * x1, axis=-1, keepdims=True)
        h2 = ((x1 * lax.rsqrt(ms + EPS)) * geff + sh2_ref[0]).astype(BF16)
        gate = _dot(h2, wg_ref[...])
        up = _dot(h2, wu_ref[...])
        yield
        act = (_silu(gate) * up).astype(BF16)
        down = _dot(act, wd_ref[...])
        yield
        out_ref[0, rows, :] = x1 + g2_ref[0] * down

    live = [stage_rows(r0) for r0 in range(0, tm, sub)]
    while live:
        for gen in list(live):
            if next(gen, StopIteration) is StopIteration:
                live.remove(gen)


def _ffn_call(x, o, mod3, n2, wo, wg, wu, wd, tm):
    b, s, d = x.shape
    hid = wg.shape[1]
    modspec = lambda j: pl.BlockSpec((1, 1, d), lambda bi, i: (bi, 0, j))
    const = lambda shape: pl.BlockSpec(shape, lambda bi, i: (0, 0), pipeline_mode=pl.Buffered(1))
    return pl.pallas_call(
        _ffn_kernel,
        grid=(b, s // tm),
        in_specs=[
            pl.BlockSpec((1, tm, d), lambda bi, i: (bi, i, 0)),
            pl.BlockSpec((1, tm, o.shape[2]), lambda bi, i: (bi, i, 0)),
            modspec(2), modspec(3), modspec(4), modspec(5),
            pl.BlockSpec((1, d), lambda bi, i: (0, 0)),
            const((o.shape[2], d)), const((d, hid)), const((d, hid)), const((hid, d)),
        ],
        out_specs=pl.BlockSpec((1, tm, d), lambda bi, i: (bi, i, 0)),
        out_shape=jax.ShapeDtypeStruct((b, s, d), F32),
        compiler_params=pltpu.CompilerParams(dimension_semantics=("parallel", "arbitrary"),
                                             vmem_limit_bytes=VMEM_LIMIT),
        name="out_ffn",
    )(x, o, mod3, mod3, mod3, mod3, n2, wo, wg, wu, wd)


def _layer(x, ctx, mod, norm1_g, w_in, qn_a, kn_a, sink_a, qn_b, kn_b, rpb_b, on_a, on_b, w_out,
           norm2_g, w_gate, w_up, w_down, rope_tabs):
    b, s, d = x.shape
    rows = s // GRID_W
    group = A_HEADS // A_KV_HEADS
    perm_heads = [h for t in range(group) for h in (t, t + group)]

    o_qa, o_ka, o_va, o_qb, o_kb, o_vb, o_end = np.cumsum(
        [0, A_WIDTH, A_KV_WIDTH, A_KV_WIDTH, B_WIDTH, B_WIDTH, B_WIDTH])
    w_bf = w_in.astype(BF16)
    w_qa = [w_bf[:, o_qa + h * HEAD_DIM:o_qa + (h + 1) * HEAD_DIM] for h in perm_heads]
    w_ka, w_va = w_bf[:, o_ka:o_va], w_bf[:, o_va:o_qb]
    w_qb, w_kb, w_vb = w_bf[:, o_qb:o_kb], w_bf[:, o_kb:o_vb], w_bf[:, o_vb:o_end]
    w_l = jnp.concatenate(w_qa + [w_ka, w_qb, w_kb], axis=1)
    w_c = jnp.concatenate([w_ka, w_kb], axis=1)
    wva_t, wvb_t = w_va.T, w_vb.T
    scale = HEAD_DIM ** -0.5 * LOG2E
    gain_l = jnp.concatenate([jnp.tile(qn_a, A_HEADS) * scale, jnp.tile(kn_a, A_KV_HEADS),
                              jnp.tile(qn_b, B_HEADS) * scale, jnp.tile(kn_b, B_HEADS)])[None]
    gain_c = jnp.concatenate([jnp.tile(kn_a, A_KV_HEADS), jnp.tile(kn_b, B_HEADS)])[None]
    ones_bd = jnp.asarray(np.kron(np.eye(MXU_TILE // HEAD_DIM), np.ones((HEAD_DIM, HEAD_DIM))), BF16)

    mod3 = mod[:b].reshape(b, 1, -1)
    mod3_c = mod[b:b + 1].reshape(1, 1, -1)
    g1 = norm1_g[None]

    qka = A_WIDTH + A_KV_WIDTH
    groups_l = [
        (0, qka, [(0, A_WIDTH, True, 0), (A_WIDTH, A_KV_WIDTH, True, 1)]),
        (qka, B_WIDTH, [(0, B_WIDTH, False, 2)]),
        (qka + B_WIDTH, B_WIDTH, [(0, B_WIDTH, False, 3)]),
    ]
    qa, ka, qb, kb, vat, vbt = _inproj_call(
        x, mod3, True, g1, w_l, gain_l, ones_bd, wva_t, wvb_t, rope_tabs, groups_l,
        [A_WIDTH, A_KV_WIDTH, B_WIDTH, B_WIDTH], tm=INPROJ_ROWS)
    groups_c = [(0, A_KV_WIDTH + B_WIDTH, [(0, A_KV_WIDTH, False, 0), (A_KV_WIDTH, B_WIDTH, False, 1)])]
    n_ctx = ctx.shape[1]
    ctx_out = _inproj_call(
        ctx.reshape(1, b * n_ctx, d), mod3_c, False, g1, w_c, gain_c, ones_bd, wva_t, wvb_t, None, groups_c,
        [A_KV_WIDTH, B_WIDTH], tm=min(b * n_ctx, INPROJ_ROWS))
    kac, kbc = (a.reshape(b, n_ctx, -1) for a in ctx_out[:2])
    vact, vbct = (a.reshape(b, n_ctx // LANES, -1, LANES) for a in ctx_out[2:])

    bias = _bias_call(rpb_b, rows)
    lanes_of = lambda g: jnp.broadcast_to(g[:, None], (g.shape[0], LANES))
    o = _attn_call(sink_a, qa, qb, ka, vat, kb, vbt, kac, vact, kbc, vbct, bias, lanes_of(on_a), lanes_of(on_b))
    return _ffn_call(x, o, mod3, norm2_g[None], w_out.astype(BF16), w_gate.astype(BF16), w_up.astype(BF16),
                     w_down.astype(BF16), tm=FFN_ROWS)


def kernel(x, c, ctx, c_ctx, w_mod, b_mod, norm1_g, w_in, qn_a, kn_a, sink_a, qn_b, kn_b, rpb_b, on_a, on_b,
           w_out, norm2_g, w_gate, w_up, w_down):
    b, s, d = x.shape
    depth = w_mod.shape[0]
    assert depth == 1, "the context stream update is only needed when a later layer reads it"
    rope_tabs = _rope_call(s)
    pad = jnp.zeros((16 - b - 1, d), F32)
    cc = jnp.concatenate([c, c_ctx[None], pad], axis=0)
    l = 0
    mod = _mod_call(cc, w_mod[l], b_mod[l][None])
    return _layer(x, ctx, mod, norm1_g[l], w_in[l], qn_a[l], kn_a[l], sink_a[l], qn_b[l], kn_b[l], rpb_b[l],
                  on_a[l], on_b[l], w_out[l], norm2_g[l], w_gate[l], w_up[l], w_down[l], rope_tabs)
```

```python
import functools

import numpy as np
import jax
import jax.numpy as jnp
from jax import lax
from jax.experimental import pallas as pl
from jax.experimental.pallas import tpu as pltpu

GRID_W = 64
HEAD_DIM = 64
A_HEADS = 8
A_KV_HEADS = 2
A_WINDOW = 128
B_HEADS = 8
NA_KH = 8
NA_KW = 16
NA_ROW_BLOCK = 2
N_MOD = 6
ROPE_BASE = 10000.0
EPS = 1e-6

A_WIDTH = A_HEADS * HEAD_DIM
A_KV_WIDTH = A_KV_HEADS * HEAD_DIM
B_WIDTH = B_HEADS * HEAD_DIM
QBLK = NA_ROW_BLOCK * GRID_W
A_SPAN = QBLK + 2 * A_WINDOW
B_KROWS = 10
B_SPAN = B_KROWS * GRID_W
N_PATTERNS = 5
ATTN_BLOCKS_PER_STEP = 8
INPROJ_ROWS = 1024
CTX_ROWS = 512
INPROJ_SUB_ROWS = 512
FFN_ROWS = 512
FFN_SUB_ROWS = 256
A_STACK = 4
ATTN_PV_LAG = 1
ATTN_LOOKAHEAD = 1
NEG = -1e30
LOG2E = 1.4426950408889634

LANES = 128
BF16_ROWS = 16
MXU_TILE = 256
VMEM_LIMIT = 56 * 1024 * 1024

F32 = jnp.float32
BF16 = jnp.bfloat16


def _dot(a, b):
    return jnp.dot(a, b, preferred_element_type=F32)


def _dot_nt(a, b):
    return lax.dot_general(a, b, (((1,), (1,)), ((), ())), preferred_element_type=F32)


def _silu(x):
    return x / (1.0 + jnp.exp(-x))


def _mod_kernel(c_ref, w_ref, b_ref, o_ref):
    split = lambda v: (v.astype(BF16), (v - v.astype(BF16).astype(F32)).astype(BF16))
    a_hi, a_lo = split(_silu(c_ref[...]))
    w_hi, w_lo = split(w_ref[...])
    rows = a_hi.shape[0]
    head = _dot(jnp.concatenate([a_hi, a_lo], axis=0), w_hi)
    o_ref[...] = head[:rows] + head[rows:] + _dot(a_hi, w_lo) + b_ref[...]


def _mod_call(cc, w_mod, b_mod):
    rows, d = cc.shape
    n = w_mod.shape[1]
    return pl.pallas_call(
        _mod_kernel,
        grid=(n // d,),
        in_specs=[pl.BlockSpec((rows, d), lambda j: (0, 0)),
                  pl.BlockSpec((d, d), lambda j: (0, j)),
                  pl.BlockSpec((1, d), lambda j: (0, j))],
        out_specs=pl.BlockSpec((rows, d), lambda j: (0, j)),
        out_shape=jax.ShapeDtypeStruct((rows, n), F32),
        name="mod",
    )(cc, w_mod, b_mod)


def _wprep_kernel(w_ref, wl_ref, wc_ref, wvat_ref, wvbt_ref):
    group = A_HEADS // A_KV_HEADS
    c_ka, c_va, c_qb, c_kb, c_vb = np.cumsum([A_WIDTH, A_KV_WIDTH, A_KV_WIDTH, B_WIDTH, B_WIDTH])
    lane = lax.broadcasted_iota(jnp.int32, (w_ref.shape[0], LANES), 1)
    lo = lane < HEAD_DIM
    for t in range(group):
        src = [w_ref[:, (h // 2) * LANES:(h // 2 + 1) * LANES] for h in (t, t + group)]
        if t % 2 == 0:
            tile = jnp.where(lo, src[0], pltpu.roll(src[1], HEAD_DIM, 1))
        else:
            tile = jnp.where(lo, pltpu.roll(src[0], HEAD_DIM, 1), src[1])
        wl_ref[:, t * LANES:(t + 1) * LANES] = tile.astype(BF16)
    k_a = w_ref[:, c_ka:c_va].astype(BF16)
    k_b = w_ref[:, c_kb:c_vb].astype(BF16)
    wl_ref[:, A_WIDTH:A_WIDTH + A_KV_WIDTH] = k_a
    wl_ref[:, A_WIDTH + A_KV_WIDTH:A_WIDTH + A_KV_WIDTH + B_WIDTH] = w_ref[:, c_qb:c_kb].astype(BF16)
    wl_ref[:, A_WIDTH + A_KV_WIDTH + B_WIDTH:] = k_b
    wc_ref[:, :A_KV_WIDTH] = k_a
    wc_ref[:, A_KV_WIDTH:] = k_b
    wvat_ref[...] = w_ref[:, c_va:c_qb].T.astype(BF16)
    for c in range(0, B_WIDTH, LANES):
        wvbt_ref[c:c + LANES, :] = w_ref[:, c_vb + c:c_vb + c + LANES].T.astype(BF16)


def _wprep_call(w_in):
    d, n = w_in.shape
    assert (A_HEADS // A_KV_HEADS) % 2 == 0 and n == A_WIDTH + 2 * A_KV_WIDTH + 3 * B_WIDTH
    shapes = [(d, A_WIDTH + A_KV_WIDTH + 2 * B_WIDTH), (d, A_KV_WIDTH + B_WIDTH), (A_KV_WIDTH, d), (B_WIDTH, d)]
    return pl.pallas_call(
        _wprep_kernel,
        grid=(1,),
        in_specs=[pl.BlockSpec((d, n), lambda i: (0, 0))],
        out_specs=[pl.BlockSpec(s, lambda i: (0, 0)) for s in shapes],
        out_shape=[jax.ShapeDtypeStruct(s, BF16) for s in shapes],
        compiler_params=pltpu.CompilerParams(vmem_limit_bytes=VMEM_LIMIT),
        name="w_in_layout",
    )(w_in)


def _rope_kernel(invf_ref, cos_ref, sin_ref, *, rows):
    n_pos = max(rows, GRID_W)
    pos = lax.broadcasted_iota(jnp.int32, (n_pos, LANES), 0).astype(F32)
    lane = lax.broadcasted_iota(jnp.int32, (GRID_W, LANES), 1)
    row_lanes = (lane & (HEAD_DIM - 1)) < HEAD_DIM // 2
    first = (lane & (HEAD_DIM // 2 - 1)) < HEAD_DIM // 4
    ang = pos * invf_ref[...]
    cos_p, sin_p = jnp.cos(ang), jnp.sin(ang)
    cos_c = cos_p[:GRID_W]
    sin_c = jnp.where(first, -sin_p[:GRID_W], sin_p[:GRID_W])
    for r in range(rows):
        tok = slice(r * GRID_W, (r + 1) * GRID_W)
        cos_ref[tok, :] = jnp.where(row_lanes, cos_p[r:r + 1], cos_c)
        sin_ref[tok, :] = jnp.where(row_lanes, jnp.where(first, -sin_p[r:r + 1], sin_p[r:r + 1]), sin_c)


def _rope_call(seq):
    quarter = HEAD_DIM // 4
    inv = (1.0 / (np.float32(ROPE_BASE) ** (np.arange(quarter, dtype=np.float32) / quarter))).astype(np.float32)
    invf = jnp.asarray(np.tile(inv, LANES // quarter)[None, :])
    return pl.pallas_call(
        functools.partial(_rope_kernel, rows=seq // GRID_W),
        grid=(1,),
        in_specs=[pl.BlockSpec((1, LANES), lambda i: (0, 0))],
        out_specs=[pl.BlockSpec((seq, LANES), lambda i: (0, 0))] * 2,
        out_shape=[jax.ShapeDtypeStruct((seq, LANES), F32)] * 2,
        name="rope_tables",
    )(invf)


def _b_patterns(rows):
    n_blk = rows // NA_ROW_BLOCK
    blocks = [0, 1, 2, n_blk - 2, n_blk - 1]
    return [(NA_ROW_BLOCK * i, _b_key_start(i, rows)) for i in blocks]


def _b_key_start(i, rows):
    return min(max(NA_ROW_BLOCK * i - NA_KH // 2, 0), rows - B_KROWS)


def _bias_kernel(rpb_ref, o_ref, *, patterns, rows):
    h = pl.program_id(0)
    n_dr, n_dc = 2 * NA_KH - 1, 2 * NA_KW - 1
    kc = lax.broadcasted_iota(jnp.int32, (GRID_W, LANES), 0)
    lane = lax.broadcasted_iota(jnp.int32, (GRID_W, LANES), 1)
    qc = lane & (GRID_W - 1)
    dc = jnp.clip(kc - qc + NA_KW - 1, 0, n_dc - 1)
    cs = jnp.clip(qc - NA_KW // 2, 0, GRID_W - NA_KW)
    col_ok = (kc >= cs) & (kc < cs + NA_KW)
    lo = lane < GRID_W
    base = h * (n_dr * n_dc)
    per_dr = []
    for dr in range(n_dr):
        m = jnp.zeros((GRID_W, LANES), F32)
        for d in range(n_dc):
            m = jnp.where(dc == d, rpb_ref[base + dr * n_dc + d] * LOG2E, m)
        per_dr.append(m)
    neg = jnp.full((GRID_W, LANES), NEG, F32)
    for p, (r0, ks) in enumerate(patterns):
        q_rows = [r0 + qr for qr in range(NA_ROW_BLOCK)]
        rs = [min(max(q - NA_KH // 2, 0), rows - NA_KH) for q in q_rows]
        for kr in range(B_KROWS):
            k_row = ks + kr
            ok = [r <= k_row < r + NA_KH for r in rs]
            drs = [min(max(k_row - q + NA_KH - 1, 0), n_dr - 1) for q in q_rows]
            if not any(ok):
                piece = neg
            else:
                vals = jnp.where(lo, per_dr[drs[0]], per_dr[drs[1]])
                mask = col_ok
                if not ok[1]:
                    mask = mask & lo
                if not ok[0]:
                    mask = mask & jnp.logical_not(lo)
                piece = jnp.where(mask, vals, neg)
            o_ref[p, 0, kr * GRID_W:(kr + 1) * GRID_W, :] = piece


def _bias_call(rpb, rows):
    patterns = _b_patterns(rows)
    heads = rpb.shape[0]
    return pl.pallas_call(
        functools.partial(_bias_kernel, patterns=patterns, rows=rows),
        grid=(heads,),
        in_specs=[pl.BlockSpec(memory_space=pltpu.SMEM)],
        out_specs=pl.BlockSpec((N_PATTERNS, 1, B_SPAN, QBLK), lambda h: (0, h // 2, 0, h % 2)),
        out_shape=jax.ShapeDtypeStruct((N_PATTERNS, heads // 2, B_SPAN, 2 * QBLK), F32),
        name="na_bias",
    )(rpb.reshape(-1))


def _group_rms(p, ones_ref):
    sq = (p * p).astype(BF16)
    n = p.shape[1]
    if n <= MXU_TILE:
        ssq = _dot(sq, ones_ref[:n, :n])
    else:
        ssq = jnp.concatenate(
            [_dot(sq[:, c:c + MXU_TILE], ones_ref[...]) for c in range(0, n, MXU_TILE)], axis=1)
    return lax.rsqrt(ssq * (1.0 / HEAD_DIM) + EPS)


def _rope(x, cos, sin):
    lane = lax.broadcasted_iota(jnp.int32, (x.shape[0], LANES), 1)
    first = (lane & (HEAD_DIM // 2 - 1)) < HEAD_DIM // 4
    outs = []
    for c in range(0, x.shape[1], LANES):
        xt = x[:, c:c + LANES]
        sw = jnp.where(first, pltpu.roll(xt, LANES - HEAD_DIM // 4, 1), pltpu.roll(xt, HEAD_DIM // 4, 1))
        outs.append(xt * cos + sw * sin)
    return outs[0] if len(outs) == 1 else jnp.concatenate(outs, axis=1)


def _inproj_kernel(x_ref, sh_ref, sc_ref, g_ref, w_ref, gain_ref, ones_ref, wva_ref, wvb_ref, *rest,
                   groups, use_rope):
    if use_rope:
        cos, sin = rest[0][...], rest[1][...]
        rest = rest[2:]
    n_std = sum(len(subs) for _, _, subs in groups)
    out_refs, vt_refs = rest[:n_std], rest[n_std:]
    geff = g_ref[...] * (1.0 + sc_ref[0])
    shift = sh_ref[0]
    tm = x_ref.shape[1]
    sub = min(tm, INPROJ_SUB_ROWS)
    h_cache = {}

    def h_of(r0):
        if r0 not in h_cache:
            x = x_ref[0, r0:r0 + sub, :]
            ms = jnp.mean(x * x, axis=-1, keepdims=True)
            h_cache[r0] = ((x * lax.rsqrt(ms + EPS)) * geff + shift).astype(BF16)
        return h_cache[r0]

    def project(r0, c0, width, subs):
        return _dot(h_of(r0), w_ref[:, c0:c0 + width])

    def finish(p, r0, c0, width, subs):
        for s0, sw, rope, oi in subs:
            y = p[:, s0:s0 + sw]
            y = y * _group_rms(y, ones_ref) * gain_ref[:, c0 + s0:c0 + s0 + sw]
            if rope:
                y = _rope(y, cos[r0:r0 + sub], sin[r0:r0 + sub])
            out_refs[oi][0, r0:r0 + sub, :] = y.astype(BF16)

    def project_vt(r0, wv_ref, vt_ref):
        return _dot_nt(wv_ref[...], h_of(r0))

    def finish_vt(vt, r0, wv_ref, vt_ref):
        for j in range(sub // LANES):
            vt_ref[0, r0 // LANES + j] = vt[:, j * LANES:(j + 1) * LANES].astype(BF16)

    stages = []
    for r0 in range(0, tm, sub):
        stages += [(project, finish, (r0,) + g) for g in groups]
        stages += [(project_vt, finish_vt, (r0,) + a) for a in zip((wva_ref, wvb_ref), vt_refs)]
    pending = None
    for first, second, args in stages:
        res = first(*args)
        if pending is not None:
            pending[0](pending[1], *pending[2])
        pending = (second, res, args)
    pending[0](pending[1], *pending[2])


def _inproj_call(x, mod3, mod_batched, g, w, gain, ones, wva_t, wvb_t, rope_tabs, groups, out_widths, tm):
    b, s, d = x.shape
    n = w.shape[1]
    use_rope = rope_tabs is not None
    mod_idx = (lambda bi, i, j: (bi, 0, j)) if mod_batched else (lambda bi, i, j: (0, 0, j))
    const = lambda arr: pl.BlockSpec(arr.shape, lambda bi, i: (0, 0))
    in_specs = [
        pl.BlockSpec((1, tm, d), lambda bi, i: (bi, i, 0)),
        pl.BlockSpec((1, 1, d), lambda bi, i: mod_idx(bi, i, 0)),
        pl.BlockSpec((1, 1, d), lambda bi, i: mod_idx(bi, i, 1)),
        const(g), const(w), const(gain), const(ones), const(wva_t), const(wvb_t),
    ]
    args = [x, mod3, mod3, g, w, gain, ones, wva_t, wvb_t]
    if use_rope:
        in_specs += [pl.BlockSpec((tm, LANES), lambda bi, i: (i, 0))] * 2
        args += list(rope_tabs)
    vt_dims = [wva_t.shape[0], wvb_t.shape[0]]
    return pl.pallas_call(
        functools.partial(_inproj_kernel, groups=groups, use_rope=use_rope),
        grid=(b, s // tm),
        in_specs=in_specs,
        out_specs=([pl.BlockSpec((1, tm, wd), lambda bi, i: (bi, i, 0)) for wd in out_widths]
                   + [pl.BlockSpec((1, tm // LANES, vd, LANES), lambda bi, i: (bi, i, 0, 0)) for vd in vt_dims]),
        out_shape=([jax.ShapeDtypeStruct((b, s, wd), BF16) for wd in out_widths]
                   + [jax.ShapeDtypeStruct((b, s // LANES, vd, LANES), BF16) for vd in vt_dims]),
        compiler_params=pltpu.CompilerParams(dimension_semantics=("parallel", "arbitrary"),
                                             vmem_limit_bytes=VMEM_LIMIT),
        name="in_proj_rope" if use_rope else "in_proj_ctx",
    )(*args)


def _scores_t(qs, k_loc, k_ctx, add_loc):
    return _dot_nt(k_loc, qs) + add_loc, _dot_nt(k_ctx, qs)


def _softmax_t(s_loc, s_ctx, extra_logit):
    m = jnp.maximum(jnp.max(s_loc, axis=0, keepdims=True), jnp.max(s_ctx, axis=0, keepdims=True))
    if extra_logit is not None:
        m = jnp.maximum(m, extra_logit)
    return jnp.exp2(s_loc - m).astype(BF16), jnp.exp2(s_ctx - m).astype(BF16), m


def _pv_t(p_loc, p_ctx, m, vt_loc, vt_ctx, extra_logit):
    n_v = vt_loc.shape[0]
    with_ones = lambda vt: jnp.concatenate([vt, jnp.ones((BF16_ROWS, vt.shape[1]), BF16)], axis=0)
    acc = _dot(with_ones(vt_loc), p_loc) + _dot(with_ones(vt_ctx), p_ctx)
    l = acc[n_v:n_v + 1]
    if extra_logit is not None:
        l = l + jnp.exp2(extra_logit - m)
    return acc[:n_v] * (1.0 / l)


def _group_norm_t(o_t, g_t):
    ms = jnp.mean(o_t * o_t, axis=0, keepdims=True)
    return ((o_t * lax.rsqrt(ms + EPS)) * g_t).T.astype(BF16)


def _lane_tiles(ref, first, n, rows=slice(None)):
    return jnp.concatenate([ref[0, first + j, rows, :] for j in range(n)], axis=1)


def _attn_kernel(sink_ref, qa_ref, qb_ref, ka_ref, vat_ref, kb_ref, vbt_ref, kac_ref, vact_ref,
                 kbc_ref, vbct_ref, ga_ref, gb_ref, bias_ref, o_ref, *, seq, n_sub):
    groups = []
    for j in range(n_sub):
        groups += _attn_block_groups(
            pl.program_id(1) * n_sub + j, slice(j * QBLK, (j + 1) * QBLK), sink_ref, qa_ref, qb_ref, ka_ref,
            vat_ref, kb_ref, vbt_ref, kac_ref, vact_ref, kbc_ref, vbct_ref, ga_ref, gb_ref, bias_ref, o_ref, seq)
    scored, soft = [], []
    for scores, finish in groups:
        scored.append(finish(*scores()))
        if len(scored) > ATTN_LOOKAHEAD:
            gen = scored.pop(0)
            next(gen)
            soft.append(gen)
        if len(soft) > ATTN_PV_LAG:
            next(soft.pop(0), None)
    for gen in scored:
        next(gen)
        soft.append(gen)
    for gen in soft:
        next(gen, None)


def _attn_block_groups(i, qrows, sink_ref, qa_ref, qb_ref, ka_ref, vat_ref, kb_ref, vbt_ref, kac_ref, vact_ref,
                       kbc_ref, vbct_ref, ga_ref, gb_ref, bias_ref, o_ref, seq):
    rows = seq // GRID_W
    stages = []
    n_ctx_tiles = kac_ref.shape[1] // LANES
    lane = lax.broadcasted_iota(jnp.int32, (QBLK, LANES), 1)
    lo = lane < HEAD_DIM
    half_masks = (lo.astype(F32).astype(BF16), jnp.logical_not(lo).astype(F32).astype(BF16))

    group = A_HEADS // A_KV_HEADS
    blk_a = jnp.clip(i - 1, 0, seq // QBLK - A_SPAN // QBLK)
    start_a = pl.multiple_of(blk_a * QBLK, QBLK)
    d = (lax.broadcasted_iota(jnp.int32, (A_SPAN, QBLK), 0)
         - lax.broadcasted_iota(jnp.int32, (A_SPAN, QBLK), 1) + (start_a - i * QBLK))
    amask = jnp.where((d <= A_WINDOW) & (d >= -A_WINDOW), 0.0, NEG).astype(F32)
    amask = jnp.concatenate([amask] * A_STACK, axis=1)
    k_loc = ka_ref[0, pl.ds(start_a, A_SPAN), :]
    k_ctx = kac_ref[0]
    q_tiles = [qa_ref[0, qrows, t * LANES:(t + 1) * LANES] for t in range(group)]
    heads_a = []

    def scores_a(kv, t0):
        qs = jnp.concatenate([q * half_masks[kv] for q in q_tiles[t0:t0 + A_STACK]], axis=0)
        return _scores_t(qs, k_loc, k_ctx, amask)

    def finish_a(kv, t0, s_loc, s_ctx):
        dims = slice(kv * HEAD_DIM, (kv + 1) * HEAD_DIM)
        sink = jnp.concatenate(
            [jnp.full((1, QBLK), sink_ref[kv * group + t0 + t] * LOG2E, F32) for t in range(A_STACK)], axis=1)
        p_loc, p_ctx, m = _softmax_t(s_loc, s_ctx, sink)
        yield
        o_t = _pv_t(p_loc, p_ctx, m, _lane_tiles(vat_ref, blk_a, A_SPAN // LANES, dims),
                    _lane_tiles(vact_ref, 0, n_ctx_tiles, dims), sink)
        heads_a.extend(o_t[:, t * QBLK:(t + 1) * QBLK] for t in range(A_STACK))
        if len(heads_a) == A_HEADS:
            o_ref[0, qrows, :A_WIDTH] = _group_norm_t(jnp.concatenate(heads_a, axis=0), ga_ref[...])

    stages += [(functools.partial(scores_a, kv, t0), functools.partial(finish_a, kv, t0))
               for kv in range(A_KV_HEADS) for t0 in range(0, group, A_STACK)]

    blk_b = jnp.clip(NA_ROW_BLOCK * i - NA_KH // 2, 0, rows - B_KROWS) // NA_ROW_BLOCK
    start_b = pl.multiple_of(blk_b * QBLK, QBLK)
    heads_b = []
    n_blk = seq // QBLK
    pattern = jnp.where(i < 2, i, jnp.where(i >= n_blk - 2, i - (n_blk - N_PATTERNS), 2))

    def scores_b(pair):
        cols = slice(pair * LANES, (pair + 1) * LANES)
        qp = qb_ref[0, qrows, cols]
        qs = jnp.concatenate([qp * half_masks[0], qp * half_masks[1]], axis=0)
        return _scores_t(qs, kb_ref[0, pl.ds(start_b, B_SPAN), cols], kbc_ref[0, :, cols], bias_ref[pattern, pair])

    def finish_b(pair, s_loc, s_ctx):
        cols = slice(pair * LANES, (pair + 1) * LANES)
        p_loc, p_ctx, m = _softmax_t(s_loc, s_ctx, None)
        yield
        o_t = _pv_t(p_loc, p_ctx, m, _lane_tiles(vbt_ref, blk_b, B_SPAN // LANES, cols),
                    _lane_tiles(vbct_ref, 0, n_ctx_tiles, cols), None)
        heads_b.extend([o_t[:HEAD_DIM, :QBLK], o_t[HEAD_DIM:, QBLK:]])
        if pair == B_HEADS // 2 - 1:
            o_ref[0, qrows, A_WIDTH:] = _group_norm_t(jnp.concatenate(heads_b, axis=0), gb_ref[...])

    stages += [(functools.partial(scores_b, p), functools.partial(finish_b, p)) for p in range(B_HEADS // 2)]
    return stages


def _attn_call(sink, qa, qb, ka, vat, kb, vbt, kac, vact, kbc, vbct, bias, ga_t, gb_t):
    b, s, _ = qa.shape
    n_blk = s // QBLK
    assert (NA_KH // 2) % NA_ROW_BLOCK == 0 and (s // GRID_W - B_KROWS) % NA_ROW_BLOCK == 0

    n_sub = ATTN_BLOCKS_PER_STEP
    full = lambda arr: pl.BlockSpec((1,) + arr.shape[1:], lambda bi, i: (bi,) + (0,) * (arr.ndim - 1))
    const = lambda arr, **kw: pl.BlockSpec(arr.shape, lambda bi, i: (0,) * arr.ndim, **kw)
    blk = lambda wd: pl.BlockSpec((1, n_sub * QBLK, wd), lambda bi, i: (bi, i, 0))
    return pl.pallas_call(
        functools.partial(_attn_kernel, seq=s, n_sub=n_sub),
        grid=(b, n_blk // n_sub),
        in_specs=[
            pl.BlockSpec(memory_space=pltpu.SMEM),
            blk(A_WIDTH), blk(B_WIDTH),
            full(ka), full(vat), full(kb), full(vbt), full(kac), full(vact), full(kbc), full(vbct),
            const(ga_t), const(gb_t), const(bias, pipeline_mode=pl.Buffered(1)),
        ],
        out_specs=blk(A_WIDTH + B_WIDTH),
        out_shape=jax.ShapeDtypeStruct((b, s, A_WIDTH + B_WIDTH), BF16),
        compiler_params=pltpu.CompilerParams(dimension_semantics=("parallel", "arbitrary"),
                                             vmem_limit_bytes=VMEM_LIMIT),
        name="attn",
    )(sink, qa, qb, ka, vat, kb, vbt, kac, vact, kbc, vbct, ga_t, gb_t, bias)


def _ffn_kernel(x_ref, o_ref, g1_ref, sh2_ref, sc2_ref, g2_ref, n2_ref, wo_ref, wg_ref, wu_ref, wd_ref,
                out_ref):
    geff = n2_ref[...] * (1.0 + sc2_ref[0])
    tm = x_ref.shape[1]
    sub = min(tm, FFN_SUB_ROWS)

    def stage_rows(r0):
        rows = slice(r0, r0 + sub)
        y = _dot(o_ref[0, rows, :], wo_ref[...])
        yield
        x1 = x_ref[0, rows, :] + g1_ref[0] * y
        ms = jnp.mean(x1 * x1, axis=-1, keepdims=True)
        h2 = ((x1 * lax.rsqrt(ms + EPS)) * geff + sh2_ref[0]).astype(BF16)
        gate = _dot(h2, wg_ref[...])
        up = _dot(h2, wu_ref[...])
        yield
        act = (_silu(gate) * up).astype(BF16)
        down = _dot(act, wd_ref[...])
        yield
        out_ref[0, rows, :] = x1 + g2_ref[0] * down

    live = [stage_rows(r0) for r0 in range(0, tm, sub)]
    while live:
        for gen in list(live):
            if next(gen, StopIteration) is StopIteration:
                live.remove(gen)


def _ffn_call(x, o, mod3, n2, wo, wg, wu, wd, tm):
    b, s, d = x.shape
    hid = wg.shape[1]
    modspec = lambda j: pl.BlockSpec((1, 1, d), lambda bi, i: (bi, 0, j))
    const = lambda shape: pl.BlockSpec(shape, lambda bi, i: (0, 0), pipeline_mode=pl.Buffered(1))
    return pl.pallas_call(
        _ffn_kernel,
        grid=(b, s // tm),
        in_specs=[
            pl.BlockSpec((1, tm, d), lambda bi, i: (bi, i, 0)),
            pl.BlockSpec((1, tm, o.shape[2]), lambda bi, i: (bi, i, 0)),
            modspec(2), modspec(3), modspec(4), modspec(5),
            pl.BlockSpec((1, d), lambda bi, i: (0, 0)),
            const((o.shape[2], d)), const((d, hid)), const((d, hid)), const((hid, d)),
        ],
        out_specs=pl.BlockSpec((1, tm, d), lambda bi, i: (bi, i, 0)),
        out_shape=jax.ShapeDtypeStruct((b, s, d), F32),
        compiler_params=pltpu.CompilerParams(dimension_semantics=("parallel", "arbitrary"),
                                             vmem_limit_bytes=VMEM_LIMIT),
        name="out_ffn",
    )(x, o, mod3, mod3, mod3, mod3, n2, wo, wg, wu, wd)


def _layer(x, ctx, mod, norm1_g, w_in, qn_a, kn_a, sink_a, qn_b, kn_b, rpb_b, on_a, on_b, w_out,
           norm2_g, w_gate, w_up, w_down, rope_tabs):
    b, s, d = x.shape
    rows = s // GRID_W
    w_l, w_c, wva_t, wvb_t = _wprep_call(w_in)
    scale = HEAD_DIM ** -0.5 * LOG2E
    gain_l = jnp.concatenate([jnp.tile(qn_a, A_HEADS) * scale, jnp.tile(kn_a, A_KV_HEADS),
                              jnp.tile(qn_b, B_HEADS) * scale, jnp.tile(kn_b, B_HEADS)])[None]
    gain_c = jnp.concatenate([jnp.tile(kn_a, A_KV_HEADS), jnp.tile(kn_b, B_HEADS)])[None]
    ones_bd = jnp.asarray(np.kron(np.eye(MXU_TILE // HEAD_DIM), np.ones((HEAD_DIM, HEAD_DIM))), BF16)

    mod3 = mod[:b].reshape(b, 1, -1)
    mod3_c = mod[b:b + 1].reshape(1, 1, -1)
    g1 = norm1_g[None]

    qka = A_WIDTH + A_KV_WIDTH
    groups_l = [
        (0, qka, [(0, A_WIDTH, True, 0), (A_WIDTH, A_KV_WIDTH, True, 1)]),
        (qka, B_WIDTH, [(0, B_WIDTH, False, 2)]),
        (qka + B_WIDTH, B_WIDTH, [(0, B_WIDTH, False, 3)]),
    ]
    qa, ka, qb, kb, vat, vbt = _inproj_call(
        x, mod3, True, g1, w_l, gain_l, ones_bd, wva_t, wvb_t, rope_tabs, groups_l,
        [A_WIDTH, A_KV_WIDTH, B_WIDTH, B_WIDTH], tm=INPROJ_ROWS)
    groups_c = [(0, A_KV_WIDTH + B_WIDTH, [(0, A_KV_WIDTH, False, 0), (A_KV_WIDTH, B_WIDTH, False, 1)])]
    n_ctx = ctx.shape[1]
    ctx_out = _inproj_call(
        ctx.reshape(1, b * n_ctx, d), mod3_c, False, g1, w_c, gain_c, ones_bd, wva_t, wvb_t, None, groups_c,
        [A_KV_WIDTH, B_WIDTH], tm=CTX_ROWS)
    kac, kbc = (a.reshape(b, n_ctx, -1) for a in ctx_out[:2])
    vact, vbct = (a.reshape(b, n_ctx // LANES, -1, LANES) for a in ctx_out[2:])

    bias = _bias_call(rpb_b, rows)
    lanes_of = lambda g: jnp.broadcast_to(g[:, None], (g.shape[0], LANES))
    o = _attn_call(sink_a, qa, qb, ka, vat, kb, vbt, kac, vact, kbc, vbct, bias, lanes_of(on_a), lanes_of(on_b))
    return _ffn_call(x, o, mod3, norm2_g[None], w_out.astype(BF16), w_gate.astype(BF16), w_up.astype(BF16),
                     w_down.astype(BF16), tm=FFN_ROWS)


def kernel(x, c, ctx, c_ctx, w_mod, b_mod, norm1_g, w_in, qn_a, kn_a, sink_a, qn_b, kn_b, rpb_b, on_a, on_b,
           w_out, norm2_g, w_gate, w_up, w_down):
    b, s, d = x.shape
    depth = w_mod.shape[0]
    assert depth == 1, "the context stream update is only needed when a later layer reads it"
    rope_tabs = _rope_call(s)
    pad = jnp.zeros((16 - b - 1, d), F32)
    cc = jnp.concatenate([c, c_ctx[None], pad], axis=0)
    l = 0
    mod = _mod_call(cc, w_mod[l], b_mod[l][None])
    return _layer(x, ctx, mod, norm1_g[l], w_in[l], qn_a[l], kn_a[l], sink_a[l], qn_b[l], kn_b[l], rpb_b[l],
                  on_a[l], on_b[l], w_out[l], norm2_g[l], w_gate[l], w_up[l], w_down[l], rope_tabs)
```

```python
import functools

import numpy as np
import jax
import jax.numpy as jnp
from jax import lax
from jax.experimental import pallas as pl
from jax.experimental.pallas import tpu as pltpu

GRID_W = 64
HEAD_DIM = 64
A_HEADS = 8
A_KV_HEADS = 2
A_WINDOW = 128
B_HEADS = 8
NA_KH = 8
NA_KW = 16
NA_ROW_BLOCK = 2
ROPE_BASE = 10000.0
EPS = 1e-6

A_WIDTH = A_HEADS * HEAD_DIM
A_KV_WIDTH = A_KV_HEADS * HEAD_DIM
B_WIDTH = B_HEADS * HEAD_DIM
QBLK = NA_ROW_BLOCK * GRID_W
A_SPAN = QBLK + 2 * A_WINDOW
B_KROWS = 10
B_SPAN = B_KROWS * GRID_W
N_PATTERNS = 5
ATTN_BLOCKS_PER_STEP = 8
MOD_COLS = 512
INPROJ_ROWS = 1024
CTX_ROWS = 512
INPROJ_SUB_ROWS = 512
FFN_ROWS = 512
FFN_SUB_ROWS = 256
A_STACK = 4
ATTN_PV_LAG = 1
ATTN_LOOKAHEAD = 1
NEG = -1e30
LOG2E = 1.4426950408889634

LANES = 128
BF16_ROWS = 16
MXU_TILE = 256
VMEM_LIMIT = 56 * 1024 * 1024

F32 = jnp.float32
BF16 = jnp.bfloat16


def _dot(a, b):
    return jnp.dot(a, b, preferred_element_type=F32)


def _dot_nt(a, b):
    return lax.dot_general(a, b, (((1,), (1,)), ((), ())), preferred_element_type=F32)


def _silu(x):
    return x / (1.0 + jnp.exp(-x))


def _mod_kernel(c_ref, w_ref, b_ref, o_ref):
    split = lambda v: (v.astype(BF16), (v - v.astype(BF16).astype(F32)).astype(BF16))
    a_hi, a_lo = split(_silu(c_ref[...]))
    w_hi, w_lo = split(w_ref[...])
    rows = a_hi.shape[0]
    head = _dot(jnp.concatenate([a_hi, a_lo], axis=0), w_hi)
    o_ref[...] = head[:rows] + head[rows:] + _dot(a_hi, w_lo) + b_ref[...]


def _mod_call(cc, w_mod, b_mod):
    rows, d = cc.shape
    n = w_mod.shape[1]
    bn = MOD_COLS
    return pl.pallas_call(
        _mod_kernel,
        grid=(n // bn,),
        in_specs=[pl.BlockSpec((rows, d), lambda j: (0, 0)),
                  pl.BlockSpec((d, bn), lambda j: (0, j)),
                  pl.BlockSpec((1, bn), lambda j: (0, j))],
        out_specs=pl.BlockSpec((rows, bn), lambda j: (0, j)),
        out_shape=jax.ShapeDtypeStruct((rows, n), F32),
        name="mod",
    )(cc, w_mod, b_mod)


def _wprep_kernel(w_ref, wl_ref, wc_ref, wvat_ref, wvbt_ref):
    group = A_HEADS // A_KV_HEADS
    c_ka, c_va, c_qb, c_kb, c_vb = np.cumsum([A_WIDTH, A_KV_WIDTH, A_KV_WIDTH, B_WIDTH, B_WIDTH])
    lane = lax.broadcasted_iota(jnp.int32, (w_ref.shape[0], LANES), 1)
    lo = lane < HEAD_DIM
    for t in range(group):
        src = [w_ref[:, (h // 2) * LANES:(h // 2 + 1) * LANES] for h in (t, t + group)]
        if t % 2 == 0:
            tile = jnp.where(lo, src[0], pltpu.roll(src[1], HEAD_DIM, 1))
        else:
            tile = jnp.where(lo, pltpu.roll(src[0], HEAD_DIM, 1), src[1])
        wl_ref[:, t * LANES:(t + 1) * LANES] = tile.astype(BF16)
    k_a = w_ref[:, c_ka:c_va].astype(BF16)
    k_b = w_ref[:, c_kb:c_vb].astype(BF16)
    wl_ref[:, A_WIDTH:A_WIDTH + A_KV_WIDTH] = k_a
    wl_ref[:, A_WIDTH + A_KV_WIDTH:A_WIDTH + A_KV_WIDTH + B_WIDTH] = w_ref[:, c_qb:c_kb].astype(BF16)
    wl_ref[:, A_WIDTH + A_KV_WIDTH + B_WIDTH:] = k_b
    wc_ref[:, :A_KV_WIDTH] = k_a
    wc_ref[:, A_KV_WIDTH:] = k_b
    wvat_ref[...] = w_ref[:, c_va:c_qb].T.astype(BF16)
    for c in range(0, B_WIDTH, LANES):
        wvbt_ref[c:c + LANES, :] = w_ref[:, c_vb + c:c_vb + c + LANES].T.astype(BF16)


def _wprep_call(w_in):
    d, n = w_in.shape
    assert (A_HEADS // A_KV_HEADS) % 2 == 0 and n == A_WIDTH + 2 * A_KV_WIDTH + 3 * B_WIDTH
    shapes = [(d, A_WIDTH + A_KV_WIDTH + 2 * B_WIDTH), (d, A_KV_WIDTH + B_WIDTH), (A_KV_WIDTH, d), (B_WIDTH, d)]
    return pl.pallas_call(
        _wprep_kernel,
        grid=(1,),
        in_specs=[pl.BlockSpec((d, n), lambda i: (0, 0))],
        out_specs=[pl.BlockSpec(s, lambda i: (0, 0)) for s in shapes],
        out_shape=[jax.ShapeDtypeStruct(s, BF16) for s in shapes],
        compiler_params=pltpu.CompilerParams(vmem_limit_bytes=VMEM_LIMIT),
        name="w_in_layout",
    )(w_in)


def _rope_kernel(invf_ref, cos_ref, sin_ref, *, rows):
    n_pos = max(rows, GRID_W)
    pos = lax.broadcasted_iota(jnp.int32, (n_pos, LANES), 0).astype(F32)
    lane = lax.broadcasted_iota(jnp.int32, (GRID_W, LANES), 1)
    row_lanes = (lane & (HEAD_DIM - 1)) < HEAD_DIM // 2
    first = (lane & (HEAD_DIM // 2 - 1)) < HEAD_DIM // 4
    ang = pos * invf_ref[...]
    cos_p, sin_p = jnp.cos(ang), jnp.sin(ang)
    cos_c = cos_p[:GRID_W]
    sin_c = jnp.where(first, -sin_p[:GRID_W], sin_p[:GRID_W])
    for r in range(rows):
        tok = slice(r * GRID_W, (r + 1) * GRID_W)
        cos_ref[tok, :] = jnp.where(row_lanes, cos_p[r:r + 1], cos_c)
        sin_ref[tok, :] = jnp.where(row_lanes, jnp.where(first, -sin_p[r:r + 1], sin_p[r:r + 1]), sin_c)


def _rope_call(seq):
    quarter = HEAD_DIM // 4
    inv = (1.0 / (np.float32(ROPE_BASE) ** (np.arange(quarter, dtype=np.float32) / quarter))).astype(np.float32)
    invf = jnp.asarray(np.tile(inv, LANES // quarter)[None, :])
    return pl.pallas_call(
        functools.partial(_rope_kernel, rows=seq // GRID_W),
        grid=(1,),
        in_specs=[pl.BlockSpec((1, LANES), lambda i: (0, 0))],
        out_specs=[pl.BlockSpec((seq, LANES), lambda i: (0, 0))] * 2,
        out_shape=[jax.ShapeDtypeStruct((seq, LANES), F32)] * 2,
        name="rope_tables",
    )(invf)


def _b_patterns(rows):
    n_blk = rows // NA_ROW_BLOCK
    blocks = [0, 1, 2, n_blk - 2, n_blk - 1]
    return [(NA_ROW_BLOCK * i, _b_key_start(i, rows)) for i in blocks]


def _b_key_start(i, rows):
    return min(max(NA_ROW_BLOCK * i - NA_KH // 2, 0), rows - B_KROWS)


def _bias_kernel(rpb_ref, o_ref, *, patterns, rows):
    h = pl.program_id(0)
    n_dr, n_dc = 2 * NA_KH - 1, 2 * NA_KW - 1
    kc = lax.broadcasted_iota(jnp.int32, (GRID_W, LANES), 0)
    lane = lax.broadcasted_iota(jnp.int32, (GRID_W, LANES), 1)
    qc = lane & (GRID_W - 1)
    dc = jnp.clip(kc - qc + NA_KW - 1, 0, n_dc - 1)
    cs = jnp.clip(qc - NA_KW // 2, 0, GRID_W - NA_KW)
    col_ok = (kc >= cs) & (kc < cs + NA_KW)
    lo = lane < GRID_W
    base = h * (n_dr * n_dc)
    per_dr = []
    for dr in range(n_dr):
        m = jnp.zeros((GRID_W, LANES), F32)
        for d in range(n_dc):
            m = jnp.where(dc == d, rpb_ref[base + dr * n_dc + d] * LOG2E, m)
        per_dr.append(m)
    neg = jnp.full((GRID_W, LANES), NEG, F32)
    for p, (r0, ks) in enumerate(patterns):
        q_rows = [r0 + qr for qr in range(NA_ROW_BLOCK)]
        rs = [min(max(q - NA_KH // 2, 0), rows - NA_KH) for q in q_rows]
        for kr in range(B_KROWS):
            k_row = ks + kr
            ok = [r <= k_row < r + NA_KH for r in rs]
            drs = [min(max(k_row - q + NA_KH - 1, 0), n_dr - 1) for q in q_rows]
            if not any(ok):
                piece = neg
            else:
                vals = jnp.where(lo, per_dr[drs[0]], per_dr[drs[1]])
                mask = col_ok
                if not ok[1]:
                    mask = mask & lo
                if not ok[0]:
                    mask = mask & jnp.logical_not(lo)
                piece = jnp.where(mask, vals, neg)
            o_ref[p, 0, kr * GRID_W:(kr + 1) * GRID_W, :] = piece


def _bias_call(rpb, rows):
    patterns = _b_patterns(rows)
    heads = rpb.shape[0]
    return pl.pallas_call(
        functools.partial(_bias_kernel, patterns=patterns, rows=rows),
        grid=(heads,),
        in_specs=[pl.BlockSpec(memory_space=pltpu.SMEM)],
        out_specs=pl.BlockSpec((N_PATTERNS, 1, B_SPAN, QBLK), lambda h: (0, h // 2, 0, h % 2)),
        out_shape=jax.ShapeDtypeStruct((N_PATTERNS, heads // 2, B_SPAN, 2 * QBLK), F32),
        name="na_bias",
    )(rpb.reshape(-1))


def _group_rms(p, ones_ref):
    sq = (p * p).astype(BF16)
    n = p.shape[1]
    if n <= MXU_TILE:
        ssq = _dot(sq, ones_ref[:n, :n])
    else:
        ssq = jnp.concatenate(
            [_dot(sq[:, c:c + MXU_TILE], ones_ref[...]) for c in range(0, n, MXU_TILE)], axis=1)
    return lax.rsqrt(ssq * (1.0 / HEAD_DIM) + EPS)


def _rope(x, cos, sin):
    lane = lax.broadcasted_iota(jnp.int32, (x.shape[0], LANES), 1)
    first = (lane & (HEAD_DIM // 2 - 1)) < HEAD_DIM // 4
    outs = []
    for c in range(0, x.shape[1], LANES):
        xt = x[:, c:c + LANES]
        sw = jnp.where(first, pltpu.roll(xt, LANES - HEAD_DIM // 4, 1), pltpu.roll(xt, HEAD_DIM // 4, 1))
        outs.append(xt * cos + sw * sin)
    return outs[0] if len(outs) == 1 else jnp.concatenate(outs, axis=1)


def _inproj_kernel(x_ref, sh_ref, sc_ref, g_ref, w_ref, gain_ref, ones_ref, wva_ref, wvb_ref, *rest,
                   groups, use_rope):
    if use_rope:
        cos, sin = rest[0][...], rest[1][...]
        rest = rest[2:]
    n_std = sum(len(subs) for _, _, subs in groups)
    out_refs, vt_refs = rest[:n_std], rest[n_std:]
    geff = g_ref[...] * (1.0 + sc_ref[0])
    shift = sh_ref[0]
    tm = x_ref.shape[1]
    sub = min(tm, INPROJ_SUB_ROWS)
    h_cache = {}

    def h_of(r0):
        if r0 not in h_cache:
            x = x_ref[0, r0:r0 + sub, :]
            ms = jnp.mean(x * x, axis=-1, keepdims=True)
            h_cache[r0] = ((x * lax.rsqrt(ms + EPS)) * geff + shift).astype(BF16)
        return h_cache[r0]

    def project(r0, c0, width, subs):
        return _dot(h_of(r0), w_ref[:, c0:c0 + width])

    def finish(p, r0, c0, width, subs):
        for s0, sw, rope, oi in subs:
            y = p[:, s0:s0 + sw]
            y = y * _group_rms(y, ones_ref) * gain_ref[:, c0 + s0:c0 + s0 + sw]
            if rope:
                y = _rope(y, cos[r0:r0 + sub], sin[r0:r0 + sub])
            out_refs[oi][0, r0:r0 + sub, :] = y.astype(BF16)

    def project_vt(r0, wv_ref, vt_ref):
        return _dot_nt(wv_ref[...], h_of(r0))

    def finish_vt(vt, r0, wv_ref, vt_ref):
        for j in range(sub // LANES):
            vt_ref[0, r0 // LANES + j] = vt[:, j * LANES:(j + 1) * LANES].astype(BF16)

    stages = []
    for r0 in range(0, tm, sub):
        stages += [(project, finish, (r0,) + g) for g in groups]
        stages += [(project_vt, finish_vt, (r0,) + a) for a in zip((wva_ref, wvb_ref), vt_refs)]
    pending = None
    for first, second, args in stages:
        res = first(*args)
        if pending is not None:
            pending[0](pending[1], *pending[2])
        pending = (second, res, args)
    pending[0](pending[1], *pending[2])


def _inproj_call(x, mod3, mod_batched, g, w, gain, ones, wva_t, wvb_t, rope_tabs, groups, out_widths, tm):
    b, s, d = x.shape
    n = w.shape[1]
    use_rope = rope_tabs is not None
    mod_idx = (lambda bi, i, j: (bi, 0, j)) if mod_batched else (lambda bi, i, j: (0, 0, j))
    const = lambda arr: pl.BlockSpec(arr.shape, lambda bi, i: (0, 0))
    in_specs = [
        pl.BlockSpec((1, tm, d), lambda bi, i: (bi, i, 0)),
        pl.BlockSpec((1, 1, d), lambda bi, i: mod_idx(bi, i, 0)),
        pl.BlockSpec((1, 1, d), lambda bi, i: mod_idx(bi, i, 1)),
        const(g), const(w), const(gain), const(ones), const(wva_t), const(wvb_t),
    ]
    args = [x, mod3, mod3, g, w, gain, ones, wva_t, wvb_t]
    if use_rope:
        in_specs += [pl.BlockSpec((tm, LANES), lambda bi, i: (i, 0))] * 2
        args += list(rope_tabs)
    vt_dims = [wva_t.shape[0], wvb_t.shape[0]]
    return pl.pallas_call(
        functools.partial(_inproj_kernel, groups=groups, use_rope=use_rope),
        grid=(b, s // tm),
        in_specs=in_specs,
        out_specs=([pl.BlockSpec((1, tm, wd), lambda bi, i: (bi, i, 0)) for wd in out_widths]
                   + [pl.BlockSpec((1, tm // LANES, vd, LANES), lambda bi, i: (bi, i, 0, 0)) for vd in vt_dims]),
        out_shape=([jax.ShapeDtypeStruct((b, s, wd), BF16) for wd in out_widths]
                   + [jax.ShapeDtypeStruct((b, s // LANES, vd, LANES), BF16) for vd in vt_dims]),
        compiler_params=pltpu.CompilerParams(dimension_semantics=("parallel", "arbitrary"),
                                             vmem_limit_bytes=VMEM_LIMIT),
        name="in_proj_rope" if use_rope else "in_proj_ctx",
    )(*args)


def _scores_t(qs, k_loc, k_ctx, add_loc):
    return _dot_nt(k_loc, qs) + add_loc, _dot_nt(k_ctx, qs)


def _softmax_t(s_loc, s_ctx, extra_logit):
    m = jnp.maximum(jnp.max(s_loc, axis=0, keepdims=True), jnp.max(s_ctx, axis=0, keepdims=True))
    if extra_logit is not None:
        m = jnp.maximum(m, extra_logit)
    return jnp.exp2((s_loc - m).astype(BF16)), jnp.exp2((s_ctx - m).astype(BF16)), m


def _pv_t(p_loc, p_ctx, m, vt_loc, vt_ctx, extra_logit):
    n_v = vt_loc.shape[0]
    with_ones = lambda vt: jnp.concatenate([vt, jnp.ones((BF16_ROWS, vt.shape[1]), BF16)], axis=0)
    acc = _dot(with_ones(vt_loc), p_loc) + _dot(with_ones(vt_ctx), p_ctx)
    l = acc[n_v:n_v + 1]
    if extra_logit is not None:
        l = l + jnp.exp2(extra_logit - m)
    return acc[:n_v] * (1.0 / l)


def _group_norm_t(o_t, g_t):
    ms = jnp.mean(o_t * o_t, axis=0, keepdims=True)
    return ((o_t * lax.rsqrt(ms + EPS)) * g_t).T.astype(BF16)


def _lane_tiles(ref, first, n, rows=slice(None)):
    return jnp.concatenate([ref[0, first + j, rows, :] for j in range(n)], axis=1)


def _attn_kernel(sink_ref, qa_ref, qb_ref, ka_ref, vat_ref, kb_ref, vbt_ref, kac_ref, vact_ref,
                 kbc_ref, vbct_ref, ga_ref, gb_ref, bias_ref, o_ref, *, seq, n_sub):
    groups = []
    for j in range(n_sub):
        groups += _attn_block_groups(
            pl.program_id(1) * n_sub + j, slice(j * QBLK, (j + 1) * QBLK), sink_ref, qa_ref, qb_ref, ka_ref,
            vat_ref, kb_ref, vbt_ref, kac_ref, vact_ref, kbc_ref, vbct_ref, ga_ref, gb_ref, bias_ref, o_ref, seq)
    scored, soft = [], []
    for scores, finish in groups:
        scored.append(finish(*scores()))
        if len(scored) > ATTN_LOOKAHEAD:
            gen = scored.pop(0)
            next(gen)
            soft.append(gen)
        if len(soft) > ATTN_PV_LAG:
            next(soft.pop(0), None)
    for gen in scored:
        next(gen)
        soft.append(gen)
    for gen in soft:
        next(gen, None)


def _attn_block_groups(i, qrows, sink_ref, qa_ref, qb_ref, ka_ref, vat_ref, kb_ref, vbt_ref, kac_ref, vact_ref,
                       kbc_ref, vbct_ref, ga_ref, gb_ref, bias_ref, o_ref, seq):
    rows = seq // GRID_W
    stages = []
    n_ctx_tiles = kac_ref.shape[1] // LANES
    lane = lax.broadcasted_iota(jnp.int32, (QBLK, LANES), 1)
    lo = lane < HEAD_DIM
    half_masks = (lo.astype(F32).astype(BF16), jnp.logical_not(lo).astype(F32).astype(BF16))

    group = A_HEADS // A_KV_HEADS
    blk_a = jnp.clip(i - 1, 0, seq // QBLK - A_SPAN // QBLK)
    start_a = pl.multiple_of(blk_a * QBLK, QBLK)
    d = (lax.broadcasted_iota(jnp.int32, (A_SPAN, QBLK), 0)
         - lax.broadcasted_iota(jnp.int32, (A_SPAN, QBLK), 1) + (start_a - i * QBLK))
    amask = jnp.where((d <= A_WINDOW) & (d >= -A_WINDOW), 0.0, NEG).astype(F32)
    amask = jnp.concatenate([amask] * A_STACK, axis=1)
    k_loc = ka_ref[0, pl.ds(start_a, A_SPAN), :]
    k_ctx = kac_ref[0]
    q_tiles = [qa_ref[0, qrows, t * LANES:(t + 1) * LANES] for t in range(group)]
    heads_a = []

    def scores_a(kv, t0):
        qs = jnp.concatenate([q * half_masks[kv] for q in q_tiles[t0:t0 + A_STACK]], axis=0)
        return _scores_t(qs, k_loc, k_ctx, amask)

    def finish_a(kv, t0, s_loc, s_ctx):
        dims = slice(kv * HEAD_DIM, (kv + 1) * HEAD_DIM)
        sink = jnp.concatenate(
            [jnp.full((1, QBLK), sink_ref[kv * group + t0 + t] * LOG2E, F32) for t in range(A_STACK)], axis=1)
        p_loc, p_ctx, m = _softmax_t(s_loc, s_ctx, sink)
        yield
        o_t = _pv_t(p_loc, p_ctx, m, _lane_tiles(vat_ref, blk_a, A_SPAN // LANES, dims),
                    _lane_tiles(vact_ref, 0, n_ctx_tiles, dims), sink)
        heads_a.extend(o_t[:, t * QBLK:(t + 1) * QBLK] for t in range(A_STACK))
        if len(heads_a) == A_HEADS:
            o_ref[0, qrows, :A_WIDTH] = _group_norm_t(jnp.concatenate(heads_a, axis=0), ga_ref[...])

    stages += [(functools.partial(scores_a, kv, t0), functools.partial(finish_a, kv, t0))
               for kv in range(A_KV_HEADS) for t0 in range(0, group, A_STACK)]

    blk_b = jnp.clip(NA_ROW_BLOCK * i - NA_KH // 2, 0, rows - B_KROWS) // NA_ROW_BLOCK
    start_b = pl.multiple_of(blk_b * QBLK, QBLK)
    heads_b = []
    n_blk = seq // QBLK
    pattern = jnp.where(i < 2, i, jnp.where(i >= n_blk - 2, i - (n_blk - N_PATTERNS), 2))

    def scores_b(pair):
        cols = slice(pair * LANES, (pair + 1) * LANES)
        qp = qb_ref[0, qrows, cols]
        qs = jnp.concatenate([qp * half_masks[0], qp * half_masks[1]], axis=0)
        return _scores_t(qs, kb_ref[0, pl.ds(start_b, B_SPAN), cols], kbc_ref[0, :, cols], bias_ref[pattern, pair])

    def finish_b(pair, s_loc, s_ctx):
        cols = slice(pair * LANES, (pair + 1) * LANES)
        p_loc, p_ctx, m = _softmax_t(s_loc, s_ctx, None)
        yield
        o_t = _pv_t(p_loc, p_ctx, m, _lane_tiles(vbt_ref, blk_b, B_SPAN // LANES, cols),
                    _lane_tiles(vbct_ref, 0, n_ctx_tiles, cols), None)
        heads_b.extend([o_t[:HEAD_DIM, :QBLK], o_t[HEAD_DIM:, QBLK:]])
        if pair == B_HEADS // 2 - 1:
            o_ref[0, qrows, A_WIDTH:] = _group_norm_t(jnp.concatenate(heads_b, axis=0), gb_ref[...])

    stages += [(functools.partial(scores_b, p), functools.partial(finish_b, p)) for p in range(B_HEADS // 2)]
    return stages


def _attn_call(sink, qa, qb, ka, vat, kb, vbt, kac, vact, kbc, vbct, bias, ga_t, gb_t):
    b, s, _ = qa.shape
    n_blk = s // QBLK
    assert (NA_KH // 2) % NA_ROW_BLOCK == 0 and (s // GRID_W - B_KROWS) % NA_ROW_BLOCK == 0

    n_sub = ATTN_BLOCKS_PER_STEP
    full = lambda arr: pl.BlockSpec((1,) + arr.shape[1:], lambda bi, i: (bi,) + (0,) * (arr.ndim - 1))
    const = lambda arr, **kw: pl.BlockSpec(arr.shape, lambda bi, i: (0,) * arr.ndim, **kw)
    blk = lambda wd: pl.BlockSpec((1, n_sub * QBLK, wd), lambda bi, i: (bi, i, 0))
    return pl.pallas_call(
        functools.partial(_attn_kernel, seq=s, n_sub=n_sub),
        grid=(b, n_blk // n_sub),
        in_specs=[
            pl.BlockSpec(memory_space=pltpu.SMEM),
            blk(A_WIDTH), blk(B_WIDTH),
            full(ka), full(vat), full(kb), full(vbt), full(kac), full(vact), full(kbc), full(vbct),
            const(ga_t), const(gb_t), const(bias, pipeline_mode=pl.Buffered(1)),
        ],
        out_specs=blk(A_WIDTH + B_WIDTH),
        out_shape=jax.ShapeDtypeStruct((b, s, A_WIDTH + B_WIDTH), BF16),
        compiler_params=pltpu.CompilerParams(dimension_semantics=("parallel", "arbitrary"),
                                             vmem_limit_bytes=VMEM_LIMIT),
        name="attn",
    )(sink, qa, qb, ka, vat, kb, vbt, kac, vact, kbc, vbct, ga_t, gb_t, bias)


def _ffn_kernel(x_ref, o_ref, g1_ref, sh2_ref, sc2_ref, g2_ref, n2_ref, wo_ref, wg_ref, wu_ref, wd_ref,
                out_ref):
    geff = n2_ref[...] * (1.0 + sc2_ref[0])
    tm = x_ref.shape[1]
    sub = min(tm, FFN_SUB_ROWS)

    def stage_rows(r0):
        rows = slice(r0, r0 + sub)
        y = _dot(o_ref[0, rows, :], wo_ref[...])
        yield
        x1 = x_ref[0, rows, :] + g1_ref[0] * y
        ms = jnp.mean(x1 * x1, axis=-1, keepdims=True)
        h2 = ((x1 * lax.rsqrt(ms + EPS)) * geff + sh2_ref[0]).astype(BF16)
        gate = _dot(h2, wg_ref[...])
        up = _dot(h2, wu_ref[...])
        yield
        act = (_silu(gate) * up).astype(BF16)
        down = _dot(act, wd_ref[...])
        yield
        out_ref[0, rows, :] = x1 + g2_ref[0] * down

    live = [stage_rows(r0) for r0 in range(0, tm, sub)]
    while live:
        for gen in list(live):
            if next(gen, StopIteration) is StopIteration:
                live.remove(gen)


def _ffn_call(x, o, mod3, n2, wo, wg, wu, wd, tm):
    b, s, d = x.shape
    hid = wg.shape[1]
    modspec = lambda j: pl.BlockSpec((1, 1, d), lambda bi, i: (bi, 0, j))
    const = lambda shape: pl.BlockSpec(shape, lambda bi, i: (0, 0), pipeline_mode=pl.Buffered(1))
    return pl.pallas_call(
        _ffn_kernel,
        grid=(b, s // tm),
        in_specs=[
            pl.BlockSpec((1, tm, d), lambda bi, i: (bi, i, 0)),
            pl.BlockSpec((1, tm, o.shape[2]), lambda bi, i: (bi, i, 0)),
            modspec(2), modspec(3), modspec(4), modspec(5),
            pl.BlockSpec((1, d), lambda bi, i: (0, 0)),
            const((o.shape[2], d)), const((d, hid)), const((d, hid)), const((hid, d)),
        ],
        out_specs=pl.BlockSpec((1, tm, d), lambda bi, i: (bi, i, 0)),
        out_shape=jax.ShapeDtypeStruct((b, s, d), F32),
        compiler_params=pltpu.CompilerParams(dimension_semantics=("parallel", "arbitrary"),
                                             vmem_limit_bytes=VMEM_LIMIT),
        name="out_ffn",
    )(x, o, mod3, mod3, mod3, mod3, n2, wo, wg, wu, wd)


def _layer(x, ctx, mod, norm1_g, w_in, qn_a, kn_a, sink_a, qn_b, kn_b, rpb_b, on_a, on_b, w_out,
           norm2_g, w_gate, w_up, w_down, rope_tabs):
    b, s, d = x.shape
    rows = s // GRID_W
    w_l, w_c, wva_t, wvb_t = _wprep_call(w_in)
    scale = HEAD_DIM ** -0.5 * LOG2E
    gain_l = jnp.concatenate([jnp.tile(qn_a, A_HEADS) * scale, jnp.tile(kn_a, A_KV_HEADS),
                              jnp.tile(qn_b, B_HEADS) * scale, jnp.tile(kn_b, B_HEADS)])[None]
    gain_c = jnp.concatenate([jnp.tile(kn_a, A_KV_HEADS), jnp.tile(kn_b, B_HEADS)])[None]
    ones_bd = jnp.asarray(np.kron(np.eye(MXU_TILE // HEAD_DIM), np.ones((HEAD_DIM, HEAD_DIM))), BF16)

    mod3 = mod[:b].reshape(b, 1, -1)
    mod3_c = mod[b:b + 1].reshape(1, 1, -1)
    g1 = norm1_g[None]

    qka = A_WIDTH + A_KV_WIDTH
    groups_l = [
        (0, qka, [(0, A_WIDTH, True, 0), (A_WIDTH, A_KV_WIDTH, True, 1)]),
        (qka, B_WIDTH, [(0, B_WIDTH, False, 2)]),
        (qka + B_WIDTH, B_WIDTH, [(0, B_WIDTH, False, 3)]),
    ]
    qa, ka, qb, kb, vat, vbt = _inproj_call(
        x, mod3, True, g1, w_l, gain_l, ones_bd, wva_t, wvb_t, rope_tabs, groups_l,
        [A_WIDTH, A_KV_WIDTH, B_WIDTH, B_WIDTH], tm=INPROJ_ROWS)
    groups_c = [(0, A_KV_WIDTH + B_WIDTH, [(0, A_KV_WIDTH, False, 0), (A_KV_WIDTH, B_WIDTH, False, 1)])]
    n_ctx = ctx.shape[1]
    ctx_out = _inproj_call(
        ctx.reshape(1, b * n_ctx, d), mod3_c, False, g1, w_c, gain_c, ones_bd, wva_t, wvb_t, None, groups_c,
        [A_KV_WIDTH, B_WIDTH], tm=CTX_ROWS)
    kac, kbc = (a.reshape(b, n_ctx, -1) for a in ctx_out[:2])
    vact, vbct = (a.reshape(b, n_ctx // LANES, -1, LANES) for a in ctx_out[2:])

    bias = _bias_call(rpb_b, rows)
    lanes_of = lambda g: jnp.broadcast_to(g[:, None], (g.shape[0], LANES))
    o = _attn_call(sink_a, qa, qb, ka, vat, kb, vbt, kac, vact, kbc, vbct, bias, lanes_of(on_a), lanes_of(on_b))
    return _ffn_call(x, o, mod3, norm2_g[None], w_out.astype(BF16), w_gate.astype(BF16), w_up.astype(BF16),
                     w_down.astype(BF16), tm=FFN_ROWS)


def kernel(x, c, ctx, c_ctx, w_mod, b_mod, norm1_g, w_in, qn_a, kn_a, sink_a, qn_b, kn_b, rpb_b, on_a, on_b,
           w_out, norm2_g, w_gate, w_up, w_down):
    b, s, d = x.shape
    depth = w_mod.shape[0]
    assert depth == 1, "the context stream update is only needed when a later layer reads it"
    rope_tabs = _rope_call(s)
    n_rows = -(-(b + 1) // BF16_ROWS) * BF16_ROWS
    cc = jnp.concatenate([c, c_ctx[None], jnp.zeros((n_rows - b - 1, d), F32)], axis=0)
    l = 0
    mod = _mod_call(cc, w_mod[l], b_mod[l][None])
    return _layer(x, ctx, mod, norm1_g[l], w_in[l], qn_a[l], kn_a[l], sink_a[l], qn_b[l], kn_b[l], rpb_b[l],
                  on_a[l], on_b[l], w_out[l], norm2_g[l], w_gate[l], w_up[l], w_down[l], rope_tabs)
```

```python
import functools

import numpy as np
import jax
import jax.numpy as jnp
from jax import lax
from jax.experimental import pallas as pl
from jax.experimental.pallas import tpu as pltpu

GRID_W = 64
HEAD_DIM = 64
A_HEADS = 8
A_KV_HEADS = 2
A_WINDOW = 128
B_HEADS = 8
NA_KH = 8
NA_KW = 16
NA_ROW_BLOCK = 2
ROPE_BASE = 10000.0
EPS = 1e-6

A_WIDTH = A_HEADS * HEAD_DIM
A_KV_WIDTH = A_KV_HEADS * HEAD_DIM
B_WIDTH = B_HEADS * HEAD_DIM
QBLK = NA_ROW_BLOCK * GRID_W
A_SPAN = QBLK + 2 * A_WINDOW
B_KROWS = 10
B_SPAN = B_KROWS * GRID_W
N_PATTERNS = 5
ATTN_BLOCKS_PER_STEP = 8
MOD_COLS = 2048
INPROJ_ROWS = 2048
CTX_ROWS = 512
INPROJ_SUB_ROWS = 512
FFN_ROWS = 1024
FFN_SUB_ROWS = 256
A_STACK = 4
ATTN_PV_LAG = 1
ATTN_LOOKAHEAD = 1
NEG = -1e30
LOG2E = 1.4426950408889634

LANES = 128
BF16_ROWS = 16
MXU_TILE = 256
VMEM_LIMIT = 56 * 1024 * 1024

F32 = jnp.float32
BF16 = jnp.bfloat16


def _dot(a, b):
    return jnp.dot(a, b, preferred_element_type=F32)


def _dot_nt(a, b):
    return lax.dot_general(a, b, (((1,), (1,)), ((), ())), preferred_element_type=F32)


def _silu(x):
    return x / (1.0 + jnp.exp(-x))


def _mod_kernel(c_ref, w_ref, b_ref, o_ref):
    split = lambda v: (v.astype(BF16), (v - v.astype(BF16).astype(F32)).astype(BF16))
    a_hi, a_lo = split(_silu(c_ref[...]))
    w_hi, w_lo = split(w_ref[...])
    rows = a_hi.shape[0]
    head = _dot(jnp.concatenate([a_hi, a_lo], axis=0), w_hi)
    o_ref[...] = head[:rows] + head[rows:] + _dot(a_hi, w_lo) + b_ref[...]


def _mod_call(cc, w_mod, b_mod):
    rows, d = cc.shape
    n = w_mod.shape[1]
    bn = MOD_COLS
    return pl.pallas_call(
        _mod_kernel,
        grid=(n // bn,),
        in_specs=[pl.BlockSpec((rows, d), lambda j: (0, 0)),
                  pl.BlockSpec((d, bn), lambda j: (0, j)),
                  pl.BlockSpec((1, bn), lambda j: (0, j))],
        out_specs=pl.BlockSpec((rows, bn), lambda j: (0, j)),
        out_shape=jax.ShapeDtypeStruct((rows, n), F32),
        name="mod",
    )(cc, w_mod, b_mod)


def _wprep_kernel(w_ref, wl_ref, wc_ref, wvat_ref, wvbt_ref):
    group = A_HEADS // A_KV_HEADS
    c_ka, c_va, c_qb, c_kb, c_vb = np.cumsum([A_WIDTH, A_KV_WIDTH, A_KV_WIDTH, B_WIDTH, B_WIDTH])
    lane = lax.broadcasted_iota(jnp.int32, (w_ref.shape[0], LANES), 1)
    lo = lane < HEAD_DIM
    for t in range(group):
        src = [w_ref[:, (h // 2) * LANES:(h // 2 + 1) * LANES] for h in (t, t + group)]
        if t % 2 == 0:
            tile = jnp.where(lo, src[0], pltpu.roll(src[1], HEAD_DIM, 1))
        else:
            tile = jnp.where(lo, pltpu.roll(src[0], HEAD_DIM, 1), src[1])
        wl_ref[:, t * LANES:(t + 1) * LANES] = tile.astype(BF16)
    k_a = w_ref[:, c_ka:c_va].astype(BF16)
    k_b = w_ref[:, c_kb:c_vb].astype(BF16)
    wl_ref[:, A_WIDTH:A_WIDTH + A_KV_WIDTH] = k_a
    wl_ref[:, A_WIDTH + A_KV_WIDTH:A_WIDTH + A_KV_WIDTH + B_WIDTH] = w_ref[:, c_qb:c_kb].astype(BF16)
    wl_ref[:, A_WIDTH + A_KV_WIDTH + B_WIDTH:] = k_b
    wc_ref[:, :A_KV_WIDTH] = k_a
    wc_ref[:, A_KV_WIDTH:] = k_b
    wvat_ref[...] = w_ref[:, c_va:c_qb].T.astype(BF16)
    for c in range(0, B_WIDTH, LANES):
        wvbt_ref[c:c + LANES, :] = w_ref[:, c_vb + c:c_vb + c + LANES].T.astype(BF16)


def _wprep_call(w_in):
    d, n = w_in.shape
    assert (A_HEADS // A_KV_HEADS) % 2 == 0 and n == A_WIDTH + 2 * A_KV_WIDTH + 3 * B_WIDTH
    shapes = [(d, A_WIDTH + A_KV_WIDTH + 2 * B_WIDTH), (d, A_KV_WIDTH + B_WIDTH), (A_KV_WIDTH, d), (B_WIDTH, d)]
    return pl.pallas_call(
        _wprep_kernel,
        grid=(1,),
        in_specs=[pl.BlockSpec((d, n), lambda i: (0, 0))],
        out_specs=[pl.BlockSpec(s, lambda i: (0, 0)) for s in shapes],
        out_shape=[jax.ShapeDtypeStruct(s, BF16) for s in shapes],
        compiler_params=pltpu.CompilerParams(vmem_limit_bytes=VMEM_LIMIT),
        name="w_in_layout",
    )(w_in)


def _rope_kernel(invf_ref, cos_ref, sin_ref, *, rows):
    n_pos = max(rows, GRID_W)
    pos = lax.broadcasted_iota(jnp.int32, (n_pos, LANES), 0).astype(F32)
    lane = lax.broadcasted_iota(jnp.int32, (GRID_W, LANES), 1)
    row_lanes = (lane & (HEAD_DIM - 1)) < HEAD_DIM // 2
    first = (lane & (HEAD_DIM // 2 - 1)) < HEAD_DIM // 4
    ang = pos * invf_ref[...]
    cos_p, sin_p = jnp.cos(ang), jnp.sin(ang)
    cos_c = cos_p[:GRID_W]
    sin_c = jnp.where(first, -sin_p[:GRID_W], sin_p[:GRID_W])
    for r in range(rows):
        tok = slice(r * GRID_W, (r + 1) * GRID_W)
        cos_ref[tok, :] = jnp.where(row_lanes, cos_p[r:r + 1], cos_c)
        sin_ref[tok, :] = jnp.where(row_lanes, jnp.where(first, -sin_p[r:r + 1], sin_p[r:r + 1]), sin_c)


def _rope_call(seq):
    quarter = HEAD_DIM // 4
    inv = (1.0 / (np.float32(ROPE_BASE) ** (np.arange(quarter, dtype=np.float32) / quarter))).astype(np.float32)
    invf = jnp.asarray(np.tile(inv, LANES // quarter)[None, :])
    return pl.pallas_call(
        functools.partial(_rope_kernel, rows=seq // GRID_W),
        grid=(1,),
        in_specs=[pl.BlockSpec((1, LANES), lambda i: (0, 0))],
        out_specs=[pl.BlockSpec((seq, LANES), lambda i: (0, 0))] * 2,
        out_shape=[jax.ShapeDtypeStruct((seq, LANES), F32)] * 2,
        name="rope_tables",
    )(invf)


def _b_patterns(rows):
    n_blk = rows // NA_ROW_BLOCK
    blocks = [0, 1, 2, n_blk - 2, n_blk - 1]
    return [(NA_ROW_BLOCK * i, _b_key_start(i, rows)) for i in blocks]


def _b_key_start(i, rows):
    return min(max(NA_ROW_BLOCK * i - NA_KH // 2, 0), rows - B_KROWS)


def _bias_kernel(rpb_ref, o_ref, *, patterns, rows):
    h = pl.program_id(0)
    n_dr, n_dc = 2 * NA_KH - 1, 2 * NA_KW - 1
    kc = lax.broadcasted_iota(jnp.int32, (GRID_W, LANES), 0)
    lane = lax.broadcasted_iota(jnp.int32, (GRID_W, LANES), 1)
    qc = lane & (GRID_W - 1)
    dc = jnp.clip(kc - qc + NA_KW - 1, 0, n_dc - 1)
    cs = jnp.clip(qc - NA_KW // 2, 0, GRID_W - NA_KW)
    col_ok = (kc >= cs) & (kc < cs + NA_KW)
    lo = lane < GRID_W
    base = h * (n_dr * n_dc)
    per_dr = []
    for dr in range(n_dr):
        m = jnp.zeros((GRID_W, LANES), F32)
        for d in range(n_dc):
            m = jnp.where(dc == d, rpb_ref[base + dr * n_dc + d] * LOG2E, m)
        per_dr.append(m)
    neg = jnp.full((GRID_W, LANES), NEG, F32)
    for p, (r0, ks) in enumerate(patterns):
        q_rows = [r0 + qr for qr in range(NA_ROW_BLOCK)]
        rs = [min(max(q - NA_KH // 2, 0), rows - NA_KH) for q in q_rows]
        for kr in range(B_KROWS):
            k_row = ks + kr
            ok = [r <= k_row < r + NA_KH for r in rs]
            drs = [min(max(k_row - q + NA_KH - 1, 0), n_dr - 1) for q in q_rows]
            if not any(ok):
                piece = neg
            else:
                vals = jnp.where(lo, per_dr[drs[0]], per_dr[drs[1]])
                mask = col_ok
                if not ok[1]:
                    mask = mask & lo
                if not ok[0]:
                    mask = mask & jnp.logical_not(lo)
                piece = jnp.where(mask, vals, neg)
            o_ref[p, 0, kr * GRID_W:(kr + 1) * GRID_W, :] = piece


def _bias_call(rpb, rows):
    patterns = _b_patterns(rows)
    heads = rpb.shape[0]
    return pl.pallas_call(
        functools.partial(_bias_kernel, patterns=patterns, rows=rows),
        grid=(heads,),
        in_specs=[pl.BlockSpec(memory_space=pltpu.SMEM)],
        out_specs=pl.BlockSpec((N_PATTERNS, 1, B_SPAN, QBLK), lambda h: (0, h // 2, 0, h % 2)),
        out_shape=jax.ShapeDtypeStruct((N_PATTERNS, heads // 2, B_SPAN, 2 * QBLK), F32),
        name="na_bias",
    )(rpb.reshape(-1))


def _group_rms(p, ones_ref):
    sq = (p * p).astype(BF16)
    n = p.shape[1]
    if n <= MXU_TILE:
        ssq = _dot(sq, ones_ref[:n, :n])
    else:
        ssq = jnp.concatenate(
            [_dot(sq[:, c:c + MXU_TILE], ones_ref[...]) for c in range(0, n, MXU_TILE)], axis=1)
    return lax.rsqrt(ssq * (1.0 / HEAD_DIM) + EPS)


def _rope(x, cos, sin):
    lane = lax.broadcasted_iota(jnp.int32, (x.shape[0], LANES), 1)
    first = (lane & (HEAD_DIM // 2 - 1)) < HEAD_DIM // 4
    outs = []
    for c in range(0, x.shape[1], LANES):
        xt = x[:, c:c + LANES]
        sw = jnp.where(first, pltpu.roll(xt, LANES - HEAD_DIM // 4, 1), pltpu.roll(xt, HEAD_DIM // 4, 1))
        outs.append(xt * cos + sw * sin)
    return outs[0] if len(outs) == 1 else jnp.concatenate(outs, axis=1)


def _inproj_kernel(x_ref, sh_ref, sc_ref, g_ref, w_ref, gain_ref, ones_ref, wva_ref, wvb_ref, *rest,
                   groups, use_rope):
    if use_rope:
        cos, sin = rest[0][...], rest[1][...]
        rest = rest[2:]
    n_std = sum(len(subs) for _, _, subs in groups)
    out_refs, vt_refs = rest[:n_std], rest[n_std:]
    geff = g_ref[...] * (1.0 + sc_ref[0])
    shift = sh_ref[0]
    tm = x_ref.shape[1]
    sub = min(tm, INPROJ_SUB_ROWS)
    h_cache = {}

    def h_of(r0):
        if r0 not in h_cache:
            x = x_ref[0, r0:r0 + sub, :]
            ms = jnp.mean(x * x, axis=-1, keepdims=True)
            h_cache[r0] = ((x * lax.rsqrt(ms + EPS)) * geff + shift).astype(BF16)
        return h_cache[r0]

    def project(r0, c0, width, subs):
        return _dot(h_of(r0), w_ref[:, c0:c0 + width])

    def finish(p, r0, c0, width, subs):
        for s0, sw, rope, oi in subs:
            y = p[:, s0:s0 + sw]
            y = y * _group_rms(y, ones_ref) * gain_ref[:, c0 + s0:c0 + s0 + sw]
            if rope:
                y = _rope(y, cos[r0:r0 + sub], sin[r0:r0 + sub])
            out_refs[oi][0, r0:r0 + sub, :] = y.astype(BF16)

    def project_vt(r0, wv_ref, vt_ref):
        return _dot_nt(wv_ref[...], h_of(r0))

    def finish_vt(vt, r0, wv_ref, vt_ref):
        for j in range(sub // LANES):
            vt_ref[0, r0 // LANES + j] = vt[:, j * LANES:(j + 1) * LANES].astype(BF16)

    stages = []
    for r0 in range(0, tm, sub):
        stages += [(project, finish, (r0,) + g) for g in groups]
        stages += [(project_vt, finish_vt, (r0,) + a) for a in zip((wva_ref, wvb_ref), vt_refs)]
    pending = None
    for first, second, args in stages:
        res = first(*args)
        if pending is not None:
            pending[0](pending[1], *pending[2])
        pending = (second, res, args)
    pending[0](pending[1], *pending[2])


def _inproj_call(x, mod3, mod_batched, g, w, gain, ones, wva_t, wvb_t, rope_tabs, groups, out_widths, tm):
    b, s, d = x.shape
    n = w.shape[1]
    use_rope = rope_tabs is not None
    mod_idx = (lambda bi, i, j: (bi, 0, j)) if mod_batched else (lambda bi, i, j: (0, 0, j))
    const = lambda arr: pl.BlockSpec(arr.shape, lambda bi, i: (0, 0))
    in_specs = [
        pl.BlockSpec((1, tm, d), lambda bi, i: (bi, i, 0)),
        pl.BlockSpec((1, 1, d), lambda bi, i: mod_idx(bi, i, 0)),
        pl.BlockSpec((1, 1, d), lambda bi, i: mod_idx(bi, i, 1)),
        const(g), const(w), const(gain), const(ones), const(wva_t), const(wvb_t),
    ]
    args = [x, mod3, mod3, g, w, gain, ones, wva_t, wvb_t]
    if use_rope:
        in_specs += [pl.BlockSpec((tm, LANES), lambda bi, i: (i, 0))] * 2
        args += list(rope_tabs)
    vt_dims = [wva_t.shape[0], wvb_t.shape[0]]
    return pl.pallas_call(
        functools.partial(_inproj_kernel, groups=groups, use_rope=use_rope),
        grid=(b, s // tm),
        in_specs=in_specs,
        out_specs=([pl.BlockSpec((1, tm, wd), lambda bi, i: (bi, i, 0)) for wd in out_widths]
                   + [pl.BlockSpec((1, tm // LANES, vd, LANES), lambda bi, i: (bi, i, 0, 0)) for vd in vt_dims]),
        out_shape=([jax.ShapeDtypeStruct((b, s, wd), BF16) for wd in out_widths]
                   + [jax.ShapeDtypeStruct((b, s // LANES, vd, LANES), BF16) for vd in vt_dims]),
        compiler_params=pltpu.CompilerParams(dimension_semantics=("parallel", "arbitrary"),
                                             vmem_limit_bytes=VMEM_LIMIT),
        name="in_proj_rope" if use_rope else "in_proj_ctx",
    )(*args)


def _scores_t(qs, k_loc, k_ctx, add_loc):
    return _dot_nt(k_loc, qs) + add_loc, _dot_nt(k_ctx, qs)


def _softmax_t(s_loc, s_ctx, extra_logit):
    m = jnp.maximum(jnp.max(s_loc, axis=0, keepdims=True), jnp.max(s_ctx, axis=0, keepdims=True))
    if extra_logit is not None:
        m = jnp.maximum(m, extra_logit)
    return jnp.exp2((s_loc - m).astype(BF16)), jnp.exp2((s_ctx - m).astype(BF16)), m


def _pv_t(p_loc, p_ctx, m, vt_loc, vt_ctx, extra_logit):
    n_v = vt_loc.shape[0]
    with_ones = lambda vt: jnp.concatenate([vt, jnp.ones((BF16_ROWS, vt.shape[1]), BF16)], axis=0)
    acc = _dot(with_ones(vt_loc), p_loc) + _dot(with_ones(vt_ctx), p_ctx)
    l = acc[n_v:n_v + 1]
    if extra_logit is not None:
        l = l + jnp.exp2(extra_logit - m)
    return acc[:n_v] * (1.0 / l)


def _group_norm_t(o_t, g_t):
    ms = jnp.mean(o_t * o_t, axis=0, keepdims=True)
    return ((o_t * lax.rsqrt(ms + EPS)) * g_t).T.astype(BF16)


def _lane_tiles(ref, first, n, rows=slice(None)):
    return jnp.concatenate([ref[0, first + j, rows, :] for j in range(n)], axis=1)


def _attn_kernel(sink_ref, qa_ref, qb_ref, ka_ref, vat_ref, kb_ref, vbt_ref, kac_ref, vact_ref,
                 kbc_ref, vbct_ref, ga_ref, gb_ref, bias_ref, o_ref, *, seq, n_sub):
    groups = []
    for j in range(n_sub):
        groups += _attn_block_groups(
            pl.program_id(1) * n_sub + j, slice(j * QBLK, (j + 1) * QBLK), sink_ref, qa_ref, qb_ref, ka_ref,
            vat_ref, kb_ref, vbt_ref, kac_ref, vact_ref, kbc_ref, vbct_ref, ga_ref, gb_ref, bias_ref, o_ref, seq)
    scored, soft = [], []
    for scores, finish in groups:
        scored.append(finish(*scores()))
        if len(scored) > ATTN_LOOKAHEAD:
            gen = scored.pop(0)
            next(gen)
            soft.append(gen)
        if len(soft) > ATTN_PV_LAG:
            next(soft.pop(0), None)
    for gen in scored:
        next(gen)
        soft.append(gen)
    for gen in soft:
        next(gen, None)


def _attn_block_groups(i, qrows, sink_ref, qa_ref, qb_ref, ka_ref, vat_ref, kb_ref, vbt_ref, kac_ref, vact_ref,
                       kbc_ref, vbct_ref, ga_ref, gb_ref, bias_ref, o_ref, seq):
    rows = seq // GRID_W
    stages = []
    n_ctx_tiles = kac_ref.shape[1] // LANES
    lane = lax.broadcasted_iota(jnp.int32, (QBLK, LANES), 1)
    lo = lane < HEAD_DIM
    half_masks = (lo.astype(F32).astype(BF16), jnp.logical_not(lo).astype(F32).astype(BF16))

    group = A_HEADS // A_KV_HEADS
    blk_a = jnp.clip(i - 1, 0, seq // QBLK - A_SPAN // QBLK)
    start_a = pl.multiple_of(blk_a * QBLK, QBLK)
    d = (lax.broadcasted_iota(jnp.int32, (A_SPAN, QBLK), 0)
         - lax.broadcasted_iota(jnp.int32, (A_SPAN, QBLK), 1) + (start_a - i * QBLK))
    amask = jnp.where((d <= A_WINDOW) & (d >= -A_WINDOW), 0.0, NEG).astype(F32)
    amask = jnp.concatenate([amask] * A_STACK, axis=1)
    k_loc = ka_ref[0, pl.ds(start_a, A_SPAN), :]
    k_ctx = kac_ref[0]
    q_tiles = [qa_ref[0, qrows, t * LANES:(t + 1) * LANES] for t in range(group)]
    heads_a = []

    def scores_a(kv, t0):
        qs = jnp.concatenate([q * half_masks[kv] for q in q_tiles[t0:t0 + A_STACK]], axis=0)
        return _scores_t(qs, k_loc, k_ctx, amask)

    def finish_a(kv, t0, s_loc, s_ctx):
        dims = slice(kv * HEAD_DIM, (kv + 1) * HEAD_DIM)
        sink = jnp.concatenate(
            [jnp.full((1, QBLK), sink_ref[kv * group + t0 + t] * LOG2E, F32) for t in range(A_STACK)], axis=1)
        p_loc, p_ctx, m = _softmax_t(s_loc, s_ctx, sink)
        yield
        o_t = _pv_t(p_loc, p_ctx, m, _lane_tiles(vat_ref, blk_a, A_SPAN // LANES, dims),
                    _lane_tiles(vact_ref, 0, n_ctx_tiles, dims), sink)
        heads_a.extend(o_t[:, t * QBLK:(t + 1) * QBLK] for t in range(A_STACK))
        if len(heads_a) == A_HEADS:
            o_ref[0, qrows, :A_WIDTH] = _group_norm_t(jnp.concatenate(heads_a, axis=0), ga_ref[...])

    stages += [(functools.partial(scores_a, kv, t0), functools.partial(finish_a, kv, t0))
               for kv in range(A_KV_HEADS) for t0 in range(0, group, A_STACK)]

    blk_b = jnp.clip(NA_ROW_BLOCK * i - NA_KH // 2, 0, rows - B_KROWS) // NA_ROW_BLOCK
    start_b = pl.multiple_of(blk_b * QBLK, QBLK)
    heads_b = []
    n_blk = seq // QBLK
    pattern = jnp.where(i < 2, i, jnp.where(i >= n_blk - 2, i - (n_blk - N_PATTERNS), 2))

    def scores_b(pair):
        cols = slice(pair * LANES, (pair + 1) * LANES)
        qp = qb_ref[0, qrows, cols]
        qs = jnp.concatenate([qp * half_masks[0], qp * half_masks[1]], axis=0)
        return _scores_t(qs, kb_ref[0, pl.ds(start_b, B_SPAN), cols], kbc_ref[0, :, cols], bias_ref[pattern, pair])

    def finish_b(pair, s_loc, s_ctx):
        cols = slice(pair * LANES, (pair + 1) * LANES)
        p_loc, p_ctx, m = _softmax_t(s_loc, s_ctx, None)
        yield
        o_t = _pv_t(p_loc, p_ctx, m, _lane_tiles(vbt_ref, blk_b, B_SPAN // LANES, cols),
                    _lane_tiles(vbct_ref, 0, n_ctx_tiles, cols), None)
        heads_b.extend([o_t[:HEAD_DIM, :QBLK], o_t[HEAD_DIM:, QBLK:]])
        if pair == B_HEADS // 2 - 1:
            o_ref[0, qrows, A_WIDTH:] = _group_norm_t(jnp.concatenate(heads_b, axis=0), gb_ref[...])

    stages += [(functools.partial(scores_b, p), functools.partial(finish_b, p)) for p in range(B_HEADS // 2)]
    return stages


def _attn_call(sink, qa, qb, ka, vat, kb, vbt, kac, vact, kbc, vbct, bias, ga_t, gb_t):
    b, s, _ = qa.shape
    n_blk = s // QBLK
    assert (NA_KH // 2) % NA_ROW_BLOCK == 0 and (s // GRID_W - B_KROWS) % NA_ROW_BLOCK == 0

    n_sub = ATTN_BLOCKS_PER_STEP
    full = lambda arr: pl.BlockSpec((1,) + arr.shape[1:], lambda bi, i: (bi,) + (0,) * (arr.ndim - 1))
    const = lambda arr, **kw: pl.BlockSpec(arr.shape, lambda bi, i: (0,) * arr.ndim, **kw)
    blk = lambda wd: pl.BlockSpec((1, n_sub * QBLK, wd), lambda bi, i: (bi, i, 0))
    return pl.pallas_call(
        functools.partial(_attn_kernel, seq=s, n_sub=n_sub),
        grid=(b, n_blk // n_sub),
        in_specs=[
            pl.BlockSpec(memory_space=pltpu.SMEM),
            blk(A_WIDTH), blk(B_WIDTH),
            full(ka), full(vat), full(kb), full(vbt), full(kac), full(vact), full(kbc), full(vbct),
            const(ga_t), const(gb_t), const(bias, pipeline_mode=pl.Buffered(1)),
        ],
        out_specs=blk(A_WIDTH + B_WIDTH),
        out_shape=jax.ShapeDtypeStruct((b, s, A_WIDTH + B_WIDTH), BF16),
        compiler_params=pltpu.CompilerParams(dimension_semantics=("parallel", "arbitrary"),
                                             vmem_limit_bytes=VMEM_LIMIT),
        name="attn",
    )(sink, qa, qb, ka, vat, kb, vbt, kac, vact, kbc, vbct, ga_t, gb_t, bias)


def _ffn_kernel(x_ref, o_ref, g1_ref, sh2_ref, sc2_ref, g2_ref, n2_ref, wo_ref, wg_ref, wu_ref, wd_ref,
                out_ref):
    geff = n2_ref[...] * (1.0 + sc2_ref[0])
    tm = x_ref.shape[1]
    sub = min(tm, FFN_SUB_ROWS)

    def stage_rows(r0):
        rows = slice(r0, r0 + sub)
        y = _dot(o_ref[0, rows, :], wo_ref[...])
        yield
        x1 = x_ref[0, rows, :] + g1_ref[0] * y
        ms = jnp.mean(x1 * x1, axis=-1, keepdims=True)
        h2 = ((x1 * lax.rsqrt(ms + EPS)) * geff + sh2_ref[0]).astype(BF16)
        gate = _dot(h2, wg_ref[...])
        up = _dot(h2, wu_ref[...])
        yield
        act = (_silu(gate) * up).astype(BF16)
        down = _dot(act, wd_ref[...])
        yield
        out_ref[0, rows, :] = x1 + g2_ref[0] * down

    live = [stage_rows(r0) for r0 in range(0, tm, sub)]
    while live:
        for gen in list(live):
            if next(gen, StopIteration) is StopIteration:
                live.remove(gen)


def _ffn_call(x, o, mod3, n2, wo, wg, wu, wd, tm):
    b, s, d = x.shape
    hid = wg.shape[1]
    modspec = lambda j: pl.BlockSpec((1, 1, d), lambda bi, i: (bi, 0, j))
    const = lambda shape: pl.BlockSpec(shape, lambda bi, i: (0, 0), pipeline_mode=pl.Buffered(1))
    return pl.pallas_call(
        _ffn_kernel,
        grid=(b, s // tm),
        in_specs=[
            pl.BlockSpec((1, tm, d), lambda bi, i: (bi, i, 0)),
            pl.BlockSpec((1, tm, o.shape[2]), lambda bi, i: (bi, i, 0)),
            modspec(2), modspec(3), modspec(4), modspec(5),
            pl.BlockSpec((1, d), lambda bi, i: (0, 0)),
            const((o.shape[2], d)), const((d, hid)), const((d, hid)), const((hid, d)),
        ],
        out_specs=pl.BlockSpec((1, tm, d), lambda bi, i: (bi, i, 0)),
        out_shape=jax.ShapeDtypeStruct((b, s, d), F32),
        compiler_params=pltpu.CompilerParams(dimension_semantics=("parallel", "arbitrary"),
                                             vmem_limit_bytes=VMEM_LIMIT),
        name="out_ffn",
    )(x, o, mod3, mod3, mod3, mod3, n2, wo, wg, wu, wd)


def _layer(x, ctx, mod, norm1_g, w_in, qn_a, kn_a, sink_a, qn_b, kn_b, rpb_b, on_a, on_b, w_out,
           norm2_g, w_gate, w_up, w_down, rope_tabs):
    b, s, d = x.shape
    rows = s // GRID_W
    w_l, w_c, wva_t, wvb_t = _wprep_call(w_in)
    scale = HEAD_DIM ** -0.5 * LOG2E
    gain_l = jnp.concatenate([jnp.tile(qn_a, A_HEADS) * scale, jnp.tile(kn_a, A_KV_HEADS),
                              jnp.tile(qn_b, B_HEADS) * scale, jnp.tile(kn_b, B_HEADS)])[None]
    gain_c = jnp.concatenate([jnp.tile(kn_a, A_KV_HEADS), jnp.tile(kn_b, B_HEADS)])[None]
    ones_bd = jnp.asarray(np.kron(np.eye(MXU_TILE // HEAD_DIM), np.ones((HEAD_DIM, HEAD_DIM))), BF16)

    mod3 = mod[:b].reshape(b, 1, -1)
    mod3_c = mod[b:b + 1].reshape(1, 1, -1)
    g1 = norm1_g[None]

    qka = A_WIDTH + A_KV_WIDTH
    groups_l = [
        (0, qka, [(0, A_WIDTH, True, 0), (A_WIDTH, A_KV_WIDTH, True, 1)]),
        (qka, B_WIDTH, [(0, B_WIDTH, False, 2)]),
        (qka + B_WIDTH, B_WIDTH, [(0, B_WIDTH, False, 3)]),
    ]
    qa, ka, qb, kb, vat, vbt = _inproj_call(
        x, mod3, True, g1, w_l, gain_l, ones_bd, wva_t, wvb_t, rope_tabs, groups_l,
        [A_WIDTH, A_KV_WIDTH, B_WIDTH, B_WIDTH], tm=INPROJ_ROWS)
    groups_c = [(0, A_KV_WIDTH + B_WIDTH, [(0, A_KV_WIDTH, False, 0), (A_KV_WIDTH, B_WIDTH, False, 1)])]
    n_ctx = ctx.shape[1]
    ctx_out = _inproj_call(
        ctx.reshape(1, b * n_ctx, d), mod3_c, False, g1, w_c, gain_c, ones_bd, wva_t, wvb_t, None, groups_c,
        [A_KV_WIDTH, B_WIDTH], tm=CTX_ROWS)
    kac, kbc = (a.reshape(b, n_ctx, -1) for a in ctx_out[:2])
    vact, vbct = (a.reshape(b, n_ctx // LANES, -1, LANES) for a in ctx_out[2:])

    bias = _bias_call(rpb_b, rows)
    lanes_of = lambda g: jnp.broadcast_to(g[:, None], (g.shape[0], LANES))
    o = _attn_call(sink_a, qa, qb, ka, vat, kb, vbt, kac, vact, kbc, vbct, bias, lanes_of(on_a), lanes_of(on_b))
    return _ffn_call(x, o, mod3, norm2_g[None], w_out.astype(BF16), w_gate.astype(BF16), w_up.astype(BF16),
                     w_down.astype(BF16), tm=FFN_ROWS)


def kernel(x, c, ctx, c_ctx, w_mod, b_mod, norm1_g, w_in, qn_a, kn_a, sink_a, qn_b, kn_b, rpb_b, on_a, on_b,
           w_out, norm2_g, w_gate, w_up, w_down):
    b, s, d = x.shape
    depth = w_mod.shape[0]
    assert depth == 1, "the context stream update is only needed when a later layer reads it"
    rope_tabs = _rope_call(s)
    n_rows = -(-(b + 1) // BF16_ROWS) * BF16_ROWS
    cc = jnp.concatenate([c, c_ctx[None], jnp.zeros((n_rows - b - 1, d), F32)], axis=0)
    l = 0
    mod = _mod_call(cc, w_mod[l], b_mod[l][None])
    return _layer(x, ctx, mod, norm1_g[l], w_in[l], qn_a[l], kn_a[l], sink_a[l], qn_b[l], kn_b[l], rpb_b[l],
                  on_a[l], on_b[l], w_out[l], norm2_g[l], w_gate[l], w_up[l], w_down[l], rope_tabs)
```

```python
import functools

import numpy as np
import jax
import jax.numpy as jnp
from jax import lax
from jax.experimental import pallas as pl
from jax.experimental.pallas import tpu as pltpu

GRID_W = 64
HEAD_DIM = 64
A_HEADS = 8
A_KV_HEADS = 2
A_WINDOW = 128
B_HEADS = 8
NA_KH = 8
NA_KW = 16
NA_ROW_BLOCK = 2
ROPE_BASE = 10000.0
EPS = 1e-6

A_WIDTH = A_HEADS * HEAD_DIM
A_KV_WIDTH = A_KV_HEADS * HEAD_DIM
B_WIDTH = B_HEADS * HEAD_DIM
QBLK = NA_ROW_BLOCK * GRID_W
A_SPAN = QBLK + 2 * A_WINDOW
B_KROWS = 10
B_SPAN = B_KROWS * GRID_W
N_PATTERNS = 5
ATTN_BLOCKS_PER_STEP = 8
MOD_COLS = 2048
INPROJ_ROWS = 2048
CTX_ROWS = 512
INPROJ_SUB_ROWS = 512
FFN_ROWS = 1024
FFN_SUB_ROWS = 256
A_STACK = 4
ATTN_PV_LAG = 1
ATTN_LOOKAHEAD = 1
NEG = -1e30
LOG2E = 1.4426950408889634

LANES = 128
BF16_ROWS = 16
MXU_TILE = 256
VMEM_LIMIT = 56 * 1024 * 1024

F32 = jnp.float32
BF16 = jnp.bfloat16


def _dot(a, b):
    return jnp.dot(a, b, preferred_element_type=F32)


def _dot_nt(a, b):
    return lax.dot_general(a, b, (((1,), (1,)), ((), ())), preferred_element_type=F32)


def _silu(x):
    return x / (1.0 + jnp.exp(-x))


def _mod_kernel(c_ref, w_ref, b_ref, o_ref):
    split = lambda v: (v.astype(BF16), (v - v.astype(BF16).astype(F32)).astype(BF16))
    a_hi, a_lo = split(_silu(c_ref[...]))
    w_hi, w_lo = split(w_ref[...])
    rows = a_hi.shape[0]
    head = _dot(jnp.concatenate([a_hi, a_lo], axis=0), w_hi)
    o_ref[...] = head[:rows] + head[rows:] + _dot(a_hi, w_lo) + b_ref[...]


def _mod_call(cc, w_mod, b_mod):
    rows, d = cc.shape
    n = w_mod.shape[1]
    bn = MOD_COLS
    return pl.pallas_call(
        _mod_kernel,
        grid=(n // bn,),
        in_specs=[pl.BlockSpec((rows, d), lambda j: (0, 0)),
                  pl.BlockSpec((d, bn), lambda j: (0, j)),
                  pl.BlockSpec((1, bn), lambda j: (0, j))],
        out_specs=pl.BlockSpec((rows, bn), lambda j: (0, j)),
        out_shape=jax.ShapeDtypeStruct((rows, n), F32),
        name="mod",
    )(cc, w_mod, b_mod)


def _wprep_kernel(w_ref, wl_ref, wc_ref, wvat_ref, wvbt_ref):
    group = A_HEADS // A_KV_HEADS
    c_ka, c_va, c_qb, c_kb, c_vb = np.cumsum([A_WIDTH, A_KV_WIDTH, A_KV_WIDTH, B_WIDTH, B_WIDTH])
    lane = lax.broadcasted_iota(jnp.int32, (w_ref.shape[0], LANES), 1)
    lo = lane < HEAD_DIM
    for t in range(group):
        src = [w_ref[:, (h // 2) * LANES:(h // 2 + 1) * LANES] for h in (t, t + group)]
        if t % 2 == 0:
            tile = jnp.where(lo, src[0], pltpu.roll(src[1], HEAD_DIM, 1))
        else:
            tile = jnp.where(lo, pltpu.roll(src[0], HEAD_DIM, 1), src[1])
        wl_ref[:, t * LANES:(t + 1) * LANES] = tile.astype(BF16)
    k_a = w_ref[:, c_ka:c_va].astype(BF16)
    k_b = w_ref[:, c_kb:c_vb].astype(BF16)
    wl_ref[:, A_WIDTH:A_WIDTH + A_KV_WIDTH] = k_a
    wl_ref[:, A_WIDTH + A_KV_WIDTH:A_WIDTH + A_KV_WIDTH + B_WIDTH] = w_ref[:, c_qb:c_kb].astype(BF16)
    wl_ref[:, A_WIDTH + A_KV_WIDTH + B_WIDTH:] = k_b
    wc_ref[:, :A_KV_WIDTH] = k_a
    wc_ref[:, A_KV_WIDTH:] = k_b
    wvat_ref[...] = w_ref[:, c_va:c_qb].T.astype(BF16)
    for c in range(0, B_WIDTH, LANES):
        wvbt_ref[c:c + LANES, :] = w_ref[:, c_vb + c:c_vb + c + LANES].T.astype(BF16)


def _wprep_call(w_in):
    d, n = w_in.shape
    assert (A_HEADS // A_KV_HEADS) % 2 == 0 and n == A_WIDTH + 2 * A_KV_WIDTH + 3 * B_WIDTH
    shapes = [(d, A_WIDTH + A_KV_WIDTH + 2 * B_WIDTH), (d, A_KV_WIDTH + B_WIDTH), (A_KV_WIDTH, d), (B_WIDTH, d)]
    return pl.pallas_call(
        _wprep_kernel,
        grid=(1,),
        in_specs=[pl.BlockSpec((d, n), lambda i: (0, 0))],
        out_specs=[pl.BlockSpec(s, lambda i: (0, 0)) for s in shapes],
        out_shape=[jax.ShapeDtypeStruct(s, BF16) for s in shapes],
        compiler_params=pltpu.CompilerParams(vmem_limit_bytes=VMEM_LIMIT),
        name="w_in_layout",
    )(w_in)


def _rope_kernel(invf_ref, cos_ref, sin_ref, *, rows):
    n_pos = max(rows, GRID_W)
    pos = lax.broadcasted_iota(jnp.int32, (n_pos, LANES), 0).astype(F32)
    lane = lax.broadcasted_iota(jnp.int32, (GRID_W, LANES), 1)
    row_lanes = (lane & (HEAD_DIM - 1)) < HEAD_DIM // 2
    first = (lane & (HEAD_DIM // 2 - 1)) < HEAD_DIM // 4
    ang = pos * invf_ref[...]
    cos_p, sin_p = jnp.cos(ang), jnp.sin(ang)
    cos_c = cos_p[:GRID_W]
    sin_c = jnp.where(first, -sin_p[:GRID_W], sin_p[:GRID_W])
    for r in range(rows):
        tok = slice(r * GRID_W, (r + 1) * GRID_W)
        cos_ref[tok, :] = jnp.where(row_lanes, cos_p[r:r + 1], cos_c)
        sin_ref[tok, :] = jnp.where(row_lanes, jnp.where(first, -sin_p[r:r + 1], sin_p[r:r + 1]), sin_c)


def _rope_call(seq):
    quarter = HEAD_DIM // 4
    inv = (1.0 / (np.float32(ROPE_BASE) ** (np.arange(quarter, dtype=np.float32) / quarter))).astype(np.float32)
    invf = jnp.asarray(np.tile(inv, LANES // quarter)[None, :])
    return pl.pallas_call(
        functools.partial(_rope_kernel, rows=seq // GRID_W),
        grid=(1,),
        in_specs=[pl.BlockSpec((1, LANES), lambda i: (0, 0))],
        out_specs=[pl.BlockSpec((seq, LANES), lambda i: (0, 0))] * 2,
        out_shape=[jax.ShapeDtypeStruct((seq, LANES), F32)] * 2,
        name="rope_tables",
    )(invf)


def _b_patterns(rows):
    n_blk = rows // NA_ROW_BLOCK
    blocks = [0, 1, 2, n_blk - 2, n_blk - 1]
    return [(NA_ROW_BLOCK * i, _b_key_start(i, rows)) for i in blocks]


def _b_key_start(i, rows):
    return min(max(NA_ROW_BLOCK * i - NA_KH // 2, 0), rows - B_KROWS)


def _bias_kernel(rpb_ref, o_ref, *, patterns, rows):
    h = pl.program_id(0)
    n_dr, n_dc = 2 * NA_KH - 1, 2 * NA_KW - 1
    kc = lax.broadcasted_iota(jnp.int32, (GRID_W, LANES), 0)
    lane = lax.broadcasted_iota(jnp.int32, (GRID_W, LANES), 1)
    qc = lane & (GRID_W - 1)
    dc = jnp.clip(kc - qc + NA_KW - 1, 0, n_dc - 1)
    cs = jnp.clip(qc - NA_KW // 2, 0, GRID_W - NA_KW)
    col_ok = (kc >= cs) & (kc < cs + NA_KW)
    lo = lane < GRID_W
    base = h * (n_dr * n_dc)
    per_dr = []
    for dr in range(n_dr):
        m = jnp.zeros((GRID_W, LANES), F32)
        for d in range(n_dc):
            m = jnp.where(dc == d, rpb_ref[base + dr * n_dc + d] * LOG2E, m)
        per_dr.append(m)
    neg = jnp.full((GRID_W, LANES), NEG, F32)
    for p, (r0, ks) in enumerate(patterns):
        q_rows = [r0 + qr for qr in range(NA_ROW_BLOCK)]
        rs = [min(max(q - NA_KH // 2, 0), rows - NA_KH) for q in q_rows]
        for kr in range(B_KROWS):
            k_row = ks + kr
            ok = [r <= k_row < r + NA_KH for r in rs]
            drs = [min(max(k_row - q + NA_KH - 1, 0), n_dr - 1) for q in q_rows]
            if not any(ok):
                piece = neg
            else:
                vals = jnp.where(lo, per_dr[drs[0]], per_dr[drs[1]])
                mask = col_ok
                if not ok[1]:
                    mask = mask & lo
                if not ok[0]:
                    mask = mask & jnp.logical_not(lo)
                piece = jnp.where(mask, vals, neg)
            o_ref[p, 0, kr * GRID_W:(kr + 1) * GRID_W, :] = piece


def _bias_call(rpb, rows):
    patterns = _b_patterns(rows)
    heads = rpb.shape[0]
    return pl.pallas_call(
        functools.partial(_bias_kernel, patterns=patterns, rows=rows),
        grid=(heads,),
        in_specs=[pl.BlockSpec(memory_space=pltpu.SMEM)],
        out_specs=pl.BlockSpec((N_PATTERNS, 1, B_SPAN, QBLK), lambda h: (0, h // 2, 0, h % 2)),
        out_shape=jax.ShapeDtypeStruct((N_PATTERNS, heads // 2, B_SPAN, 2 * QBLK), F32),
        name="na_bias",
    )(rpb.reshape(-1))


def _group_rms(p, ones_ref):
    sq = (p * p).astype(BF16)
    n = p.shape[1]
    if n <= MXU_TILE:
        ssq = _dot(sq, ones_ref[:n, :n])
    else:
        ssq = jnp.concatenate(
            [_dot(sq[:, c:c + MXU_TILE], ones_ref[...]) for c in range(0, n, MXU_TILE)], axis=1)
    return lax.rsqrt(ssq * (1.0 / HEAD_DIM) + EPS)


def _rope(x, cos, sin):
    lane = lax.broadcasted_iota(jnp.int32, (x.shape[0], LANES), 1)
    first = (lane & (HEAD_DIM // 2 - 1)) < HEAD_DIM // 4
    outs = []
    for c in range(0, x.shape[1], LANES):
        xt = x[:, c:c + LANES]
        sw = jnp.where(first, pltpu.roll(xt, LANES - HEAD_DIM // 4, 1), pltpu.roll(xt, HEAD_DIM // 4, 1))
        outs.append(xt * cos + sw * sin)
    return outs[0] if len(outs) == 1 else jnp.concatenate(outs, axis=1)


def _inproj_kernel(x_ref, sh_ref, sc_ref, g_ref, w_ref, gain_ref, ones_ref, wva_ref, wvb_ref, *rest,
                   groups, use_rope):
    if use_rope:
        cos, sin = rest[0][...], rest[1][...]
        rest = rest[2:]
    n_std = sum(len(subs) for _, _, subs in groups)
    out_refs, vt_refs = rest[:n_std], rest[n_std:]
    geff = g_ref[...] * (1.0 + sc_ref[0])
    shift = sh_ref[0]
    tm = x_ref.shape[1]
    sub = min(tm, INPROJ_SUB_ROWS)
    h_cache = {}

    def h_of(r0):
        if r0 not in h_cache:
            x = x_ref[0, r0:r0 + sub, :]
            ms = jnp.mean(x * x, axis=-1, keepdims=True)
            h_cache[r0] = ((x * lax.rsqrt(ms + EPS)) * geff + shift).astype(BF16)
        return h_cache[r0]

    def project(r0, c0, width, subs):
        return _dot(h_of(r0), w_ref[:, c0:c0 + width])

    def finish(p, r0, c0, width, subs):
        for s0, sw, rope, oi in subs:
            y = p[:, s0:s0 + sw]
            y = y * _group_rms(y, ones_ref) * gain_ref[:, c0 + s0:c0 + s0 + sw]
            if rope:
                y = _rope(y, cos[r0:r0 + sub], sin[r0:r0 + sub])
            out_refs[oi][0, r0:r0 + sub, :] = y.astype(BF16)

    def project_vt(r0, wv_ref, vt_ref):
        return _dot_nt(wv_ref[...], h_of(r0))

    def finish_vt(vt, r0, wv_ref, vt_ref):
        for j in range(sub // LANES):
            vt_ref[0, r0 // LANES + j] = vt[:, j * LANES:(j + 1) * LANES].astype(BF16)

    stages = []
    for r0 in range(0, tm, sub):
        stages += [(project, finish, (r0,) + g) for g in groups]
        stages += [(project_vt, finish_vt, (r0,) + a) for a in zip((wva_ref, wvb_ref), vt_refs)]
    pending = None
    for first, second, args in stages:
        res = first(*args)
        if pending is not None:
            pending[0](pending[1], *pending[2])
        pending = (second, res, args)
    pending[0](pending[1], *pending[2])


def _inproj_call(x, mod3, mod_batched, g, w, gain, ones, wva_t, wvb_t, rope_tabs, groups, out_widths, tm):
    b, s, d = x.shape
    n = w.shape[1]
    use_rope = rope_tabs is not None
    mod_idx = (lambda bi, i, j: (bi, 0, j)) if mod_batched else (lambda bi, i, j: (0, 0, j))
    const = lambda arr: pl.BlockSpec(arr.shape, lambda bi, i: (0, 0))
    in_specs = [
        pl.BlockSpec((1, tm, d), lambda bi, i: (bi, i, 0)),
        pl.BlockSpec((1, 1, d), lambda bi, i: mod_idx(bi, i, 0)),
        pl.BlockSpec((1, 1, d), lambda bi, i: mod_idx(bi, i, 1)),
        const(g), const(w), const(gain), const(ones), const(wva_t), const(wvb_t),
    ]
    args = [x, mod3, mod3, g, w, gain, ones, wva_t, wvb_t]
    if use_rope:
        in_specs += [pl.BlockSpec((tm, LANES), lambda bi, i: (i, 0))] * 2
        args += list(rope_tabs)
    vt_dims = [wva_t.shape[0], wvb_t.shape[0]]
    return pl.pallas_call(
        functools.partial(_inproj_kernel, groups=groups, use_rope=use_rope),
        grid=(b, s // tm),
        in_specs=in_specs,
        out_specs=([pl.BlockSpec((1, tm, wd), lambda bi, i: (bi, i, 0)) for wd in out_widths]
                   + [pl.BlockSpec((1, tm // LANES, vd, LANES), lambda bi, i: (bi, i, 0, 0)) for vd in vt_dims]),
        out_shape=([jax.ShapeDtypeStruct((b, s, wd), BF16) for wd in out_widths]
                   + [jax.ShapeDtypeStruct((b, s // LANES, vd, LANES), BF16) for vd in vt_dims]),
        compiler_params=pltpu.CompilerParams(dimension_semantics=("parallel", "arbitrary"),
                                             vmem_limit_bytes=VMEM_LIMIT),
        name="in_proj_rope" if use_rope else "in_proj_ctx",
    )(*args)


def _scores_t(qs, k_loc, k_ctx, add_loc):
    return _dot_nt(k_loc, qs) + add_loc, _dot_nt(k_ctx, qs)


def _softmax_t(s_loc, s_ctx, extra_logit):
    m = jnp.maximum(jnp.max(s_loc, axis=0, keepdims=True), jnp.max(s_ctx, axis=0, keepdims=True))
    if extra_logit is not None:
        m = jnp.maximum(m, extra_logit)
    return jnp.exp2((s_loc - m).astype(BF16)), jnp.exp2((s_ctx - m).astype(BF16)), m


def _pv_t(p_loc, p_ctx, m, vt_loc, vt_ctx, extra_logit):
    n_v = vt_loc.shape[0]
    with_ones = lambda vt: jnp.concatenate([vt, jnp.ones((BF16_ROWS, vt.shape[1]), BF16)], axis=0)
    acc = _dot(with_ones(vt_loc), p_loc) + _dot(with_ones(vt_ctx), p_ctx)
    l = acc[n_v:n_v + 1]
    if extra_logit is not None:
        l = l + jnp.exp2(extra_logit - m)
    return acc[:n_v] * (1.0 / l)


def _group_norm_t(o_t, g_t):
    ms = jnp.mean(o_t * o_t, axis=0, keepdims=True)
    return ((o_t * lax.rsqrt(ms + EPS)) * g_t).T.astype(BF16)


def _lane_tiles(ref, first, n, rows=slice(None)):
    return jnp.concatenate([ref[0, first + j, rows, :] for j in range(n)], axis=1)


def _attn_kernel(sink_ref, qa_ref, qb_ref, ka_ref, vat_ref, kb_ref, vbt_ref, kac_ref, vact_ref,
                 kbc_ref, vbct_ref, ga_ref, gb_ref, bias_ref, *rest, seq, n_sub, n_cast):
    o_ref = rest[n_cast]
    for src_ref, dst_ref in zip(rest[:n_cast], rest[n_cast + 1:]):
        dst_ref[...] = src_ref[...].astype(BF16)
    groups = []
    for j in range(n_sub):
        groups += _attn_block_groups(
            pl.program_id(1) * n_sub + j, slice(j * QBLK, (j + 1) * QBLK), sink_ref, qa_ref, qb_ref, ka_ref,
            vat_ref, kb_ref, vbt_ref, kac_ref, vact_ref, kbc_ref, vbct_ref, ga_ref, gb_ref, bias_ref, o_ref, seq)
    scored, soft = [], []
    for scores, finish in groups:
        scored.append(finish(*scores()))
        if len(scored) > ATTN_LOOKAHEAD:
            gen = scored.pop(0)
            next(gen)
            soft.append(gen)
        if len(soft) > ATTN_PV_LAG:
            next(soft.pop(0), None)
    for gen in scored:
        next(gen)
        soft.append(gen)
    for gen in soft:
        next(gen, None)


def _attn_block_groups(i, qrows, sink_ref, qa_ref, qb_ref, ka_ref, vat_ref, kb_ref, vbt_ref, kac_ref, vact_ref,
                       kbc_ref, vbct_ref, ga_ref, gb_ref, bias_ref, o_ref, seq):
    rows = seq // GRID_W
    stages = []
    n_ctx_tiles = kac_ref.shape[1] // LANES
    lane = lax.broadcasted_iota(jnp.int32, (QBLK, LANES), 1)
    lo = lane < HEAD_DIM
    half_masks = (lo.astype(F32).astype(BF16), jnp.logical_not(lo).astype(F32).astype(BF16))

    group = A_HEADS // A_KV_HEADS
    blk_a = jnp.clip(i - 1, 0, seq // QBLK - A_SPAN // QBLK)
    start_a = pl.multiple_of(blk_a * QBLK, QBLK)
    d = (lax.broadcasted_iota(jnp.int32, (A_SPAN, QBLK), 0)
         - lax.broadcasted_iota(jnp.int32, (A_SPAN, QBLK), 1) + (start_a - i * QBLK))
    amask = jnp.where((d <= A_WINDOW) & (d >= -A_WINDOW), 0.0, NEG).astype(F32)
    amask = jnp.concatenate([amask] * A_STACK, axis=1)
    k_loc = ka_ref[0, pl.ds(start_a, A_SPAN), :]
    k_ctx = kac_ref[0]
    q_tiles = [qa_ref[0, qrows, t * LANES:(t + 1) * LANES] for t in range(group)]
    heads_a = []

    def scores_a(kv, t0):
        qs = jnp.concatenate([q * half_masks[kv] for q in q_tiles[t0:t0 + A_STACK]], axis=0)
        return _scores_t(qs, k_loc, k_ctx, amask)

    def finish_a(kv, t0, s_loc, s_ctx):
        dims = slice(kv * HEAD_DIM, (kv + 1) * HEAD_DIM)
        sink = jnp.concatenate(
            [jnp.full((1, QBLK), sink_ref[kv * group + t0 + t] * LOG2E, F32) for t in range(A_STACK)], axis=1)
        p_loc, p_ctx, m = _softmax_t(s_loc, s_ctx, sink)
        yield
        o_t = _pv_t(p_loc, p_ctx, m, _lane_tiles(vat_ref, blk_a, A_SPAN // LANES, dims),
                    _lane_tiles(vact_ref, 0, n_ctx_tiles, dims), sink)
        heads_a.extend(o_t[:, t * QBLK:(t + 1) * QBLK] for t in range(A_STACK))
        if len(heads_a) == A_HEADS:
            o_ref[0, qrows, :A_WIDTH] = _group_norm_t(jnp.concatenate(heads_a, axis=0), ga_ref[...])

    stages += [(functools.partial(scores_a, kv, t0), functools.partial(finish_a, kv, t0))
               for kv in range(A_KV_HEADS) for t0 in range(0, group, A_STACK)]

    blk_b = jnp.clip(NA_ROW_BLOCK * i - NA_KH // 2, 0, rows - B_KROWS) // NA_ROW_BLOCK
    start_b = pl.multiple_of(blk_b * QBLK, QBLK)
    heads_b = []
    n_blk = seq // QBLK
    pattern = jnp.where(i < 2, i, jnp.where(i >= n_blk - 2, i - (n_blk - N_PATTERNS), 2))

    def scores_b(pair):
        cols = slice(pair * LANES, (pair + 1) * LANES)
        qp = qb_ref[0, qrows, cols]
        qs = jnp.concatenate([qp * half_masks[0], qp * half_masks[1]], axis=0)
        return _scores_t(qs, kb_ref[0, pl.ds(start_b, B_SPAN), cols], kbc_ref[0, :, cols], bias_ref[pattern, pair])

    def finish_b(pair, s_loc, s_ctx):
        cols = slice(pair * LANES, (pair + 1) * LANES)
        p_loc, p_ctx, m = _softmax_t(s_loc, s_ctx, None)
        yield
        o_t = _pv_t(p_loc, p_ctx, m, _lane_tiles(vbt_ref, blk_b, B_SPAN // LANES, cols),
                    _lane_tiles(vbct_ref, 0, n_ctx_tiles, cols), None)
        heads_b.extend([o_t[:HEAD_DIM, :QBLK], o_t[HEAD_DIM:, QBLK:]])
        if pair == B_HEADS // 2 - 1:
            o_ref[0, qrows, A_WIDTH:] = _group_norm_t(jnp.concatenate(heads_b, axis=0), gb_ref[...])

    stages += [(functools.partial(scores_b, p), functools.partial(finish_b, p)) for p in range(B_HEADS // 2)]
    return stages


def _attn_call(sink, qa, qb, ka, vat, kb, vbt, kac, vact, kbc, vbct, bias, ga_t, gb_t, cast_weights):
    b, s, _ = qa.shape
    n_blk = s // QBLK
    assert (NA_KH // 2) % NA_ROW_BLOCK == 0 and (s // GRID_W - B_KROWS) % NA_ROW_BLOCK == 0

    n_sub = ATTN_BLOCKS_PER_STEP
    steps_per_batch = n_blk // n_sub
    full = lambda arr: pl.BlockSpec((1,) + arr.shape[1:], lambda bi, i: (bi,) + (0,) * (arr.ndim - 1))
    const = lambda arr, **kw: pl.BlockSpec(arr.shape, lambda bi, i: (0,) * arr.ndim, **kw)
    blk = lambda wd: pl.BlockSpec((1, n_sub * QBLK, wd), lambda bi, i: (bi, i, 0))
    flat = [w.reshape(-1, LANES) for w in cast_weights]
    slab = lambda w: pl.BlockSpec((w.shape[0] // (b * steps_per_batch), LANES),
                                  lambda bi, i: (bi * steps_per_batch + i, 0))
    assert all(w.shape[0] % (b * steps_per_batch * BF16_ROWS) == 0 for w in flat)
    outs = pl.pallas_call(
        functools.partial(_attn_kernel, seq=s, n_sub=n_sub, n_cast=len(flat)),
        grid=(b, steps_per_batch),
        in_specs=[
            pl.BlockSpec(memory_space=pltpu.SMEM),
            blk(A_WIDTH), blk(B_WIDTH),
            full(ka), full(vat), full(kb), full(vbt), full(kac), full(vact), full(kbc), full(vbct),
            const(ga_t), const(gb_t), const(bias, pipeline_mode=pl.Buffered(1)),
        ] + [slab(w) for w in flat],
        out_specs=[blk(A_WIDTH + B_WIDTH)] + [slab(w) for w in flat],
        out_shape=([jax.ShapeDtypeStruct((b, s, A_WIDTH + B_WIDTH), BF16)]
                   + [jax.ShapeDtypeStruct(w.shape, BF16) for w in flat]),
        compiler_params=pltpu.CompilerParams(dimension_semantics=("parallel", "arbitrary"),
                                             vmem_limit_bytes=VMEM_LIMIT),
        name="attn",
    )(sink, qa, qb, ka, vat, kb, vbt, kac, vact, kbc, vbct, ga_t, gb_t, bias, *flat)
    return outs[0], [w16.reshape(w.shape) for w16, w in zip(outs[1:], cast_weights)]


def _ffn_kernel(x_ref, o_ref, g1_ref, sh2_ref, sc2_ref, g2_ref, n2_ref, wo_ref, wg_ref, wu_ref, wd_ref,
                out_ref):
    geff = n2_ref[...] * (1.0 + sc2_ref[0])
    tm = x_ref.shape[1]
    sub = min(tm, FFN_SUB_ROWS)

    def stage_rows(r0):
        rows = slice(r0, r0 + sub)
        y = _dot(o_ref[0, rows, :], wo_ref[...])
        yield
        x1 = x_ref[0, rows, :] + g1_ref[0] * y
        ms = jnp.mean(x1 * x1, axis=-1, keepdims=True)
        h2 = ((x1 * lax.rsqrt(ms + EPS)) * geff + sh2_ref[0]).astype(BF16)
        gate = _dot(h2, wg_ref[...])
        up = _dot(h2, wu_ref[...])
        yield
        act = (_silu(gate) * up).astype(BF16)
        down = _dot(act, wd_ref[...])
        yield
        out_ref[0, rows, :] = x1 + g2_ref[0] * down

    live = [stage_rows(r0) for r0 in range(0, tm, sub)]
    while live:
        for gen in list(live):
            if next(gen, StopIteration) is StopIteration:
                live.remove(gen)


def _ffn_call(x, o, mod3, n2, wo, wg, wu, wd, tm):
    b, s, d = x.shape
    hid = wg.shape[1]
    modspec = lambda j: pl.BlockSpec((1, 1, d), lambda bi, i: (bi, 0, j))
    const = lambda shape: pl.BlockSpec(shape, lambda bi, i: (0, 0), pipeline_mode=pl.Buffered(1))
    return pl.pallas_call(
        _ffn_kernel,
        grid=(b, s // tm),
        in_specs=[
            pl.BlockSpec((1, tm, d), lambda bi, i: (bi, i, 0)),
            pl.BlockSpec((1, tm, o.shape[2]), lambda bi, i: (bi, i, 0)),
            modspec(2), modspec(3), modspec(4), modspec(5),
            pl.BlockSpec((1, d), lambda bi, i: (0, 0)),
            const((o.shape[2], d)), const((d, hid)), const((d, hid)), const((hid, d)),
        ],
        out_specs=pl.BlockSpec((1, tm, d), lambda bi, i: (bi, i, 0)),
        out_shape=jax.ShapeDtypeStruct((b, s, d), F32),
        compiler_params=pltpu.CompilerParams(dimension_semantics=("parallel", "arbitrary"),
                                             vmem_limit_bytes=VMEM_LIMIT),
        name="out_ffn",
    )(x, o, mod3, mod3, mod3, mod3, n2, wo, wg, wu, wd)


def _layer(x, ctx, mod, norm1_g, w_in, qn_a, kn_a, sink_a, qn_b, kn_b, rpb_b, on_a, on_b, w_out,
           norm2_g, w_gate, w_up, w_down, rope_tabs):
    b, s, d = x.shape
    rows = s // GRID_W
    w_l, w_c, wva_t, wvb_t = _wprep_call(w_in)
    scale = HEAD_DIM ** -0.5 * LOG2E
    gain_l = jnp.concatenate([jnp.tile(qn_a, A_HEADS) * scale, jnp.tile(kn_a, A_KV_HEADS),
                              jnp.tile(qn_b, B_HEADS) * scale, jnp.tile(kn_b, B_HEADS)])[None]
    gain_c = jnp.concatenate([jnp.tile(kn_a, A_KV_HEADS), jnp.tile(kn_b, B_HEADS)])[None]
    ones_bd = jnp.asarray(np.kron(np.eye(MXU_TILE // HEAD_DIM), np.ones((HEAD_DIM, HEAD_DIM))), BF16)

    mod3 = mod[:b].reshape(b, 1, -1)
    mod3_c = mod[b:b + 1].reshape(1, 1, -1)
    g1 = norm1_g[None]

    qka = A_WIDTH + A_KV_WIDTH
    groups_l = [
        (0, qka, [(0, A_WIDTH, True, 0), (A_WIDTH, A_KV_WIDTH, True, 1)]),
        (qka, B_WIDTH, [(0, B_WIDTH, False, 2)]),
        (qka + B_WIDTH, B_WIDTH, [(0, B_WIDTH, False, 3)]),
    ]
    qa, ka, qb, kb, vat, vbt = _inproj_call(
        x, mod3, True, g1, w_l, gain_l, ones_bd, wva_t, wvb_t, rope_tabs, groups_l,
        [A_WIDTH, A_KV_WIDTH, B_WIDTH, B_WIDTH], tm=INPROJ_ROWS)
    groups_c = [(0, A_KV_WIDTH + B_WIDTH, [(0, A_KV_WIDTH, False, 0), (A_KV_WIDTH, B_WIDTH, False, 1)])]
    n_ctx = ctx.shape[1]
    ctx_out = _inproj_call(
        ctx.reshape(1, b * n_ctx, d), mod3_c, False, g1, w_c, gain_c, ones_bd, wva_t, wvb_t, None, groups_c,
        [A_KV_WIDTH, B_WIDTH], tm=CTX_ROWS)
    kac, kbc = (a.reshape(b, n_ctx, -1) for a in ctx_out[:2])
    vact, vbct = (a.reshape(b, n_ctx // LANES, -1, LANES) for a in ctx_out[2:])

    bias = _bias_call(rpb_b, rows)
    lanes_of = lambda g: jnp.broadcast_to(g[:, None], (g.shape[0], LANES))
    o, ffn_weights = _attn_call(sink_a, qa, qb, ka, vat, kb, vbt, kac, vact, kbc, vbct, bias, lanes_of(on_a),
                                lanes_of(on_b), [w_out, w_gate, w_up, w_down])
    return _ffn_call(x, o, mod3, norm2_g[None], *ffn_weights, tm=FFN_ROWS)


def kernel(x, c, ctx, c_ctx, w_mod, b_mod, norm1_g, w_in, qn_a, kn_a, sink_a, qn_b, kn_b, rpb_b, on_a, on_b,
           w_out, norm2_g, w_gate, w_up, w_down):
    b, s, d = x.shape
    depth = w_mod.shape[0]
    assert depth == 1, "the context stream update is only needed when a later layer reads it"
    rope_tabs = _rope_call(s)
    n_rows = -(-(b + 1) // BF16_ROWS) * BF16_ROWS
    cc = jnp.concatenate([c, c_ctx[None], jnp.zeros((n_rows - b - 1, d), F32)], axis=0)
    l = 0
    mod = _mod_call(cc, w_mod[l], b_mod[l][None])
    return _layer(x, ctx, mod, norm1_g[l], w_in[l], qn_a[l], kn_a[l], sink_a[l], qn_b[l], kn_b[l], rpb_b[l],
                  on_a[l], on_b[l], w_out[l], norm2_g[l], w_gate[l], w_up[l], w_down[l], rope_tabs)
```

```python
import functools

import numpy as np
import jax
import jax.numpy as jnp
from jax import lax
from jax.experimental import pallas as pl
from jax.experimental.pallas import tpu as pltpu

GRID_W = 64
HEAD_DIM = 64
A_HEADS = 8
A_KV_HEADS = 2
A_WINDOW = 128
B_HEADS = 8
NA_KH = 8
NA_KW = 16
NA_ROW_BLOCK = 2
ROPE_BASE = 10000.0
EPS = 1e-6

A_WIDTH = A_HEADS * HEAD_DIM
A_KV_WIDTH = A_KV_HEADS * HEAD_DIM
B_WIDTH = B_HEADS * HEAD_DIM
QBLK = NA_ROW_BLOCK * GRID_W
A_SPAN = QBLK + 2 * A_WINDOW
B_KROWS = 10
B_SPAN = B_KROWS * GRID_W
N_PATTERNS = 5
ATTN_BLOCKS_PER_STEP = 8
MOD_COLS = 2048
INPROJ_ROWS = 2048
CTX_ROWS = 512
INPROJ_SUB_ROWS = 512
FFN_ROWS = 1024
FFN_SUB_ROWS = 256
A_STACK = 4
ATTN_PV_LAG = 1
ATTN_LOOKAHEAD = 1
NEG = -1e30
LOG2E = 1.4426950408889634

LANES = 128
BF16_ROWS = 16
MXU_TILE = 256
VMEM_LIMIT = 56 * 1024 * 1024

F32 = jnp.float32
BF16 = jnp.bfloat16


def _dot(a, b):
    return jnp.dot(a, b, preferred_element_type=F32)


def _dot_nt(a, b):
    return lax.dot_general(a, b, (((1,), (1,)), ((), ())), preferred_element_type=F32)


def _silu(x):
    return x / (1.0 + jnp.exp(-x))


def _mod_kernel(c_ref, w_ref, b_ref, o_ref):
    split = lambda v: (v.astype(BF16), (v - v.astype(BF16).astype(F32)).astype(BF16))
    a_hi, a_lo = split(_silu(c_ref[...]))
    w_hi, w_lo = split(w_ref[...])
    rows = a_hi.shape[0]
    head = _dot(jnp.concatenate([a_hi, a_lo], axis=0), w_hi)
    o_ref[...] = head[:rows] + head[rows:] + _dot(a_hi, w_lo) + b_ref[...]


def _mod_call(cc, w_mod, b_mod):
    rows, d = cc.shape
    n = w_mod.shape[1]
    bn = MOD_COLS
    return pl.pallas_call(
        _mod_kernel,
        grid=(n // bn,),
        in_specs=[pl.BlockSpec((rows, d), lambda j: (0, 0)),
                  pl.BlockSpec((d, bn), lambda j: (0, j)),
                  pl.BlockSpec((1, bn), lambda j: (0, j))],
        out_specs=pl.BlockSpec((rows, bn), lambda j: (0, j)),
        out_shape=jax.ShapeDtypeStruct((rows, n), F32),
        name="mod",
    )(cc, w_mod, b_mod)


def _wprep_kernel(w_ref, wl_ref, wc_ref, wvat_ref, wvbt_ref):
    group = A_HEADS // A_KV_HEADS
    c_ka, c_va, c_qb, c_kb, c_vb = np.cumsum([A_WIDTH, A_KV_WIDTH, A_KV_WIDTH, B_WIDTH, B_WIDTH])
    lane = lax.broadcasted_iota(jnp.int32, (w_ref.shape[0], LANES), 1)
    lo = lane < HEAD_DIM
    for t in range(group):
        src = [w_ref[:, (h // 2) * LANES:(h // 2 + 1) * LANES] for h in (t, t + group)]
        if t % 2 == 0:
            tile = jnp.where(lo, src[0], pltpu.roll(src[1], HEAD_DIM, 1))
        else:
            tile = jnp.where(lo, pltpu.roll(src[0], HEAD_DIM, 1), src[1])
        wl_ref[:, t * LANES:(t + 1) * LANES] = tile.astype(BF16)
    k_a = w_ref[:, c_ka:c_va].astype(BF16)
    k_b = w_ref[:, c_kb:c_vb].astype(BF16)
    wl_ref[:, A_WIDTH:A_WIDTH + A_KV_WIDTH] = k_a
    wl_ref[:, A_WIDTH + A_KV_WIDTH:A_WIDTH + A_KV_WIDTH + B_WIDTH] = w_ref[:, c_qb:c_kb].astype(BF16)
    wl_ref[:, A_WIDTH + A_KV_WIDTH + B_WIDTH:] = k_b
    wc_ref[:, :A_KV_WIDTH] = k_a
    wc_ref[:, A_KV_WIDTH:] = k_b
    wvat_ref[...] = w_ref[:, c_va:c_qb].T.astype(BF16)
    for c in range(0, B_WIDTH, LANES):
        wvbt_ref[c:c + LANES, :] = w_ref[:, c_vb + c:c_vb + c + LANES].T.astype(BF16)


def _wprep_call(w_in):
    d, n = w_in.shape
    assert (A_HEADS // A_KV_HEADS) % 2 == 0 and n == A_WIDTH + 2 * A_KV_WIDTH + 3 * B_WIDTH
    shapes = [(d, A_WIDTH + A_KV_WIDTH + 2 * B_WIDTH), (d, A_KV_WIDTH + B_WIDTH), (A_KV_WIDTH, d), (B_WIDTH, d)]
    return pl.pallas_call(
        _wprep_kernel,
        grid=(1,),
        in_specs=[pl.BlockSpec((d, n), lambda i: (0, 0))],
        out_specs=[pl.BlockSpec(s, lambda i: (0, 0)) for s in shapes],
        out_shape=[jax.ShapeDtypeStruct(s, BF16) for s in shapes],
        compiler_params=pltpu.CompilerParams(vmem_limit_bytes=VMEM_LIMIT),
        name="w_in_layout",
    )(w_in)


def _rope_kernel(invf_ref, cos_ref, sin_ref, *, rows):
    n_pos = max(rows, GRID_W)
    pos = lax.broadcasted_iota(jnp.int32, (n_pos, LANES), 0).astype(F32)
    lane = lax.broadcasted_iota(jnp.int32, (GRID_W, LANES), 1)
    row_lanes = (lane & (HEAD_DIM - 1)) < HEAD_DIM // 2
    first = (lane & (HEAD_DIM // 2 - 1)) < HEAD_DIM // 4
    ang = pos * invf_ref[...]
    cos_p, sin_p = jnp.cos(ang), jnp.sin(ang)
    cos_c = cos_p[:GRID_W]
    sin_c = jnp.where(first, -sin_p[:GRID_W], sin_p[:GRID_W])
    for r in range(rows):
        tok = slice(r * GRID_W, (r + 1) * GRID_W)
        cos_ref[tok, :] = jnp.where(row_lanes, cos_p[r:r + 1], cos_c)
        sin_ref[tok, :] = jnp.where(row_lanes, jnp.where(first, -sin_p[r:r + 1], sin_p[r:r + 1]), sin_c)


def _rope_call(seq):
    quarter = HEAD_DIM // 4
    inv = (1.0 / (np.float32(ROPE_BASE) ** (np.arange(quarter, dtype=np.float32) / quarter))).astype(np.float32)
    invf = jnp.asarray(np.tile(inv, LANES // quarter)[None, :])
    return pl.pallas_call(
        functools.partial(_rope_kernel, rows=seq // GRID_W),
        grid=(1,),
        in_specs=[pl.BlockSpec((1, LANES), lambda i: (0, 0))],
        out_specs=[pl.BlockSpec((seq, LANES), lambda i: (0, 0))] * 2,
        out_shape=[jax.ShapeDtypeStruct((seq, LANES), F32)] * 2,
        name="rope_tables",
    )(invf)


def _b_patterns(rows):
    n_blk = rows // NA_ROW_BLOCK
    blocks = [0, 1, 2, n_blk - 2, n_blk - 1]
    return [(NA_ROW_BLOCK * i, _b_key_start(i, rows)) for i in blocks]


def _b_key_start(i, rows):
    return min(max(NA_ROW_BLOCK * i - NA_KH // 2, 0), rows - B_KROWS)


def _bias_kernel(rpb_ref, o_ref, *, patterns, rows):
    h = pl.program_id(0)
    n_dr, n_dc = 2 * NA_KH - 1, 2 * NA_KW - 1
    kc = lax.broadcasted_iota(jnp.int32, (GRID_W, LANES), 0)
    lane = lax.broadcasted_iota(jnp.int32, (GRID_W, LANES), 1)
    qc = lane & (GRID_W - 1)
    dc = jnp.clip(kc - qc + NA_KW - 1, 0, n_dc - 1)
    cs = jnp.clip(qc - NA_KW // 2, 0, GRID_W - NA_KW)
    col_ok = (kc >= cs) & (kc < cs + NA_KW)
    lo = lane < GRID_W
    base = h * (n_dr * n_dc)
    per_dr = []
    for dr in range(n_dr):
        m = jnp.zeros((GRID_W, LANES), F32)
        for d in range(n_dc):
            m = jnp.where(dc == d, rpb_ref[base + dr * n_dc + d] * LOG2E, m)
        per_dr.append(m)
    neg = jnp.full((GRID_W, LANES), NEG, F32)
    for p, (r0, ks) in enumerate(patterns):
        q_rows = [r0 + qr for qr in range(NA_ROW_BLOCK)]
        rs = [min(max(q - NA_KH // 2, 0), rows - NA_KH) for q in q_rows]
        for kr in range(B_KROWS):
            k_row = ks + kr
            ok = [r <= k_row < r + NA_KH for r in rs]
            drs = [min(max(k_row - q + NA_KH - 1, 0), n_dr - 1) for q in q_rows]
            if not any(ok):
                piece = neg
            else:
                vals = jnp.where(lo, per_dr[drs[0]], per_dr[drs[1]])
                mask = col_ok
                if not ok[1]:
                    mask = mask & lo
                if not ok[0]:
                    mask = mask & jnp.logical_not(lo)
                piece = jnp.where(mask, vals, neg)
            o_ref[p, 0, kr * GRID_W:(kr + 1) * GRID_W, :] = piece


def _bias_call(rpb, rows):
    patterns = _b_patterns(rows)
    heads = rpb.shape[0]
    return pl.pallas_call(
        functools.partial(_bias_kernel, patterns=patterns, rows=rows),
        grid=(heads,),
        in_specs=[pl.BlockSpec(memory_space=pltpu.SMEM)],
        out_specs=pl.BlockSpec((N_PATTERNS, 1, B_SPAN, QBLK), lambda h: (0, h // 2, 0, h % 2)),
        out_shape=jax.ShapeDtypeStruct((N_PATTERNS, heads // 2, B_SPAN, 2 * QBLK), F32),
        name="na_bias",
    )(rpb.reshape(-1))


def _group_rms(p, ones_ref):
    sq = (p * p).astype(BF16)
    n = p.shape[1]
    if n <= MXU_TILE:
        ssq = _dot(sq, ones_ref[:n, :n])
    else:
        ssq = jnp.concatenate(
            [_dot(sq[:, c:c + MXU_TILE], ones_ref[...]) for c in range(0, n, MXU_TILE)], axis=1)
    return lax.rsqrt(ssq * (1.0 / HEAD_DIM) + EPS)


def _rope(x, cos, sin):
    lane = lax.broadcasted_iota(jnp.int32, (x.shape[0], LANES), 1)
    first = (lane & (HEAD_DIM // 2 - 1)) < HEAD_DIM // 4
    outs = []
    for c in range(0, x.shape[1], LANES):
        xt = x[:, c:c + LANES]
        sw = jnp.where(first, pltpu.roll(xt, LANES - HEAD_DIM // 4, 1), pltpu.roll(xt, HEAD_DIM // 4, 1))
        outs.append(xt * cos + sw * sin)
    return outs[0] if len(outs) == 1 else jnp.concatenate(outs, axis=1)


def _inproj_kernel(x_ref, sh_ref, sc_ref, g_ref, w_ref, gain_ref, ones_ref, wva_ref, wvb_ref, *rest,
                   groups, use_rope):
    if use_rope:
        cos, sin = rest[0][...], rest[1][...]
        rest = rest[2:]
    n_std = sum(len(subs) for _, _, subs in groups)
    out_refs, vt_refs = rest[:n_std], rest[n_std:]
    geff = g_ref[...] * (1.0 + sc_ref[0])
    shift = sh_ref[0]
    tm = x_ref.shape[1]
    sub = min(tm, INPROJ_SUB_ROWS)
    h_cache = {}

    def h_of(r0):
        if r0 not in h_cache:
            x = x_ref[0, r0:r0 + sub, :]
            ms = jnp.mean(x * x, axis=-1, keepdims=True)
            h_cache[r0] = ((x * lax.rsqrt(ms + EPS)) * geff + shift).astype(BF16)
        return h_cache[r0]

    def project(r0, c0, width, subs):
        return _dot(h_of(r0), w_ref[:, c0:c0 + width])

    def finish(p, r0, c0, width, subs):
        for s0, sw, rope, oi in subs:
            y = p[:, s0:s0 + sw]
            y = y * _group_rms(y, ones_ref) * gain_ref[:, c0 + s0:c0 + s0 + sw]
            if rope:
                y = _rope(y, cos[r0:r0 + sub], sin[r0:r0 + sub])
            out_refs[oi][0, r0:r0 + sub, :] = y.astype(BF16)

    def project_vt(r0, wv_ref, vt_ref):
        return _dot_nt(wv_ref[...], h_of(r0))

    def finish_vt(vt, r0, wv_ref, vt_ref):
        for j in range(sub // LANES):
            vt_ref[0, r0 // LANES + j] = vt[:, j * LANES:(j + 1) * LANES].astype(BF16)

    stages = []
    for r0 in range(0, tm, sub):
        stages += [(project, finish, (r0,) + g) for g in groups]
        stages += [(project_vt, finish_vt, (r0,) + a) for a in zip((wva_ref, wvb_ref), vt_refs)]
    pending = None
    for first, second, args in stages:
        res = first(*args)
        if pending is not None:
            pending[0](pending[1], *pending[2])
        pending = (second, res, args)
    pending[0](pending[1], *pending[2])


def _inproj_call(x, mod3, mod_batched, g, w, gain, ones, wva_t, wvb_t, rope_tabs, groups, out_widths, tm):
    b, s, d = x.shape
    n = w.shape[1]
    use_rope = rope_tabs is not None
    mod_idx = (lambda bi, i, j: (bi, 0, j)) if mod_batched else (lambda bi, i, j: (0, 0, j))
    const = lambda arr: pl.BlockSpec(arr.shape, lambda bi, i: (0, 0))
    in_specs = [
        pl.BlockSpec((1, tm, d), lambda bi, i: (bi, i, 0)),
        pl.BlockSpec((1, 1, d), lambda bi, i: mod_idx(bi, i, 0)),
        pl.BlockSpec((1, 1, d), lambda bi, i: mod_idx(bi, i, 1)),
        const(g), const(w), const(gain), const(ones), const(wva_t), const(wvb_t),
    ]
    args = [x, mod3, mod3, g, w, gain, ones, wva_t, wvb_t]
    if use_rope:
        in_specs += [pl.BlockSpec((tm, LANES), lambda bi, i: (i, 0))] * 2
        args += list(rope_tabs)
    vt_dims = [wva_t.shape[0], wvb_t.shape[0]]
    return pl.pallas_call(
        functools.partial(_inproj_kernel, groups=groups, use_rope=use_rope),
        grid=(b, s // tm),
        in_specs=in_specs,
        out_specs=([pl.BlockSpec((1, tm, wd), lambda bi, i: (bi, i, 0)) for wd in out_widths]
                   + [pl.BlockSpec((1, tm // LANES, vd, LANES), lambda bi, i: (bi, i, 0, 0)) for vd in vt_dims]),
        out_shape=([jax.ShapeDtypeStruct((b, s, wd), BF16) for wd in out_widths]
                   + [jax.ShapeDtypeStruct((b, s // LANES, vd, LANES), BF16) for vd in vt_dims]),
        compiler_params=pltpu.CompilerParams(dimension_semantics=("parallel", "arbitrary"),
                                             vmem_limit_bytes=VMEM_LIMIT),
        name="in_proj_rope" if use_rope else "in_proj_ctx",
    )(*args)


def _scores_t(qs, k_loc, k_ctx, add_loc):
    return _dot_nt(k_loc, qs) + add_loc, _dot_nt(k_ctx, qs)


def _softmax_t(s_loc, s_ctx, extra_logit):
    m = jnp.maximum(jnp.max(s_loc, axis=0, keepdims=True), jnp.max(s_ctx, axis=0, keepdims=True))
    if extra_logit is not None:
        m = jnp.maximum(m, extra_logit)
    return jnp.exp2((s_loc - m).astype(BF16)), jnp.exp2((s_ctx - m).astype(BF16)), m


def _pv_t(p_loc, p_ctx, m, vt_loc, vt_ctx, extra_logit):
    n_v = vt_loc.shape[0]
    with_ones = lambda vt: jnp.concatenate([vt, jnp.ones((BF16_ROWS, vt.shape[1]), BF16)], axis=0)
    acc = _dot(with_ones(vt_loc), p_loc) + _dot(with_ones(vt_ctx), p_ctx)
    l = acc[n_v:n_v + 1]
    if extra_logit is not None:
        l = l + jnp.exp2(extra_logit - m)
    return acc[:n_v] * (1.0 / l)


def _group_norm_t(o_t, g_t):
    ms = jnp.mean(o_t * o_t, axis=0, keepdims=True)
    return ((o_t * lax.rsqrt(ms + EPS)) * g_t).T.astype(BF16)


def _lane_tiles(ref, first, n, rows=slice(None)):
    return jnp.concatenate([ref[0, first + j, rows, :] for j in range(n)], axis=1)


def _attn_kernel(sink_ref, qa_ref, qb_ref, ka_ref, vat_ref, kb_ref, vbt_ref, kac_ref, vact_ref,
                 kbc_ref, vbct_ref, ga_ref, gb_ref, bias_ref, *rest, seq, n_sub, n_cast):
    o_ref = rest[n_cast]
    for src_ref, dst_ref in zip(rest[:n_cast], rest[n_cast + 1:]):
        dst_ref[...] = src_ref[0].astype(BF16)
    groups = []
    for j in range(n_sub):
        groups += _attn_block_groups(
            pl.program_id(1) * n_sub + j, slice(j * QBLK, (j + 1) * QBLK), sink_ref, qa_ref, qb_ref, ka_ref,
            vat_ref, kb_ref, vbt_ref, kac_ref, vact_ref, kbc_ref, vbct_ref, ga_ref, gb_ref, bias_ref, o_ref, seq)
    scored, soft = [], []
    for scores, finish in groups:
        scored.append(finish(*scores()))
        if len(scored) > ATTN_LOOKAHEAD:
            gen = scored.pop(0)
            next(gen)
            soft.append(gen)
        if len(soft) > ATTN_PV_LAG:
            next(soft.pop(0), None)
    for gen in scored:
        next(gen)
        soft.append(gen)
    for gen in soft:
        next(gen, None)


def _attn_block_groups(i, qrows, sink_ref, qa_ref, qb_ref, ka_ref, vat_ref, kb_ref, vbt_ref, kac_ref, vact_ref,
                       kbc_ref, vbct_ref, ga_ref, gb_ref, bias_ref, o_ref, seq):
    rows = seq // GRID_W
    stages = []
    n_ctx_tiles = kac_ref.shape[1] // LANES
    lane = lax.broadcasted_iota(jnp.int32, (QBLK, LANES), 1)
    lo = lane < HEAD_DIM
    half_masks = (lo.astype(F32).astype(BF16), jnp.logical_not(lo).astype(F32).astype(BF16))

    group = A_HEADS // A_KV_HEADS
    blk_a = jnp.clip(i - 1, 0, seq // QBLK - A_SPAN // QBLK)
    start_a = pl.multiple_of(blk_a * QBLK, QBLK)
    d = (lax.broadcasted_iota(jnp.int32, (A_SPAN, QBLK), 0)
         - lax.broadcasted_iota(jnp.int32, (A_SPAN, QBLK), 1) + (start_a - i * QBLK))
    amask = jnp.where((d <= A_WINDOW) & (d >= -A_WINDOW), 0.0, NEG).astype(F32)
    amask = jnp.concatenate([amask] * A_STACK, axis=1)
    k_loc = ka_ref[0, pl.ds(start_a, A_SPAN), :]
    k_ctx = kac_ref[0]
    q_tiles = [qa_ref[0, qrows, t * LANES:(t + 1) * LANES] for t in range(group)]
    heads_a = []

    def scores_a(kv, t0):
        qs = jnp.concatenate([q * half_masks[kv] for q in q_tiles[t0:t0 + A_STACK]], axis=0)
        return _scores_t(qs, k_loc, k_ctx, amask)

    def finish_a(kv, t0, s_loc, s_ctx):
        dims = slice(kv * HEAD_DIM, (kv + 1) * HEAD_DIM)
        sink = jnp.concatenate(
            [jnp.full((1, QBLK), sink_ref[kv * group + t0 + t] * LOG2E, F32) for t in range(A_STACK)], axis=1)
        p_loc, p_ctx, m = _softmax_t(s_loc, s_ctx, sink)
        yield
        o_t = _pv_t(p_loc, p_ctx, m, _lane_tiles(vat_ref, blk_a, A_SPAN // LANES, dims),
                    _lane_tiles(vact_ref, 0, n_ctx_tiles, dims), sink)
        heads_a.extend(o_t[:, t * QBLK:(t + 1) * QBLK] for t in range(A_STACK))
        if len(heads_a) == A_HEADS:
            o_ref[0, qrows, :A_WIDTH] = _group_norm_t(jnp.concatenate(heads_a, axis=0), ga_ref[...])

    stages += [(functools.partial(scores_a, kv, t0), functools.partial(finish_a, kv, t0))
               for kv in range(A_KV_HEADS) for t0 in range(0, group, A_STACK)]

    blk_b = jnp.clip(NA_ROW_BLOCK * i - NA_KH // 2, 0, rows - B_KROWS) // NA_ROW_BLOCK
    start_b = pl.multiple_of(blk_b * QBLK, QBLK)
    heads_b = []
    n_blk = seq // QBLK
    pattern = jnp.where(i < 2, i, jnp.where(i >= n_blk - 2, i - (n_blk - N_PATTERNS), 2))

    def scores_b(pair):
        cols = slice(pair * LANES, (pair + 1) * LANES)
        qp = qb_ref[0, qrows, cols]
        qs = jnp.concatenate([qp * half_masks[0], qp * half_masks[1]], axis=0)
        return _scores_t(qs, kb_ref[0, pl.ds(start_b, B_SPAN), cols], kbc_ref[0, :, cols], bias_ref[pattern, pair])

    def finish_b(pair, s_loc, s_ctx):
        cols = slice(pair * LANES, (pair + 1) * LANES)
        p_loc, p_ctx, m = _softmax_t(s_loc, s_ctx, None)
        yield
        o_t = _pv_t(p_loc, p_ctx, m, _lane_tiles(vbt_ref, blk_b, B_SPAN // LANES, cols),
                    _lane_tiles(vbct_ref, 0, n_ctx_tiles, cols), None)
        heads_b.extend([o_t[:HEAD_DIM, :QBLK], o_t[HEAD_DIM:, QBLK:]])
        if pair == B_HEADS // 2 - 1:
            o_ref[0, qrows, A_WIDTH:] = _group_norm_t(jnp.concatenate(heads_b, axis=0), gb_ref[...])

    stages += [(functools.partial(scores_b, p), functools.partial(finish_b, p)) for p in range(B_HEADS // 2)]
    return stages


def _attn_call(sink, qa, qb, ka, vat, kb, vbt, kac, vact, kbc, vbct, bias, ga_t, gb_t, cast_weights):
    b, s, _ = qa.shape
    n_blk = s // QBLK
    assert (NA_KH // 2) % NA_ROW_BLOCK == 0 and (s // GRID_W - B_KROWS) % NA_ROW_BLOCK == 0

    n_sub = ATTN_BLOCKS_PER_STEP
    steps_per_batch = n_blk // n_sub
    full = lambda arr: pl.BlockSpec((1,) + arr.shape[1:], lambda bi, i: (bi,) + (0,) * (arr.ndim - 1))
    const = lambda arr, **kw: pl.BlockSpec(arr.shape, lambda bi, i: (0,) * arr.ndim, **kw)
    blk = lambda wd: pl.BlockSpec((1, n_sub * QBLK, wd), lambda bi, i: (bi, i, 0))
    n_steps = b * steps_per_batch
    in_slabs, out_slabs = [], []
    for w in cast_weights:
        _, rows, cols = w.shape
        c = next(c for c in range(1, n_steps + 1)
                 if n_steps % c == 0 and cols % (c * LANES) == 0 and rows % (n_steps // c * BF16_ROWS) == 0)
        r = n_steps // c
        step = lambda bi, i: bi * steps_per_batch + i
        in_slabs.append(pl.BlockSpec((1, rows // r, cols // c),
                                     lambda bi, i, c=c: (0, step(bi, i) // c, step(bi, i) % c)))
        out_slabs.append(pl.BlockSpec((rows // r, cols // c), lambda bi, i, c=c: (step(bi, i) // c, step(bi, i) % c)))
    outs = pl.pallas_call(
        functools.partial(_attn_kernel, seq=s, n_sub=n_sub, n_cast=len(cast_weights)),
        grid=(b, steps_per_batch),
        in_specs=[
            pl.BlockSpec(memory_space=pltpu.SMEM),
            blk(A_WIDTH), blk(B_WIDTH),
            full(ka), full(vat), full(kb), full(vbt), full(kac), full(vact), full(kbc), full(vbct),
            const(ga_t), const(gb_t), const(bias, pipeline_mode=pl.Buffered(1)),
        ] + in_slabs,
        out_specs=[blk(A_WIDTH + B_WIDTH)] + out_slabs,
        out_shape=([jax.ShapeDtypeStruct((b, s, A_WIDTH + B_WIDTH), BF16)]
                   + [jax.ShapeDtypeStruct(w.shape[1:], BF16) for w in cast_weights]),
        compiler_params=pltpu.CompilerParams(dimension_semantics=("parallel", "arbitrary"),
                                             vmem_limit_bytes=VMEM_LIMIT),
        name="attn",
    )(sink, qa, qb, ka, vat, kb, vbt, kac, vact, kbc, vbct, ga_t, gb_t, bias, *cast_weights)
    return outs[0], outs[1:]


def _ffn_kernel(x_ref, o_ref, g1_ref, sh2_ref, sc2_ref, g2_ref, n2_ref, wo_ref, wg_ref, wu_ref, wd_ref,
                out_ref):
    geff = n2_ref[...] * (1.0 + sc2_ref[0])
    tm = x_ref.shape[1]
    sub = min(tm, FFN_SUB_ROWS)

    def stage_rows(r0):
        rows = slice(r0, r0 + sub)
        y = _dot(o_ref[0, rows, :], wo_ref[...])
        yield
        x1 = x_ref[0, rows, :] + g1_ref[0] * y
        ms = jnp.mean(x1 * x1, axis=-1, keepdims=True)
        h2 = ((x1 * lax.rsqrt(ms + EPS)) * geff + sh2_ref[0]).astype(BF16)
        gate = _dot(h2, wg_ref[...])
        up = _dot(h2, wu_ref[...])
        yield
        act = (_silu(gate) * up).astype(BF16)
        down = _dot(act, wd_ref[...])
        yield
        out_ref[0, rows, :] = x1 + g2_ref[0] * down

    live = [stage_rows(r0) for r0 in range(0, tm, sub)]
    while live:
        for gen in list(live):
            if next(gen, StopIteration) is StopIteration:
                live.remove(gen)


def _ffn_call(x, o, mod3, n2, wo, wg, wu, wd, tm):
    b, s, d = x.shape
    hid = wg.shape[1]
    modspec = lambda j: pl.BlockSpec((1, 1, d), lambda bi, i: (bi, 0, j))
    const = lambda shape: pl.BlockSpec(shape, lambda bi, i: (0, 0), pipeline_mode=pl.Buffered(1))
    return pl.pallas_call(
        _ffn_kernel,
        grid=(b, s // tm),
        in_specs=[
            pl.BlockSpec((1, tm, d), lambda bi, i: (bi, i, 0)),
            pl.BlockSpec((1, tm, o.shape[2]), lambda bi, i: (bi, i, 0)),
            modspec(2), modspec(3), modspec(4), modspec(5),
            pl.BlockSpec((1, d), lambda bi, i: (0, 0)),
            const((o.shape[2], d)), const((d, hid)), const((d, hid)), const((hid, d)),
        ],
        out_specs=pl.BlockSpec((1, tm, d), lambda bi, i: (bi, i, 0)),
        out_shape=jax.ShapeDtypeStruct((b, s, d), F32),
        compiler_params=pltpu.CompilerParams(dimension_semantics=("parallel", "arbitrary"),
                                             vmem_limit_bytes=VMEM_LIMIT),
        name="out_ffn",
    )(x, o, mod3, mod3, mod3, mod3, n2, wo, wg, wu, wd)


def _layer(x, ctx, mod, norm1_g, w_in, qn_a, kn_a, sink_a, qn_b, kn_b, rpb_b, on_a, on_b, w_out,
           norm2_g, w_gate, w_up, w_down, rope_tabs):
    b, s, d = x.shape
    rows = s // GRID_W
    w_l, w_c, wva_t, wvb_t = _wprep_call(w_in)
    scale = HEAD_DIM ** -0.5 * LOG2E
    gain_l = jnp.concatenate([jnp.tile(qn_a, A_HEADS) * scale, jnp.tile(kn_a, A_KV_HEADS),
                              jnp.tile(qn_b, B_HEADS) * scale, jnp.tile(kn_b, B_HEADS)])[None]
    gain_c = jnp.concatenate([jnp.tile(kn_a, A_KV_HEADS), jnp.tile(kn_b, B_HEADS)])[None]
    ones_bd = jnp.asarray(np.kron(np.eye(MXU_TILE // HEAD_DIM), np.ones((HEAD_DIM, HEAD_DIM))), BF16)

    mod3 = mod[:b].reshape(b, 1, -1)
    mod3_c = mod[b:b + 1].reshape(1, 1, -1)
    g1 = norm1_g[None]

    qka = A_WIDTH + A_KV_WIDTH
    groups_l = [
        (0, qka, [(0, A_WIDTH, True, 0), (A_WIDTH, A_KV_WIDTH, True, 1)]),
        (qka, B_WIDTH, [(0, B_WIDTH, False, 2)]),
        (qka + B_WIDTH, B_WIDTH, [(0, B_WIDTH, False, 3)]),
    ]
    qa, ka, qb, kb, vat, vbt = _inproj_call(
        x, mod3, True, g1, w_l, gain_l, ones_bd, wva_t, wvb_t, rope_tabs, groups_l,
        [A_WIDTH, A_KV_WIDTH, B_WIDTH, B_WIDTH], tm=INPROJ_ROWS)
    groups_c = [(0, A_KV_WIDTH + B_WIDTH, [(0, A_KV_WIDTH, False, 0), (A_KV_WIDTH, B_WIDTH, False, 1)])]
    n_ctx = ctx.shape[1]
    ctx_out = _inproj_call(
        ctx.reshape(1, b * n_ctx, d), mod3_c, False, g1, w_c, gain_c, ones_bd, wva_t, wvb_t, None, groups_c,
        [A_KV_WIDTH, B_WIDTH], tm=CTX_ROWS)
    kac, kbc = (a.reshape(b, n_ctx, -1) for a in ctx_out[:2])
    vact, vbct = (a.reshape(b, n_ctx // LANES, -1, LANES) for a in ctx_out[2:])

    bias = _bias_call(rpb_b, rows)
    lanes_of = lambda g: jnp.broadcast_to(g[:, None], (g.shape[0], LANES))
    o, ffn_weights = _attn_call(sink_a, qa, qb, ka, vat, kb, vbt, kac, vact, kbc, vbct, bias, lanes_of(on_a),
                                lanes_of(on_b), [w_out, w_gate, w_up, w_down])
    return _ffn_call(x, o, mod3, norm2_g[None], *ffn_weights, tm=FFN_ROWS)


def kernel(x, c, ctx, c_ctx, w_mod, b_mod, norm1_g, w_in, qn_a, kn_a, sink_a, qn_b, kn_b, rpb_b, on_a, on_b,
           w_out, norm2_g, w_gate, w_up, w_down):
    b, s, d = x.shape
    depth = w_mod.shape[0]
    assert depth == 1, "the context stream update is only needed when a later layer reads it"
    rope_tabs = _rope_call(s)
    n_rows = -(-(b + 1) // BF16_ROWS) * BF16_ROWS
    cc = jnp.concatenate([c, c_ctx[None], jnp.zeros((n_rows - b - 1, d), F32)], axis=0)
    l = 0
    mod = _mod_call(cc, w_mod[l], b_mod[l][None])
    return _layer(x, ctx, mod, norm1_g[l], w_in[l], qn_a[l], kn_a[l], sink_a[l], qn_b[l], kn_b[l], rpb_b[l],
                  on_a[l], on_b[l], w_out[l:l + 1], norm2_g[l], w_gate[l:l + 1], w_up[l:l + 1], w_down[l:l + 1],
                  rope_tabs)
```

```python
import functools

import numpy as np
import jax
import jax.numpy as jnp
from jax import lax
from jax.experimental import pallas as pl
from jax.experimental.pallas import tpu as pltpu

GRID_W = 64
HEAD_DIM = 64
A_HEADS = 8
A_KV_HEADS = 2
A_WINDOW = 128
B_HEADS = 8
NA_KH = 8
NA_KW = 16
NA_ROW_BLOCK = 2
ROPE_BASE = 10000.0
EPS = 1e-6

A_WIDTH = A_HEADS * HEAD_DIM
A_KV_WIDTH = A_KV_HEADS * HEAD_DIM
B_WIDTH = B_HEADS * HEAD_DIM
QBLK = NA_ROW_BLOCK * GRID_W
A_SPAN = QBLK + 2 * A_WINDOW
B_KROWS = 10
B_SPAN = B_KROWS * GRID_W
N_PATTERNS = 5
ATTN_BLOCKS_PER_STEP = 8
MOD_COLS = 1536
INPROJ_ROWS = 2048
CTX_ROWS = 512
INPROJ_SUB_ROWS = 512
FFN_ROWS = 1024
FFN_SUB_ROWS = 256
A_STACK = 4
ATTN_PV_LAG = 1
ATTN_LOOKAHEAD = 1
NEG = -1e30
LOG2E = 1.4426950408889634

LANES = 128
BF16_ROWS = 16
MXU_TILE = 256
VMEM_LIMIT = 56 * 1024 * 1024

F32 = jnp.float32
BF16 = jnp.bfloat16


def _dot(a, b):
    return jnp.dot(a, b, preferred_element_type=F32)


def _dot_nt(a, b):
    return lax.dot_general(a, b, (((1,), (1,)), ((), ())), preferred_element_type=F32)


def _silu(x):
    return x / (1.0 + jnp.exp(-x))


def _mod_kernel(c_ref, w_ref, b_ref, win_ref, o_ref, wl_ref, wc_ref, wvat_ref, wvbt_ref):
    split = lambda v: (v.astype(BF16), (v - v.astype(BF16).astype(F32)).astype(BF16))
    a_hi, a_lo = split(_silu(c_ref[...]))
    w_hi, w_lo = split(w_ref[...])
    rows = a_hi.shape[0]
    head = _dot(jnp.concatenate([a_hi, a_lo], axis=0), w_hi)
    o_ref[...] = head[:rows] + head[rows:] + _dot(a_hi, w_lo) + b_ref[...]
    _wprep_slab(win_ref, wl_ref, wc_ref, wvat_ref, wvbt_ref)


def _mod_call(cc, w_mod, b_mod, w_in):
    rows, d = cc.shape
    n = w_mod.shape[1]
    bn = MOD_COLS
    steps = n // bn
    d_in, n_in = w_in.shape
    slab = d_in // steps
    assert (A_HEADS // A_KV_HEADS) % 2 == 0 and n_in == A_WIDTH + 2 * A_KV_WIDTH + 3 * B_WIDTH
    assert d_in % steps == 0 and slab % LANES == 0
    row_slab = lambda cols: pl.BlockSpec((slab, cols), lambda j: (j, 0))
    col_slab = lambda r: pl.BlockSpec((r, slab), lambda j: (0, j))
    n_l, n_c = A_WIDTH + A_KV_WIDTH + 2 * B_WIDTH, A_KV_WIDTH + B_WIDTH
    return pl.pallas_call(
        _mod_kernel,
        grid=(steps,),
        in_specs=[pl.BlockSpec((rows, d), lambda j: (0, 0)),
                  pl.BlockSpec((d, bn), lambda j: (0, j)),
                  pl.BlockSpec((1, bn), lambda j: (0, j)),
                  row_slab(n_in)],
        out_specs=[pl.BlockSpec((rows, bn), lambda j: (0, j)),
                   row_slab(n_l), row_slab(n_c), col_slab(A_KV_WIDTH), col_slab(B_WIDTH)],
        out_shape=[jax.ShapeDtypeStruct((rows, n), F32),
                   jax.ShapeDtypeStruct((d_in, n_l), BF16), jax.ShapeDtypeStruct((d_in, n_c), BF16),
                   jax.ShapeDtypeStruct((A_KV_WIDTH, d_in), BF16), jax.ShapeDtypeStruct((B_WIDTH, d_in), BF16)],
        compiler_params=pltpu.CompilerParams(vmem_limit_bytes=VMEM_LIMIT),
        name="mod_w_in_layout",
    )(cc, w_mod, b_mod, w_in)


def _wprep_slab(w_ref, wl_ref, wc_ref, wvat_ref, wvbt_ref):
    group = A_HEADS // A_KV_HEADS
    c_ka, c_va, c_qb, c_kb, c_vb = np.cumsum([A_WIDTH, A_KV_WIDTH, A_KV_WIDTH, B_WIDTH, B_WIDTH])
    lane = lax.broadcasted_iota(jnp.int32, (w_ref.shape[0], LANES), 1)
    lo = lane < HEAD_DIM
    for t in range(group):
        src = [w_ref[:, (h // 2) * LANES:(h // 2 + 1) * LANES] for h in (t, t + group)]
        if t % 2 == 0:
            tile = jnp.where(lo, src[0], pltpu.roll(src[1], HEAD_DIM, 1))
        else:
            tile = jnp.where(lo, pltpu.roll(src[0], HEAD_DIM, 1), src[1])
        wl_ref[:, t * LANES:(t + 1) * LANES] = tile.astype(BF16)
    k_a = w_ref[:, c_ka:c_va].astype(BF16)
    k_b = w_ref[:, c_kb:c_vb].astype(BF16)
    wl_ref[:, A_WIDTH:A_WIDTH + A_KV_WIDTH] = k_a
    wl_ref[:, A_WIDTH + A_KV_WIDTH:A_WIDTH + A_KV_WIDTH + B_WIDTH] = w_ref[:, c_qb:c_kb].astype(BF16)
    wl_ref[:, A_WIDTH + A_KV_WIDTH + B_WIDTH:] = k_b
    wc_ref[:, :A_KV_WIDTH] = k_a
    wc_ref[:, A_KV_WIDTH:] = k_b
    wvat_ref[...] = w_ref[:, c_va:c_qb].T.astype(BF16)
    for c in range(0, B_WIDTH, LANES):
        wvbt_ref[c:c + LANES, :] = w_ref[:, c_vb + c:c_vb + c + LANES].T.astype(BF16)


def _rope_kernel(invf_ref, cos_ref, sin_ref, *, rows):
    n_pos = max(rows, GRID_W)
    pos = lax.broadcasted_iota(jnp.int32, (n_pos, LANES), 0).astype(F32)
    lane = lax.broadcasted_iota(jnp.int32, (GRID_W, LANES), 1)
    row_lanes = (lane & (HEAD_DIM - 1)) < HEAD_DIM // 2
    first = (lane & (HEAD_DIM // 2 - 1)) < HEAD_DIM // 4
    ang = pos * invf_ref[...]
    cos_p, sin_p = jnp.cos(ang), jnp.sin(ang)
    cos_c = cos_p[:GRID_W]
    sin_c = jnp.where(first, -sin_p[:GRID_W], sin_p[:GRID_W])
    for r in range(rows):
        tok = slice(r * GRID_W, (r + 1) * GRID_W)
        cos_ref[tok, :] = jnp.where(row_lanes, cos_p[r:r + 1], cos_c)
        sin_ref[tok, :] = jnp.where(row_lanes, jnp.where(first, -sin_p[r:r + 1], sin_p[r:r + 1]), sin_c)


def _rope_call(seq):
    quarter = HEAD_DIM // 4
    inv = (1.0 / (np.float32(ROPE_BASE) ** (np.arange(quarter, dtype=np.float32) / quarter))).astype(np.float32)
    invf = jnp.asarray(np.tile(inv, LANES // quarter)[None, :])
    return pl.pallas_call(
        functools.partial(_rope_kernel, rows=seq // GRID_W),
        grid=(1,),
        in_specs=[pl.BlockSpec((1, LANES), lambda i: (0, 0))],
        out_specs=[pl.BlockSpec((seq, LANES), lambda i: (0, 0))] * 2,
        out_shape=[jax.ShapeDtypeStruct((seq, LANES), F32)] * 2,
        name="rope_tables",
    )(invf)


def _b_patterns(rows):
    n_blk = rows // NA_ROW_BLOCK
    blocks = [0, 1, 2, n_blk - 2, n_blk - 1]
    return [(NA_ROW_BLOCK * i, _b_key_start(i, rows)) for i in blocks]


def _b_key_start(i, rows):
    return min(max(NA_ROW_BLOCK * i - NA_KH // 2, 0), rows - B_KROWS)


def _bias_kernel(rpb_ref, o_ref, *, patterns, rows):
    h = pl.program_id(0)
    n_dr, n_dc = 2 * NA_KH - 1, 2 * NA_KW - 1
    kc = lax.broadcasted_iota(jnp.int32, (GRID_W, LANES), 0)
    lane = lax.broadcasted_iota(jnp.int32, (GRID_W, LANES), 1)
    qc = lane & (GRID_W - 1)
    dc = jnp.clip(kc - qc + NA_KW - 1, 0, n_dc - 1)
    cs = jnp.clip(qc - NA_KW // 2, 0, GRID_W - NA_KW)
    col_ok = (kc >= cs) & (kc < cs + NA_KW)
    lo = lane < GRID_W
    base = h * (n_dr * n_dc)
    per_dr = []
    for dr in range(n_dr):
        m = jnp.zeros((GRID_W, LANES), F32)
        for d in range(n_dc):
            m = jnp.where(dc == d, rpb_ref[base + dr * n_dc + d] * LOG2E, m)
        per_dr.append(m)
    neg = jnp.full((GRID_W, LANES), NEG, F32)
    for p, (r0, ks) in enumerate(patterns):
        q_rows = [r0 + qr for qr in range(NA_ROW_BLOCK)]
        rs = [min(max(q - NA_KH // 2, 0), rows - NA_KH) for q in q_rows]
        for kr in range(B_KROWS):
            k_row = ks + kr
            ok = [r <= k_row < r + NA_KH for r in rs]
            drs = [min(max(k_row - q + NA_KH - 1, 0), n_dr - 1) for q in q_rows]
            if not any(ok):
                piece = neg
            else:
                vals = jnp.where(lo, per_dr[drs[0]], per_dr[drs[1]])
                mask = col_ok
                if not ok[1]:
                    mask = mask & lo
                if not ok[0]:
                    mask = mask & jnp.logical_not(lo)
                piece = jnp.where(mask, vals, neg)
            o_ref[p, 0, kr * GRID_W:(kr + 1) * GRID_W, :] = piece


def _bias_call(rpb, rows):
    patterns = _b_patterns(rows)
    heads = rpb.shape[0]
    return pl.pallas_call(
        functools.partial(_bias_kernel, patterns=patterns, rows=rows),
        grid=(heads,),
        in_specs=[pl.BlockSpec(memory_space=pltpu.SMEM)],
        out_specs=pl.BlockSpec((N_PATTERNS, 1, B_SPAN, QBLK), lambda h: (0, h // 2, 0, h % 2)),
        out_shape=jax.ShapeDtypeStruct((N_PATTERNS, heads // 2, B_SPAN, 2 * QBLK), F32),
        name="na_bias",
    )(rpb.reshape(-1))


def _group_rms(p, ones_ref):
    sq = (p * p).astype(BF16)
    n = p.shape[1]
    if n <= MXU_TILE:
        ssq = _dot(sq, ones_ref[:n, :n])
    else:
        ssq = jnp.concatenate(
            [_dot(sq[:, c:c + MXU_TILE], ones_ref[...]) for c in range(0, n, MXU_TILE)], axis=1)
    return lax.rsqrt(ssq * (1.0 / HEAD_DIM) + EPS)


def _rope(x, cos, sin):
    lane = lax.broadcasted_iota(jnp.int32, (x.shape[0], LANES), 1)
    first = (lane & (HEAD_DIM // 2 - 1)) < HEAD_DIM // 4
    outs = []
    for c in range(0, x.shape[1], LANES):
        xt = x[:, c:c + LANES]
        sw = jnp.where(first, pltpu.roll(xt, LANES - HEAD_DIM // 4, 1), pltpu.roll(xt, HEAD_DIM // 4, 1))
        outs.append(xt * cos + sw * sin)
    return outs[0] if len(outs) == 1 else jnp.concatenate(outs, axis=1)


def _inproj_kernel(x_ref, sh_ref, sc_ref, g_ref, w_ref, gain_ref, ones_ref, wva_ref, wvb_ref, *rest,
                   groups, use_rope):
    if use_rope:
        cos, sin = rest[0][...], rest[1][...]
        rest = rest[2:]
    n_std = sum(len(subs) for _, _, subs in groups)
    out_refs, vt_refs = rest[:n_std], rest[n_std:]
    geff = g_ref[...] * (1.0 + sc_ref[0])
    shift = sh_ref[0]
    tm = x_ref.shape[1]
    sub = min(tm, INPROJ_SUB_ROWS)
    h_cache = {}

    def h_of(r0):
        if r0 not in h_cache:
            x = x_ref[0, r0:r0 + sub, :]
            ms = jnp.mean(x * x, axis=-1, keepdims=True)
            h_cache[r0] = ((x * lax.rsqrt(ms + EPS)) * geff + shift).astype(BF16)
        return h_cache[r0]

    def project(r0, c0, width, subs):
        return _dot(h_of(r0), w_ref[:, c0:c0 + width])

    def finish(p, r0, c0, width, subs):
        for s0, sw, rope, oi in subs:
            y = p[:, s0:s0 + sw]
            y = y * _group_rms(y, ones_ref) * gain_ref[:, c0 + s0:c0 + s0 + sw]
            if rope:
                y = _rope(y, cos[r0:r0 + sub], sin[r0:r0 + sub])
            out_refs[oi][0, r0:r0 + sub, :] = y.astype(BF16)

    def project_vt(r0, wv_ref, vt_ref):
        return _dot_nt(wv_ref[...], h_of(r0))

    def finish_vt(vt, r0, wv_ref, vt_ref):
        for j in range(sub // LANES):
            vt_ref[0, r0 // LANES + j] = vt[:, j * LANES:(j + 1) * LANES].astype(BF16)

    stages = []
    for r0 in range(0, tm, sub):
        stages += [(project, finish, (r0,) + g) for g in groups]
        stages += [(project_vt, finish_vt, (r0,) + a) for a in zip((wva_ref, wvb_ref), vt_refs)]
    pending = None
    for first, second, args in stages:
        res = first(*args)
        if pending is not None:
            pending[0](pending[1], *pending[2])
        pending = (second, res, args)
    pending[0](pending[1], *pending[2])


def _inproj_call(x, mod3, mod_batched, g, w, gain, ones, wva_t, wvb_t, rope_tabs, groups, out_widths, tm):
    b, s, d = x.shape
    n = w.shape[1]
    use_rope = rope_tabs is not None
    mod_idx = (lambda bi, i, j: (bi, 0, j)) if mod_batched else (lambda bi, i, j: (0, 0, j))
    const = lambda arr: pl.BlockSpec(arr.shape, lambda bi, i: (0, 0))
    in_specs = [
        pl.BlockSpec((1, tm, d), lambda bi, i: (bi, i, 0)),
        pl.BlockSpec((1, 1, d), lambda bi, i: mod_idx(bi, i, 0)),
        pl.BlockSpec((1, 1, d), lambda bi, i: mod_idx(bi, i, 1)),
        const(g), const(w), const(gain), const(ones), const(wva_t), const(wvb_t),
    ]
    args = [x, mod3, mod3, g, w, gain, ones, wva_t, wvb_t]
    if use_rope:
        in_specs += [pl.BlockSpec((tm, LANES), lambda bi, i: (i, 0))] * 2
        args += list(rope_tabs)
    vt_dims = [wva_t.shape[0], wvb_t.shape[0]]
    return pl.pallas_call(
        functools.partial(_inproj_kernel, groups=groups, use_rope=use_rope),
        grid=(b, s // tm),
        in_specs=in_specs,
        out_specs=([pl.BlockSpec((1, tm, wd), lambda bi, i: (bi, i, 0)) for wd in out_widths]
                   + [pl.BlockSpec((1, tm // LANES, vd, LANES), lambda bi, i: (bi, i, 0, 0)) for vd in vt_dims]),
        out_shape=([jax.ShapeDtypeStruct((b, s, wd), BF16) for wd in out_widths]
                   + [jax.ShapeDtypeStruct((b, s // LANES, vd, LANES), BF16) for vd in vt_dims]),
        compiler_params=pltpu.CompilerParams(dimension_semantics=("parallel", "arbitrary"),
                                             vmem_limit_bytes=VMEM_LIMIT),
        name="in_proj_rope" if use_rope else "in_proj_ctx",
    )(*args)


def _scores_t(qs, k_loc, k_ctx, add_loc):
    return _dot_nt(k_loc, qs) + add_loc, _dot_nt(k_ctx, qs)


def _softmax_t(s_loc, s_ctx, extra_logit):
    m = jnp.maximum(jnp.max(s_loc, axis=0, keepdims=True), jnp.max(s_ctx, axis=0, keepdims=True))
    if extra_logit is not None:
        m = jnp.maximum(m, extra_logit)
    return jnp.exp2((s_loc - m).astype(BF16)), jnp.exp2((s_ctx - m).astype(BF16)), m


def _pv_t(p_loc, p_ctx, m, vt_loc, vt_ctx, extra_logit):
    n_v = vt_loc.shape[0]
    with_ones = lambda vt: jnp.concatenate([vt, jnp.ones((BF16_ROWS, vt.shape[1]), BF16)], axis=0)
    acc = _dot(with_ones(vt_loc), p_loc) + _dot(with_ones(vt_ctx), p_ctx)
    l = acc[n_v:n_v + 1]
    if extra_logit is not None:
        l = l + jnp.exp2(extra_logit - m)
    return acc[:n_v] * (1.0 / l)


def _group_norm_t(o_t, g_t):
    ms = jnp.mean(o_t * o_t, axis=0, keepdims=True)
    return ((o_t * lax.rsqrt(ms + EPS)) * g_t).T.astype(BF16)


def _lane_tiles(ref, first, n, rows=slice(None)):
    return jnp.concatenate([ref[0, first + j, rows, :] for j in range(n)], axis=1)


def _attn_kernel(sink_ref, qa_ref, qb_ref, ka_ref, vat_ref, kb_ref, vbt_ref, kac_ref, vact_ref,
                 kbc_ref, vbct_ref, ga_ref, gb_ref, bias_ref, *rest, seq, n_sub, n_cast):
    o_ref = rest[n_cast]
    for src_ref, dst_ref in zip(rest[:n_cast], rest[n_cast + 1:]):
        dst_ref[...] = src_ref[0].astype(BF16)
    groups = []
    for j in range(n_sub):
        groups += _attn_block_groups(
            pl.program_id(1) * n_sub + j, slice(j * QBLK, (j + 1) * QBLK), sink_ref, qa_ref, qb_ref, ka_ref,
            vat_ref, kb_ref, vbt_ref, kac_ref, vact_ref, kbc_ref, vbct_ref, ga_ref, gb_ref, bias_ref, o_ref, seq)
    scored, soft = [], []
    for scores, finish in groups:
        scored.append(finish(*scores()))
        if len(scored) > ATTN_LOOKAHEAD:
            gen = scored.pop(0)
            next(gen)
            soft.append(gen)
        if len(soft) > ATTN_PV_LAG:
            next(soft.pop(0), None)
    for gen in scored:
        next(gen)
        soft.append(gen)
    for gen in soft:
        next(gen, None)


def _attn_block_groups(i, qrows, sink_ref, qa_ref, qb_ref, ka_ref, vat_ref, kb_ref, vbt_ref, kac_ref, vact_ref,
                       kbc_ref, vbct_ref, ga_ref, gb_ref, bias_ref, o_ref, seq):
    rows = seq // GRID_W
    stages = []
    n_ctx_tiles = kac_ref.shape[1] // LANES
    lane = lax.broadcasted_iota(jnp.int32, (QBLK, LANES), 1)
    lo = lane < HEAD_DIM
    half_masks = (lo.astype(F32).astype(BF16), jnp.logical_not(lo).astype(F32).astype(BF16))

    group = A_HEADS // A_KV_HEADS
    blk_a = jnp.clip(i - 1, 0, seq // QBLK - A_SPAN // QBLK)
    start_a = pl.multiple_of(blk_a * QBLK, QBLK)
    d = (lax.broadcasted_iota(jnp.int32, (A_SPAN, QBLK), 0)
         - lax.broadcasted_iota(jnp.int32, (A_SPAN, QBLK), 1) + (start_a - i * QBLK))
    amask = jnp.where((d <= A_WINDOW) & (d >= -A_WINDOW), 0.0, NEG).astype(F32)
    amask = jnp.concatenate([amask] * A_STACK, axis=1)
    k_loc = ka_ref[0, pl.ds(start_a, A_SPAN), :]
    k_ctx = kac_ref[0]
    q_tiles = [qa_ref[0, qrows, t * LANES:(t + 1) * LANES] for t in range(group)]
    heads_a = []

    def scores_a(kv, t0):
        qs = jnp.concatenate([q * half_masks[kv] for q in q_tiles[t0:t0 + A_STACK]], axis=0)
        return _scores_t(qs, k_loc, k_ctx, amask)

    def finish_a(kv, t0, s_loc, s_ctx):
        dims = slice(kv * HEAD_DIM, (kv + 1) * HEAD_DIM)
        sink = jnp.concatenate(
            [jnp.full((1, QBLK), sink_ref[kv * group + t0 + t] * LOG2E, F32) for t in range(A_STACK)], axis=1)
        p_loc, p_ctx, m = _softmax_t(s_loc, s_ctx, sink)
        yield
        o_t = _pv_t(p_loc, p_ctx, m, _lane_tiles(vat_ref, blk_a, A_SPAN // LANES, dims),
                    _lane_tiles(vact_ref, 0, n_ctx_tiles, dims), sink)
        heads_a.extend(o_t[:, t * QBLK:(t + 1) * QBLK] for t in range(A_STACK))
        if len(heads_a) == A_HEADS:
            o_ref[0, qrows, :A_WIDTH] = _group_norm_t(jnp.concatenate(heads_a, axis=0), ga_ref[...])

    stages += [(functools.partial(scores_a, kv, t0), functools.partial(finish_a, kv, t0))
               for kv in range(A_KV_HEADS) for t0 in range(0, group, A_STACK)]

    blk_b = jnp.clip(NA_ROW_BLOCK * i - NA_KH // 2, 0, rows - B_KROWS) // NA_ROW_BLOCK
    start_b = pl.multiple_of(blk_b * QBLK, QBLK)
    heads_b = []
    n_blk = seq // QBLK
    pattern = jnp.where(i < 2, i, jnp.where(i >= n_blk - 2, i - (n_blk - N_PATTERNS), 2))

    def scores_b(pair):
        cols = slice(pair * LANES, (pair + 1) * LANES)
        qp = qb_ref[0, qrows, cols]
        qs = jnp.concatenate([qp * half_masks[0], qp * half_masks[1]], axis=0)
        return _scores_t(qs, kb_ref[0, pl.ds(start_b, B_SPAN), cols], kbc_ref[0, :, cols], bias_ref[pattern, pair])

    def finish_b(pair, s_loc, s_ctx):
        cols = slice(pair * LANES, (pair + 1) * LANES)
        p_loc, p_ctx, m = _softmax_t(s_loc, s_ctx, None)
        yield
        o_t = _pv_t(p_loc, p_ctx, m, _lane_tiles(vbt_ref, blk_b, B_SPAN // LANES, cols),
                    _lane_tiles(vbct_ref, 0, n_ctx_tiles, cols), None)
        heads_b.extend([o_t[:HEAD_DIM, :QBLK], o_t[HEAD_DIM:, QBLK:]])
        if pair == B_HEADS // 2 - 1:
            o_ref[0, qrows, A_WIDTH:] = _group_norm_t(jnp.concatenate(heads_b, axis=0), gb_ref[...])

    stages += [(functools.partial(scores_b, p), functools.partial(finish_b, p)) for p in range(B_HEADS // 2)]
    return stages


def _attn_call(sink, qa, qb, ka, vat, kb, vbt, kac, vact, kbc, vbct, bias, ga_t, gb_t, cast_weights):
    b, s, _ = qa.shape
    n_blk = s // QBLK
    assert (NA_KH // 2) % NA_ROW_BLOCK == 0 and (s // GRID_W - B_KROWS) % NA_ROW_BLOCK == 0

    n_sub = ATTN_BLOCKS_PER_STEP
    steps_per_batch = n_blk // n_sub
    full = lambda arr: pl.BlockSpec((1,) + arr.shape[1:], lambda bi, i: (bi,) + (0,) * (arr.ndim - 1))
    const = lambda arr, **kw: pl.BlockSpec(arr.shape, lambda bi, i: (0,) * arr.ndim, **kw)
    blk = lambda wd: pl.BlockSpec((1, n_sub * QBLK, wd), lambda bi, i: (bi, i, 0))
    n_steps = b * steps_per_batch
    in_slabs, out_slabs = [], []
    for w in cast_weights:
        _, rows, cols = w.shape
        c = next(c for c in range(1, n_steps + 1)
                 if n_steps % c == 0 and cols % (c * LANES) == 0 and rows % (n_steps // c * BF16_ROWS) == 0)
        r = n_steps // c
        step = lambda bi, i: bi * steps_per_batch + i
        in_slabs.append(pl.BlockSpec((1, rows // r, cols // c),
                                     lambda bi, i, c=c: (0, step(bi, i) // c, step(bi, i) % c)))
        out_slabs.append(pl.BlockSpec((rows // r, cols // c), lambda bi, i, c=c: (step(bi, i) // c, step(bi, i) % c)))
    outs = pl.pallas_call(
        functools.partial(_attn_kernel, seq=s, n_sub=n_sub, n_cast=len(cast_weights)),
        grid=(b, steps_per_batch),
        in_specs=[
            pl.BlockSpec(memory_space=pltpu.SMEM),
            blk(A_WIDTH), blk(B_WIDTH),
            full(ka), full(vat), full(kb), full(vbt), full(kac), full(vact), full(kbc), full(vbct),
            const(ga_t), const(gb_t), const(bias, pipeline_mode=pl.Buffered(1)),
        ] + in_slabs,
        out_specs=[blk(A_WIDTH + B_WIDTH)] + out_slabs,
        out_shape=([jax.ShapeDtypeStruct((b, s, A_WIDTH + B_WIDTH), BF16)]
                   + [jax.ShapeDtypeStruct(w.shape[1:], BF16) for w in cast_weights]),
        compiler_params=pltpu.CompilerParams(dimension_semantics=("parallel", "arbitrary"),
                                             vmem_limit_bytes=VMEM_LIMIT),
        name="attn",
    )(sink, qa, qb, ka, vat, kb, vbt, kac, vact, kbc, vbct, ga_t, gb_t, bias, *cast_weights)
    return outs[0], outs[1:]


def _ffn_kernel(x_ref, o_ref, g1_ref, sh2_ref, sc2_ref, g2_ref, n2_ref, wo_ref, wg_ref, wu_ref, wd_ref,
                out_ref):
    geff = n2_ref[...] * (1.0 + sc2_ref[0])
    tm = x_ref.shape[1]
    sub = min(tm, FFN_SUB_ROWS)

    def stage_rows(r0):
        rows = slice(r0, r0 + sub)
        y = _dot(o_ref[0, rows, :], wo_ref[...])
        yield
        x1 = x_ref[0, rows, :] + g1_ref[0] * y
        ms = jnp.mean(x1 * x1, axis=-1, keepdims=True)
        h2 = ((x1 * lax.rsqrt(ms + EPS)) * geff + sh2_ref[0]).astype(BF16)
        gate = _dot(h2, wg_ref[...])
        up = _dot(h2, wu_ref[...])
        yield
        act = (_silu(gate) * up).astype(BF16)
        down = _dot(act, wd_ref[...])
        yield
        out_ref[0, rows, :] = x1 + g2_ref[0] * down

    live = [stage_rows(r0) for r0 in range(0, tm, sub)]
    while live:
        for gen in list(live):
            if next(gen, StopIteration) is StopIteration:
                live.remove(gen)


def _ffn_call(x, o, mod3, n2, wo, wg, wu, wd, tm):
    b, s, d = x.shape
    hid = wg.shape[1]
    modspec = lambda j: pl.BlockSpec((1, 1, d), lambda bi, i: (bi, 0, j))
    const = lambda shape: pl.BlockSpec(shape, lambda bi, i: (0, 0), pipeline_mode=pl.Buffered(1))
    return pl.pallas_call(
        _ffn_kernel,
        grid=(b, s // tm),
        in_specs=[
            pl.BlockSpec((1, tm, d), lambda bi, i: (bi, i, 0)),
            pl.BlockSpec((1, tm, o.shape[2]), lambda bi, i: (bi, i, 0)),
            modspec(2), modspec(3), modspec(4), modspec(5),
            pl.BlockSpec((1, d), lambda bi, i: (0, 0)),
            const((o.shape[2], d)), const((d, hid)), const((d, hid)), const((hid, d)),
        ],
        out_specs=pl.BlockSpec((1, tm, d), lambda bi, i: (bi, i, 0)),
        out_shape=jax.ShapeDtypeStruct((b, s, d), F32),
        compiler_params=pltpu.CompilerParams(dimension_semantics=("parallel", "arbitrary"),
                                             vmem_limit_bytes=VMEM_LIMIT),
        name="out_ffn",
    )(x, o, mod3, mod3, mod3, mod3, n2, wo, wg, wu, wd)


def _layer(x, ctx, mod, norm1_g, w_in, qn_a, kn_a, sink_a, qn_b, kn_b, rpb_b, on_a, on_b, w_out,
           norm2_g, w_gate, w_up, w_down, rope_tabs):
    b, s, d = x.shape
    rows = s // GRID_W
    w_l, w_c, wva_t, wvb_t = w_in
    scale = HEAD_DIM ** -0.5 * LOG2E
    gain_l = jnp.concatenate([jnp.tile(qn_a, A_HEADS) * scale, jnp.tile(kn_a, A_KV_HEADS),
                              jnp.tile(qn_b, B_HEADS) * scale, jnp.tile(kn_b, B_HEADS)])[None]
    gain_c = jnp.concatenate([jnp.tile(kn_a, A_KV_HEADS), jnp.tile(kn_b, B_HEADS)])[None]
    ones_bd = jnp.asarray(np.kron(np.eye(MXU_TILE // HEAD_DIM), np.ones((HEAD_DIM, HEAD_DIM))), BF16)

    mod3 = mod[:b].reshape(b, 1, -1)
    mod3_c = mod[b:b + 1].reshape(1, 1, -1)
    g1 = norm1_g[None]

    qka = A_WIDTH + A_KV_WIDTH
    groups_l = [
        (0, qka, [(0, A_WIDTH, True, 0), (A_WIDTH, A_KV_WIDTH, True, 1)]),
        (qka, B_WIDTH, [(0, B_WIDTH, False, 2)]),
        (qka + B_WIDTH, B_WIDTH, [(0, B_WIDTH, False, 3)]),
    ]
    qa, ka, qb, kb, vat, vbt = _inproj_call(
        x, mod3, True, g1, w_l, gain_l, ones_bd, wva_t, wvb_t, rope_tabs, groups_l,
        [A_WIDTH, A_KV_WIDTH, B_WIDTH, B_WIDTH], tm=INPROJ_ROWS)
    groups_c = [(0, A_KV_WIDTH + B_WIDTH, [(0, A_KV_WIDTH, False, 0), (A_KV_WIDTH, B_WIDTH, False, 1)])]
    n_ctx = ctx.shape[1]
    ctx_out = _inproj_call(
        ctx.reshape(1, b * n_ctx, d), mod3_c, False, g1, w_c, gain_c, ones_bd, wva_t, wvb_t, None, groups_c,
        [A_KV_WIDTH, B_WIDTH], tm=CTX_ROWS)
    kac, kbc = (a.reshape(b, n_ctx, -1) for a in ctx_out[:2])
    vact, vbct = (a.reshape(b, n_ctx // LANES, -1, LANES) for a in ctx_out[2:])

    bias = _bias_call(rpb_b, rows)
    lanes_of = lambda g: jnp.broadcast_to(g[:, None], (g.shape[0], LANES))
    o, ffn_weights = _attn_call(sink_a, qa, qb, ka, vat, kb, vbt, kac, vact, kbc, vbct, bias, lanes_of(on_a),
                                lanes_of(on_b), [w_out, w_gate, w_up, w_down])
    return _ffn_call(x, o, mod3, norm2_g[None], *ffn_weights, tm=FFN_ROWS)


def kernel(x, c, ctx, c_ctx, w_mod, b_mod, norm1_g, w_in, qn_a, kn_a, sink_a, qn_b, kn_b, rpb_b, on_a, on_b,
           w_out, norm2_g, w_gate, w_up, w_down):
    b, s, d = x.shape
    depth = w_mod.shape[0]
    assert depth == 1, "the context stream update is only needed when a later layer reads it"
    rope_tabs = _rope_call(s)
    n_rows = -(-(b + 1) // BF16_ROWS) * BF16_ROWS
    cc = jnp.concatenate([c, c_ctx[None], jnp.zeros((n_rows - b - 1, d), F32)], axis=0)
    l = 0
    mod, *w_in_layouts = _mod_call(cc, w_mod[l], b_mod[l][None], w_in[l])
    return _layer(x, ctx, mod, norm1_g[l], w_in_layouts, qn_a[l], kn_a[l], sink_a[l], qn_b[l], kn_b[l], rpb_b[l],
                  on_a[l], on_b[l], w_out[l:l + 1], norm2_g[l], w_gate[l:l + 1], w_up[l:l + 1], w_down[l:l + 1],
                  rope_tabs)
```

```python
import functools

import numpy as np
import jax
import jax.numpy as jnp
from jax import lax
from jax.experimental import pallas as pl
from jax.experimental.pallas import tpu as pltpu

GRID_W = 64
HEAD_DIM = 64
A_HEADS = 8
A_KV_HEADS = 2
A_WINDOW = 128
B_HEADS = 8
NA_KH = 8
NA_KW = 16
NA_ROW_BLOCK = 2
ROPE_BASE = 10000.0
EPS = 1e-6

A_WIDTH = A_HEADS * HEAD_DIM
A_KV_WIDTH = A_KV_HEADS * HEAD_DIM
B_WIDTH = B_HEADS * HEAD_DIM
QBLK = NA_ROW_BLOCK * GRID_W
A_SPAN = QBLK + 2 * A_WINDOW
B_KROWS = 10
B_SPAN = B_KROWS * GRID_W
N_PATTERNS = 5
ATTN_BLOCKS_PER_STEP = 8
MOD_COLS = 1536
INPROJ_ROWS = 2048
CTX_ROWS = 512
INPROJ_SUB_ROWS = 512
FFN_ROWS = 1024
FFN_SUB_ROWS = 256
A_STACK = 4
ATTN_PV_LAG = 1
ATTN_LOOKAHEAD = 1
NEG = -1e30
LOG2E = 1.4426950408889634

LANES = 128
BF16_ROWS = 16
MXU_TILE = 256
VMEM_LIMIT = 56 * 1024 * 1024

F32 = jnp.float32
BF16 = jnp.bfloat16


def _dot(a, b):
    return jnp.dot(a, b, preferred_element_type=F32)


def _dot_nt(a, b):
    return lax.dot_general(a, b, (((1,), (1,)), ((), ())), preferred_element_type=F32)


def _silu(x):
    return x / (1.0 + jnp.exp(-x))


def _mod_kernel(c_ref, cctx_ref, w_ref, b_ref, win_ref, o_ref, wl_ref, wc_ref, wvat_ref, wvbt_ref):
    cond = jnp.concatenate([c_ref[...], jnp.broadcast_to(cctx_ref[...], c_ref.shape)], axis=0)
    split = lambda v: (v.astype(BF16), (v - v.astype(BF16).astype(F32)).astype(BF16))
    a_hi, a_lo = split(_silu(cond))
    w_hi, w_lo = split(w_ref[...])
    rows = a_hi.shape[0]
    head = _dot(jnp.concatenate([a_hi, a_lo], axis=0), w_hi)
    res = head[:rows] + head[rows:] + _dot(a_hi, w_lo) + b_ref[...]
    for r in range(rows):
        o_ref[r] = res[r:r + 1]
    _wprep_slab(win_ref, wl_ref, wc_ref, wvat_ref, wvbt_ref)


def _mod_call(c, c_ctx, w_mod, b_mod, w_in):
    b, d = c.shape
    assert b % 8 == 0
    rows = 2 * b
    n = w_mod.shape[1]
    bn = MOD_COLS
    steps = n // bn
    d_in, n_in = w_in.shape
    slab = d_in // steps
    assert (A_HEADS // A_KV_HEADS) % 2 == 0 and n_in == A_WIDTH + 2 * A_KV_WIDTH + 3 * B_WIDTH
    assert d_in % steps == 0 and slab % LANES == 0
    row_slab = lambda cols: pl.BlockSpec((slab, cols), lambda j: (j, 0))
    col_slab = lambda r: pl.BlockSpec((r, slab), lambda j: (0, j))
    n_l, n_c = A_WIDTH + A_KV_WIDTH + 2 * B_WIDTH, A_KV_WIDTH + B_WIDTH
    return pl.pallas_call(
        _mod_kernel,
        grid=(steps,),
        in_specs=[pl.BlockSpec((b, d), lambda j: (0, 0)),
                  pl.BlockSpec((1, d), lambda j: (0, 0)),
                  pl.BlockSpec((d, bn), lambda j: (0, j)),
                  pl.BlockSpec((1, bn), lambda j: (0, j)),
                  row_slab(n_in)],
        out_specs=[pl.BlockSpec((rows, 1, bn), lambda j: (0, 0, j)),
                   row_slab(n_l), row_slab(n_c), col_slab(A_KV_WIDTH), col_slab(B_WIDTH)],
        out_shape=[jax.ShapeDtypeStruct((rows, 1, n), F32),
                   jax.ShapeDtypeStruct((d_in, n_l), BF16), jax.ShapeDtypeStruct((d_in, n_c), BF16),
                   jax.ShapeDtypeStruct((A_KV_WIDTH, d_in), BF16), jax.ShapeDtypeStruct((B_WIDTH, d_in), BF16)],
        compiler_params=pltpu.CompilerParams(vmem_limit_bytes=VMEM_LIMIT),
        name="mod_w_in_layout",
    )(c, c_ctx, w_mod, b_mod, w_in)


def _wprep_slab(w_ref, wl_ref, wc_ref, wvat_ref, wvbt_ref):
    group = A_HEADS // A_KV_HEADS
    c_ka, c_va, c_qb, c_kb, c_vb = np.cumsum([A_WIDTH, A_KV_WIDTH, A_KV_WIDTH, B_WIDTH, B_WIDTH])
    lane = lax.broadcasted_iota(jnp.int32, (w_ref.shape[0], LANES), 1)
    lo = lane < HEAD_DIM
    for t in range(group):
        src = [w_ref[:, (h // 2) * LANES:(h // 2 + 1) * LANES] for h in (t, t + group)]
        if t % 2 == 0:
            tile = jnp.where(lo, src[0], pltpu.roll(src[1], HEAD_DIM, 1))
        else:
            tile = jnp.where(lo, pltpu.roll(src[0], HEAD_DIM, 1), src[1])
        wl_ref[:, t * LANES:(t + 1) * LANES] = tile.astype(BF16)
    k_a = w_ref[:, c_ka:c_va].astype(BF16)
    k_b = w_ref[:, c_kb:c_vb].astype(BF16)
    wl_ref[:, A_WIDTH:A_WIDTH + A_KV_WIDTH] = k_a
    wl_ref[:, A_WIDTH + A_KV_WIDTH:A_WIDTH + A_KV_WIDTH + B_WIDTH] = w_ref[:, c_qb:c_kb].astype(BF16)
    wl_ref[:, A_WIDTH + A_KV_WIDTH + B_WIDTH:] = k_b
    wc_ref[:, :A_KV_WIDTH] = k_a
    wc_ref[:, A_KV_WIDTH:] = k_b
    wvat_ref[...] = w_ref[:, c_va:c_qb].T.astype(BF16)
    for c in range(0, B_WIDTH, LANES):
        wvbt_ref[c:c + LANES, :] = w_ref[:, c_vb + c:c_vb + c + LANES].T.astype(BF16)


def _rope_kernel(invf_ref, cos_ref, sin_ref, *, rows):
    n_pos = max(rows, GRID_W)
    pos = lax.broadcasted_iota(jnp.int32, (n_pos, LANES), 0).astype(F32)
    lane = lax.broadcasted_iota(jnp.int32, (GRID_W, LANES), 1)
    row_lanes = (lane & (HEAD_DIM - 1)) < HEAD_DIM // 2
    first = (lane & (HEAD_DIM // 2 - 1)) < HEAD_DIM // 4
    ang = pos * invf_ref[...]
    cos_p, sin_p = jnp.cos(ang), jnp.sin(ang)
    cos_c = cos_p[:GRID_W]
    sin_c = jnp.where(first, -sin_p[:GRID_W], sin_p[:GRID_W])
    for r in range(rows):
        tok = slice(r * GRID_W, (r + 1) * GRID_W)
        cos_ref[tok, :] = jnp.where(row_lanes, cos_p[r:r + 1], cos_c)
        sin_ref[tok, :] = jnp.where(row_lanes, jnp.where(first, -sin_p[r:r + 1], sin_p[r:r + 1]), sin_c)


def _rope_call(seq):
    quarter = HEAD_DIM // 4
    inv = (1.0 / (np.float32(ROPE_BASE) ** (np.arange(quarter, dtype=np.float32) / quarter))).astype(np.float32)
    invf = jnp.asarray(np.tile(inv, LANES // quarter)[None, :])
    return pl.pallas_call(
        functools.partial(_rope_kernel, rows=seq // GRID_W),
        grid=(1,),
        in_specs=[pl.BlockSpec((1, LANES), lambda i: (0, 0))],
        out_specs=[pl.BlockSpec((seq, LANES), lambda i: (0, 0))] * 2,
        out_shape=[jax.ShapeDtypeStruct((seq, LANES), F32)] * 2,
        name="rope_tables",
    )(invf)


def _b_patterns(rows):
    n_blk = rows // NA_ROW_BLOCK
    blocks = [0, 1, 2, n_blk - 2, n_blk - 1]
    return [(NA_ROW_BLOCK * i, _b_key_start(i, rows)) for i in blocks]


def _b_key_start(i, rows):
    return min(max(NA_ROW_BLOCK * i - NA_KH // 2, 0), rows - B_KROWS)


def _bias_kernel(rpb_ref, o_ref, *, patterns, rows):
    h = pl.program_id(0)
    n_dr, n_dc = 2 * NA_KH - 1, 2 * NA_KW - 1
    kc = lax.broadcasted_iota(jnp.int32, (GRID_W, LANES), 0)
    lane = lax.broadcasted_iota(jnp.int32, (GRID_W, LANES), 1)
    qc = lane & (GRID_W - 1)
    dc = jnp.clip(kc - qc + NA_KW - 1, 0, n_dc - 1)
    cs = jnp.clip(qc - NA_KW // 2, 0, GRID_W - NA_KW)
    col_ok = (kc >= cs) & (kc < cs + NA_KW)
    lo = lane < GRID_W
    base = h * (n_dr * n_dc)
    per_dr = []
    for dr in range(n_dr):
        m = jnp.zeros((GRID_W, LANES), F32)
        for d in range(n_dc):
            m = jnp.where(dc == d, rpb_ref[base + dr * n_dc + d] * LOG2E, m)
        per_dr.append(m)
    neg = jnp.full((GRID_W, LANES), NEG, F32)
    for p, (r0, ks) in enumerate(patterns):
        q_rows = [r0 + qr for qr in range(NA_ROW_BLOCK)]
        rs = [min(max(q - NA_KH // 2, 0), rows - NA_KH) for q in q_rows]
        for kr in range(B_KROWS):
            k_row = ks + kr
            ok = [r <= k_row < r + NA_KH for r in rs]
            drs = [min(max(k_row - q + NA_KH - 1, 0), n_dr - 1) for q in q_rows]
            if not any(ok):
                piece = neg
            else:
                vals = jnp.where(lo, per_dr[drs[0]], per_dr[drs[1]])
                mask = col_ok
                if not ok[1]:
                    mask = mask & lo
                if not ok[0]:
                    mask = mask & jnp.logical_not(lo)
                piece = jnp.where(mask, vals, neg)
            o_ref[p, 0, kr * GRID_W:(kr + 1) * GRID_W, :] = piece


def _bias_call(rpb, rows):
    patterns = _b_patterns(rows)
    heads = rpb.shape[0]
    return pl.pallas_call(
        functools.partial(_bias_kernel, patterns=patterns, rows=rows),
        grid=(heads,),
        in_specs=[pl.BlockSpec(memory_space=pltpu.SMEM)],
        out_specs=pl.BlockSpec((N_PATTERNS, 1, B_SPAN, QBLK), lambda h: (0, h // 2, 0, h % 2)),
        out_shape=jax.ShapeDtypeStruct((N_PATTERNS, heads // 2, B_SPAN, 2 * QBLK), F32),
        name="na_bias",
    )(rpb.reshape(-1))


def _group_rms(p, ones_ref):
    sq = (p * p).astype(BF16)
    n = p.shape[1]
    if n <= MXU_TILE:
        ssq = _dot(sq, ones_ref[:n, :n])
    else:
        ssq = jnp.concatenate(
            [_dot(sq[:, c:c + MXU_TILE], ones_ref[...]) for c in range(0, n, MXU_TILE)], axis=1)
    return lax.rsqrt(ssq * (1.0 / HEAD_DIM) + EPS)


def _rope(x, cos, sin):
    lane = lax.broadcasted_iota(jnp.int32, (x.shape[0], LANES), 1)
    first = (lane & (HEAD_DIM // 2 - 1)) < HEAD_DIM // 4
    outs = []
    for c in range(0, x.shape[1], LANES):
        xt = x[:, c:c + LANES]
        sw = jnp.where(first, pltpu.roll(xt, LANES - HEAD_DIM // 4, 1), pltpu.roll(xt, HEAD_DIM // 4, 1))
        outs.append(xt * cos + sw * sin)
    return outs[0] if len(outs) == 1 else jnp.concatenate(outs, axis=1)


def _inproj_kernel(x_ref, sh_ref, sc_ref, g_ref, w_ref, gain_ref, ones_ref, wva_ref, wvb_ref, *rest,
                   groups, use_rope):
    if use_rope:
        cos, sin = rest[0][...], rest[1][...]
        rest = rest[2:]
    n_std = sum(len(subs) for _, _, subs in groups)
    out_refs, vt_refs = rest[:n_std], rest[n_std:]
    geff = g_ref[...] * (1.0 + sc_ref[0])
    shift = sh_ref[0]
    tm = x_ref.shape[1]
    sub = min(tm, INPROJ_SUB_ROWS)
    h_cache = {}

    def h_of(r0):
        if r0 not in h_cache:
            x = x_ref[0, r0:r0 + sub, :]
            ms = jnp.mean(x * x, axis=-1, keepdims=True)
            h_cache[r0] = ((x * lax.rsqrt(ms + EPS)) * geff + shift).astype(BF16)
        return h_cache[r0]

    def project(r0, c0, width, subs):
        return _dot(h_of(r0), w_ref[:, c0:c0 + width])

    def finish(p, r0, c0, width, subs):
        for s0, sw, rope, oi, gr in subs:
            y = p[:, s0:s0 + sw]
            y = y * _group_rms(y, ones_ref) * jnp.tile(gain_ref[gr:gr + 1, :], (1, sw // LANES))
            if rope:
                y = _rope(y, cos[r0:r0 + sub], sin[r0:r0 + sub])
            out_refs[oi][0, r0:r0 + sub, :] = y.astype(BF16)

    def project_vt(r0, wv_ref, vt_ref):
        return _dot_nt(wv_ref[...], h_of(r0))

    def finish_vt(vt, r0, wv_ref, vt_ref):
        for j in range(sub // LANES):
            vt_ref[0, r0 // LANES + j] = vt[:, j * LANES:(j + 1) * LANES].astype(BF16)

    stages = []
    for r0 in range(0, tm, sub):
        stages += [(project, finish, (r0,) + g) for g in groups]
        stages += [(project_vt, finish_vt, (r0,) + a) for a in zip((wva_ref, wvb_ref), vt_refs)]
    pending = None
    for first, second, args in stages:
        res = first(*args)
        if pending is not None:
            pending[0](pending[1], *pending[2])
        pending = (second, res, args)
    pending[0](pending[1], *pending[2])


def _inproj_call(x, mod3, mod_row, g, w, gain, ones, wva_t, wvb_t, rope_tabs, groups, out_widths, tm):
    b, s, d = x.shape
    n = w.shape[1]
    use_rope = rope_tabs is not None
    mod_idx = (lambda bi, i, j: (bi, 0, j)) if mod_row is None else (lambda bi, i, j: (mod_row, 0, j))
    const = lambda arr: pl.BlockSpec(arr.shape, lambda bi, i: (0, 0))
    in_specs = [
        pl.BlockSpec((1, tm, d), lambda bi, i: (bi, i, 0)),
        pl.BlockSpec((1, 1, d), lambda bi, i: mod_idx(bi, i, 0)),
        pl.BlockSpec((1, 1, d), lambda bi, i: mod_idx(bi, i, 1)),
        const(g), const(w), const(gain), const(ones), const(wva_t), const(wvb_t),
    ]
    args = [x, mod3, mod3, g, w, gain, ones, wva_t, wvb_t]
    if use_rope:
        in_specs += [pl.BlockSpec((tm, LANES), lambda bi, i: (i, 0))] * 2
        args += list(rope_tabs)
    vt_dims = [wva_t.shape[0], wvb_t.shape[0]]
    return pl.pallas_call(
        functools.partial(_inproj_kernel, groups=groups, use_rope=use_rope),
        grid=(b, s // tm),
        in_specs=in_specs,
        out_specs=([pl.BlockSpec((1, tm, wd), lambda bi, i: (bi, i, 0)) for wd in out_widths]
                   + [pl.BlockSpec((1, tm // LANES, vd, LANES), lambda bi, i: (bi, i, 0, 0)) for vd in vt_dims]),
        out_shape=([jax.ShapeDtypeStruct((b, s, wd), BF16) for wd in out_widths]
                   + [jax.ShapeDtypeStruct((b, s // LANES, vd, LANES), BF16) for vd in vt_dims]),
        compiler_params=pltpu.CompilerParams(dimension_semantics=("parallel", "arbitrary"),
                                             vmem_limit_bytes=VMEM_LIMIT),
        name="in_proj_rope" if use_rope else "in_proj_ctx",
    )(*args)


def _scores_t(qs, k_loc, k_ctx, add_loc):
    return _dot_nt(k_loc, qs) + add_loc, _dot_nt(k_ctx, qs)


def _softmax_t(s_loc, s_ctx, extra_logit):
    m = jnp.maximum(jnp.max(s_loc, axis=0, keepdims=True), jnp.max(s_ctx, axis=0, keepdims=True))
    if extra_logit is not None:
        m = jnp.maximum(m, extra_logit)
    return jnp.exp2((s_loc - m).astype(BF16)), jnp.exp2((s_ctx - m).astype(BF16)), m


def _pv_t(p_loc, p_ctx, m, vt_loc, vt_ctx, extra_logit):
    n_v = vt_loc.shape[0]
    with_ones = lambda vt: jnp.concatenate([vt, jnp.ones((BF16_ROWS, vt.shape[1]), BF16)], axis=0)
    acc = _dot(with_ones(vt_loc), p_loc) + _dot(with_ones(vt_ctx), p_ctx)
    l = acc[n_v:n_v + 1]
    if extra_logit is not None:
        l = l + jnp.exp2(extra_logit - m)
    return acc[:n_v] * (1.0 / l)


def _group_norm_t(o_t, g_row):
    ms = jnp.mean(o_t * o_t, axis=0, keepdims=True)
    return ((o_t * lax.rsqrt(ms + EPS)).T * g_row).astype(BF16)


def _lane_tiles(ref, first, n, rows=slice(None)):
    return jnp.concatenate([ref[0, first + j, rows, :] for j in range(n)], axis=1)


def _attn_kernel(sink_ref, qa_ref, qb_ref, ka_ref, vat_ref, kb_ref, vbt_ref, kac_ref, vact_ref,
                 kbc_ref, vbct_ref, ga_ref, gb_ref, bias_ref, *rest, seq, n_sub, n_cast):
    o_ref = rest[n_cast]
    for src_ref, dst_ref in zip(rest[:n_cast], rest[n_cast + 1:]):
        dst_ref[...] = src_ref[0].astype(BF16)
    groups = []
    for j in range(n_sub):
        groups += _attn_block_groups(
            pl.program_id(1) * n_sub + j, slice(j * QBLK, (j + 1) * QBLK), sink_ref, qa_ref, qb_ref, ka_ref,
            vat_ref, kb_ref, vbt_ref, kac_ref, vact_ref, kbc_ref, vbct_ref, ga_ref, gb_ref, bias_ref, o_ref, seq)
    scored, soft = [], []
    for scores, finish in groups:
        scored.append(finish(*scores()))
        if len(scored) > ATTN_LOOKAHEAD:
            gen = scored.pop(0)
            next(gen)
            soft.append(gen)
        if len(soft) > ATTN_PV_LAG:
            next(soft.pop(0), None)
    for gen in scored:
        next(gen)
        soft.append(gen)
    for gen in soft:
        next(gen, None)


def _attn_block_groups(i, qrows, sink_ref, qa_ref, qb_ref, ka_ref, vat_ref, kb_ref, vbt_ref, kac_ref, vact_ref,
                       kbc_ref, vbct_ref, ga_ref, gb_ref, bias_ref, o_ref, seq):
    rows = seq // GRID_W
    stages = []
    n_ctx_tiles = kac_ref.shape[1] // LANES
    lane = lax.broadcasted_iota(jnp.int32, (QBLK, LANES), 1)
    lo = lane < HEAD_DIM
    half_masks = (lo.astype(F32).astype(BF16), jnp.logical_not(lo).astype(F32).astype(BF16))

    group = A_HEADS // A_KV_HEADS
    blk_a = jnp.clip(i - 1, 0, seq // QBLK - A_SPAN // QBLK)
    start_a = pl.multiple_of(blk_a * QBLK, QBLK)
    d = (lax.broadcasted_iota(jnp.int32, (A_SPAN, QBLK), 0)
         - lax.broadcasted_iota(jnp.int32, (A_SPAN, QBLK), 1) + (start_a - i * QBLK))
    amask = jnp.where((d <= A_WINDOW) & (d >= -A_WINDOW), 0.0, NEG).astype(F32)
    amask = jnp.concatenate([amask] * A_STACK, axis=1)
    k_loc = ka_ref[0, pl.ds(start_a, A_SPAN), :]
    k_ctx = kac_ref[0]
    q_tiles = [qa_ref[0, qrows, t * LANES:(t + 1) * LANES] for t in range(group)]
    heads_a = []

    def scores_a(kv, t0):
        qs = jnp.concatenate([q * half_masks[kv] for q in q_tiles[t0:t0 + A_STACK]], axis=0)
        return _scores_t(qs, k_loc, k_ctx, amask)

    def finish_a(kv, t0, s_loc, s_ctx):
        dims = slice(kv * HEAD_DIM, (kv + 1) * HEAD_DIM)
        sink = jnp.concatenate(
            [jnp.full((1, QBLK), sink_ref[kv * group + t0 + t] * LOG2E, F32) for t in range(A_STACK)], axis=1)
        p_loc, p_ctx, m = _softmax_t(s_loc, s_ctx, sink)
        yield
        o_t = _pv_t(p_loc, p_ctx, m, _lane_tiles(vat_ref, blk_a, A_SPAN // LANES, dims),
                    _lane_tiles(vact_ref, 0, n_ctx_tiles, dims), sink)
        heads_a.extend(o_t[:, t * QBLK:(t + 1) * QBLK] for t in range(A_STACK))
        if len(heads_a) == A_HEADS:
            o_ref[0, qrows, :A_WIDTH] = _group_norm_t(jnp.concatenate(heads_a, axis=0), ga_ref[...])

    stages += [(functools.partial(scores_a, kv, t0), functools.partial(finish_a, kv, t0))
               for kv in range(A_KV_HEADS) for t0 in range(0, group, A_STACK)]

    blk_b = jnp.clip(NA_ROW_BLOCK * i - NA_KH // 2, 0, rows - B_KROWS) // NA_ROW_BLOCK
    start_b = pl.multiple_of(blk_b * QBLK, QBLK)
    heads_b = []
    n_blk = seq // QBLK
    pattern = jnp.where(i < 2, i, jnp.where(i >= n_blk - 2, i - (n_blk - N_PATTERNS), 2))

    def scores_b(pair):
        cols = slice(pair * LANES, (pair + 1) * LANES)
        qp = qb_ref[0, qrows, cols]
        qs = jnp.concatenate([qp * half_masks[0], qp * half_masks[1]], axis=0)
        return _scores_t(qs, kb_ref[0, pl.ds(start_b, B_SPAN), cols], kbc_ref[0, :, cols], bias_ref[pattern, pair])

    def finish_b(pair, s_loc, s_ctx):
        cols = slice(pair * LANES, (pair + 1) * LANES)
        p_loc, p_ctx, m = _softmax_t(s_loc, s_ctx, None)
        yield
        o_t = _pv_t(p_loc, p_ctx, m, _lane_tiles(vbt_ref, blk_b, B_SPAN // LANES, cols),
                    _lane_tiles(vbct_ref, 0, n_ctx_tiles, cols), None)
        heads_b.extend([o_t[:HEAD_DIM, :QBLK], o_t[HEAD_DIM:, QBLK:]])
        if pair == B_HEADS // 2 - 1:
            o_ref[0, qrows, A_WIDTH:] = _group_norm_t(jnp.concatenate(heads_b, axis=0), gb_ref[...])

    stages += [(functools.partial(scores_b, p), functools.partial(finish_b, p)) for p in range(B_HEADS // 2)]
    return stages


def _attn_call(sink, qa, qb, ka, vat, kb, vbt, kac, vact, kbc, vbct, bias, ga_t, gb_t, cast_weights):
    b, s, _ = qa.shape
    n_blk = s // QBLK
    assert (NA_KH // 2) % NA_ROW_BLOCK == 0 and (s // GRID_W - B_KROWS) % NA_ROW_BLOCK == 0

    n_sub = ATTN_BLOCKS_PER_STEP
    steps_per_batch = n_blk // n_sub
    full = lambda arr: pl.BlockSpec((1,) + arr.shape[1:], lambda bi, i: (bi,) + (0,) * (arr.ndim - 1))
    const = lambda arr, **kw: pl.BlockSpec(arr.shape, lambda bi, i: (0,) * arr.ndim, **kw)
    blk = lambda wd: pl.BlockSpec((1, n_sub * QBLK, wd), lambda bi, i: (bi, i, 0))
    n_steps = b * steps_per_batch
    in_slabs, out_slabs = [], []
    for w in cast_weights:
        _, rows, cols = w.shape
        c = next(c for c in range(1, n_steps + 1)
                 if n_steps % c == 0 and cols % (c * LANES) == 0 and rows % (n_steps // c * BF16_ROWS) == 0)
        r = n_steps // c
        step = lambda bi, i: bi * steps_per_batch + i
        in_slabs.append(pl.BlockSpec((1, rows // r, cols // c),
                                     lambda bi, i, c=c: (0, step(bi, i) // c, step(bi, i) % c)))
        out_slabs.append(pl.BlockSpec((rows // r, cols // c), lambda bi, i, c=c: (step(bi, i) // c, step(bi, i) % c)))
    outs = pl.pallas_call(
        functools.partial(_attn_kernel, seq=s, n_sub=n_sub, n_cast=len(cast_weights)),
        grid=(b, steps_per_batch),
        in_specs=[
            pl.BlockSpec(memory_space=pltpu.SMEM),
            blk(A_WIDTH), blk(B_WIDTH),
            full(ka), full(vat), full(kb), full(vbt), full(kac), full(vact), full(kbc), full(vbct),
            const(ga_t), const(gb_t), const(bias, pipeline_mode=pl.Buffered(1)),
        ] + in_slabs,
        out_specs=[blk(A_WIDTH + B_WIDTH)] + out_slabs,
        out_shape=([jax.ShapeDtypeStruct((b, s, A_WIDTH + B_WIDTH), BF16)]
                   + [jax.ShapeDtypeStruct(w.shape[1:], BF16) for w in cast_weights]),
        compiler_params=pltpu.CompilerParams(dimension_semantics=("parallel", "arbitrary"),
                                             vmem_limit_bytes=VMEM_LIMIT),
        name="attn",
    )(sink, qa, qb, ka, vat, kb, vbt, kac, vact, kbc, vbct, ga_t, gb_t, bias, *cast_weights)
    return outs[0], outs[1:]


def _ffn_kernel(x_ref, o_ref, g1_ref, sh2_ref, sc2_ref, g2_ref, n2_ref, wo_ref, wg_ref, wu_ref, wd_ref,
                out_ref):
    geff = n2_ref[...] * (1.0 + sc2_ref[0])
    tm = x_ref.shape[1]
    sub = min(tm, FFN_SUB_ROWS)

    def stage_rows(r0):
        rows = slice(r0, r0 + sub)
        y = _dot(o_ref[0, rows, :], wo_ref[...])
        yield
        x1 = x_ref[0, rows, :] + g1_ref[0] * y
        ms = jnp.mean(x1 * x1, axis=-1, keepdims=True)
        h2 = ((x1 * lax.rsqrt(ms + EPS)) * geff + sh2_ref[0]).astype(BF16)
        gate = _dot(h2, wg_ref[...])
        up = _dot(h2, wu_ref[...])
        yield
        act = (_silu(gate) * up).astype(BF16)
        down = _dot(act, wd_ref[...])
        yield
        out_ref[0, rows, :] = x1 + g2_ref[0] * down

    live = [stage_rows(r0) for r0 in range(0, tm, sub)]
    while live:
        for gen in list(live):
            if next(gen, StopIteration) is StopIteration:
                live.remove(gen)


def _ffn_call(x, o, mod3, n2, wo, wg, wu, wd, tm):
    b, s, d = x.shape
    hid = wg.shape[1]
    modspec = lambda j: pl.BlockSpec((1, 1, d), lambda bi, i: (bi, 0, j))
    const = lambda shape: pl.BlockSpec(shape, lambda bi, i: (0, 0), pipeline_mode=pl.Buffered(1))
    return pl.pallas_call(
        _ffn_kernel,
        grid=(b, s // tm),
        in_specs=[
            pl.BlockSpec((1, tm, d), lambda bi, i: (bi, i, 0)),
            pl.BlockSpec((1, tm, o.shape[2]), lambda bi, i: (bi, i, 0)),
            modspec(2), modspec(3), modspec(4), modspec(5),
            pl.BlockSpec((1, d), lambda bi, i: (0, 0)),
            const((o.shape[2], d)), const((d, hid)), const((d, hid)), const((hid, d)),
        ],
        out_specs=pl.BlockSpec((1, tm, d), lambda bi, i: (bi, i, 0)),
        out_shape=jax.ShapeDtypeStruct((b, s, d), F32),
        compiler_params=pltpu.CompilerParams(dimension_semantics=("parallel", "arbitrary"),
                                             vmem_limit_bytes=VMEM_LIMIT),
        name="out_ffn",
    )(x, o, mod3, mod3, mod3, mod3, n2, wo, wg, wu, wd)


def _layer(x, ctx, mod, norm1_g, w_in, qn_a, kn_a, sink_a, qn_b, kn_b, rpb_b, on_a, on_b, w_out,
           norm2_g, w_gate, w_up, w_down, rope_tabs):
    b, s, d = x.shape
    rows = s // GRID_W
    w_l, w_c, wva_t, wvb_t = w_in
    scale = HEAD_DIM ** -0.5 * LOG2E
    gains = jnp.tile(jnp.stack([qn_a * scale, kn_a, qn_b * scale, kn_b]), (1, LANES // HEAD_DIM))
    ones_bd = jnp.asarray(np.kron(np.eye(MXU_TILE // HEAD_DIM), np.ones((HEAD_DIM, HEAD_DIM))), BF16)
    g1 = norm1_g[None]

    qka = A_WIDTH + A_KV_WIDTH
    groups_l = [
        (0, qka, [(0, A_WIDTH, True, 0, 0), (A_WIDTH, A_KV_WIDTH, True, 1, 1)]),
        (qka, B_WIDTH, [(0, B_WIDTH, False, 2, 2)]),
        (qka + B_WIDTH, B_WIDTH, [(0, B_WIDTH, False, 3, 3)]),
    ]
    qa, ka, qb, kb, vat, vbt = _inproj_call(
        x, mod, None, g1, w_l, gains, ones_bd, wva_t, wvb_t, rope_tabs, groups_l,
        [A_WIDTH, A_KV_WIDTH, B_WIDTH, B_WIDTH], tm=INPROJ_ROWS)
    groups_c = [(0, A_KV_WIDTH + B_WIDTH, [(0, A_KV_WIDTH, False, 0, 1), (A_KV_WIDTH, B_WIDTH, False, 1, 3)])]
    n_ctx = ctx.shape[1]
    ctx_out = _inproj_call(
        ctx.reshape(1, b * n_ctx, d), mod, b, g1, w_c, gains, ones_bd, wva_t, wvb_t, None, groups_c,
        [A_KV_WIDTH, B_WIDTH], tm=CTX_ROWS)
    kac, kbc = (a.reshape(b, n_ctx, -1) for a in ctx_out[:2])
    vact, vbct = (a.reshape(b, n_ctx // LANES, -1, LANES) for a in ctx_out[2:])

    bias = _bias_call(rpb_b, rows)
    o, ffn_weights = _attn_call(sink_a, qa, qb, ka, vat, kb, vbt, kac, vact, kbc, vbct, bias, on_a[None],
                                on_b[None], [w_out, w_gate, w_up, w_down])
    return _ffn_call(x, o, mod, norm2_g[None], *ffn_weights, tm=FFN_ROWS)


def kernel(x, c, ctx, c_ctx, w_mod, b_mod, norm1_g, w_in, qn_a, kn_a, sink_a, qn_b, kn_b, rpb_b, on_a, on_b,
           w_out, norm2_g, w_gate, w_up, w_down):
    b, s, d = x.shape
    depth = w_mod.shape[0]
    assert depth == 1, "the context stream update is only needed when a later layer reads it"
    rope_tabs = _rope_call(s)
    l = 0
    mod, *w_in_layouts = _mod_call(c, c_ctx[None], w_mod[l], b_mod[l][None], w_in[l])
    return _layer(x, ctx, mod, norm1_g[l], w_in_layouts, qn_a[l], kn_a[l], sink_a[l], qn_b[l], kn_b[l], rpb_b[l],
                  on_a[l], on_b[l], w_out[l:l + 1], norm2_g[l], w_gate[l:l + 1], w_up[l:l + 1], w_down[l:l + 1],
                  rope_tabs)
```

```python
import functools

import numpy as np
import jax
import jax.numpy as jnp
from jax import lax
from jax.experimental import pallas as pl
from jax.experimental.pallas import tpu as pltpu

GRID_W = 64
HEAD_DIM = 64
A_HEADS = 8
A_KV_HEADS = 2
A_WINDOW = 128
B_HEADS = 8
NA_KH = 8
NA_KW = 16
NA_ROW_BLOCK = 2
ROPE_BASE = 10000.0
EPS = 1e-6

A_WIDTH = A_HEADS * HEAD_DIM
A_KV_WIDTH = A_KV_HEADS * HEAD_DIM
B_WIDTH = B_HEADS * HEAD_DIM
QBLK = NA_ROW_BLOCK * GRID_W
A_SPAN = QBLK + 2 * A_WINDOW
B_KROWS = 10
B_SPAN = B_KROWS * GRID_W
N_PATTERNS = 5
ATTN_BLOCKS_PER_STEP = 8
MOD_COLS = 1536
INPROJ_ROWS = 2048
CTX_ROWS = 512
INPROJ_SUB_ROWS = 512
FFN_ROWS = 1024
FFN_SUB_ROWS = 256
A_STACK = 4
ATTN_PV_LAG = 1
ATTN_LOOKAHEAD = 1
NEG = -1e30
LOG2E = 1.4426950408889634

LANES = 128
BF16_ROWS = 16
MXU_TILE = 256
VMEM_LIMIT = 56 * 1024 * 1024

F32 = jnp.float32
BF16 = jnp.bfloat16


def _dot(a, b):
    return jnp.dot(a, b, preferred_element_type=F32)


def _dot_nt(a, b):
    return lax.dot_general(a, b, (((1,), (1,)), ((), ())), preferred_element_type=F32)


def _silu(x):
    return x / (1.0 + jnp.exp(-x))


def _mod_kernel(c_ref, cctx_ref, w_ref, b_ref, win_ref, o_ref, wl_ref, wc_ref, wvat_ref, wvbt_ref):
    cond = jnp.concatenate([c_ref[...], jnp.broadcast_to(cctx_ref[...], c_ref.shape)], axis=0)
    split = lambda v: (v.astype(BF16), (v - v.astype(BF16).astype(F32)).astype(BF16))
    a_hi, a_lo = split(_silu(cond))
    w_hi, w_lo = split(w_ref[...])
    rows = a_hi.shape[0]
    head = _dot(jnp.concatenate([a_hi, a_lo], axis=0), w_hi)
    res = head[:rows] + head[rows:] + _dot(a_hi, w_lo) + b_ref[...]
    for r in range(rows):
        o_ref[r] = res[r:r + 1]
    _wprep_slab(win_ref, wl_ref, wc_ref, wvat_ref, wvbt_ref)


def _mod_call(c, c_ctx, w_mod, b_mod, w_in):
    b, d = c.shape
    assert b % 8 == 0
    rows = 2 * b
    n = w_mod.shape[1]
    bn = MOD_COLS
    steps = n // bn
    d_in, n_in = w_in.shape
    slab = d_in // steps
    assert (A_HEADS // A_KV_HEADS) % 2 == 0 and n_in == A_WIDTH + 2 * A_KV_WIDTH + 3 * B_WIDTH
    assert d_in % steps == 0 and slab % LANES == 0
    row_slab = lambda cols: pl.BlockSpec((slab, cols), lambda j: (j, 0))
    col_slab = lambda r: pl.BlockSpec((r, slab), lambda j: (0, j))
    n_l, n_c = A_WIDTH + A_KV_WIDTH + 2 * B_WIDTH, A_KV_WIDTH + B_WIDTH
    return pl.pallas_call(
        _mod_kernel,
        grid=(steps,),
        in_specs=[pl.BlockSpec((b, d), lambda j: (0, 0)),
                  pl.BlockSpec((1, d), lambda j: (0, 0)),
                  pl.BlockSpec((d, bn), lambda j: (0, j)),
                  pl.BlockSpec((1, bn), lambda j: (0, j)),
                  row_slab(n_in)],
        out_specs=[pl.BlockSpec((rows, 1, bn), lambda j: (0, 0, j)),
                   row_slab(n_l), row_slab(n_c), col_slab(A_KV_WIDTH), col_slab(B_WIDTH)],
        out_shape=[jax.ShapeDtypeStruct((rows, 1, n), F32),
                   jax.ShapeDtypeStruct((d_in, n_l), BF16), jax.ShapeDtypeStruct((d_in, n_c), BF16),
                   jax.ShapeDtypeStruct((A_KV_WIDTH, d_in), BF16), jax.ShapeDtypeStruct((B_WIDTH, d_in), BF16)],
        compiler_params=pltpu.CompilerParams(vmem_limit_bytes=VMEM_LIMIT),
        name="mod_w_in_layout",
    )(c, c_ctx, w_mod, b_mod, w_in)


def _wprep_slab(w_ref, wl_ref, wc_ref, wvat_ref, wvbt_ref):
    group = A_HEADS // A_KV_HEADS
    c_ka, c_va, c_qb, c_kb, c_vb = np.cumsum([A_WIDTH, A_KV_WIDTH, A_KV_WIDTH, B_WIDTH, B_WIDTH])
    lane = lax.broadcasted_iota(jnp.int32, (w_ref.shape[0], LANES), 1)
    lo = lane < HEAD_DIM
    for t in range(group):
        src = [w_ref[:, (h // 2) * LANES:(h // 2 + 1) * LANES] for h in (t, t + group)]
        if t % 2 == 0:
            tile = jnp.where(lo, src[0], pltpu.roll(src[1], HEAD_DIM, 1))
        else:
            tile = jnp.where(lo, pltpu.roll(src[0], HEAD_DIM, 1), src[1])
        wl_ref[:, t * LANES:(t + 1) * LANES] = tile.astype(BF16)
    k_a = w_ref[:, c_ka:c_va].astype(BF16)
    k_b = w_ref[:, c_kb:c_vb].astype(BF16)
    wl_ref[:, A_WIDTH:A_WIDTH + A_KV_WIDTH] = k_a
    wl_ref[:, A_WIDTH + A_KV_WIDTH:A_WIDTH + A_KV_WIDTH + B_WIDTH] = w_ref[:, c_qb:c_kb].astype(BF16)
    wl_ref[:, A_WIDTH + A_KV_WIDTH + B_WIDTH:] = k_b
    wc_ref[:, :A_KV_WIDTH] = k_a
    wc_ref[:, A_KV_WIDTH:] = k_b
    wvat_ref[...] = w_ref[:, c_va:c_qb].T.astype(BF16)
    for c in range(0, B_WIDTH, LANES):
        wvbt_ref[c:c + LANES, :] = w_ref[:, c_vb + c:c_vb + c + LANES].T.astype(BF16)


def _rope_kernel(invf_ref, cos_ref, sin_ref, *, rows):
    n_pos = max(rows, GRID_W)
    pos = lax.broadcasted_iota(jnp.int32, (n_pos, LANES), 0).astype(F32)
    lane = lax.broadcasted_iota(jnp.int32, (GRID_W, LANES), 1)
    row_lanes = (lane & (HEAD_DIM - 1)) < HEAD_DIM // 2
    first = (lane & (HEAD_DIM // 2 - 1)) < HEAD_DIM // 4
    ang = pos * invf_ref[...]
    cos_p, sin_p = jnp.cos(ang), jnp.sin(ang)
    cos_c = cos_p[:GRID_W]
    sin_c = jnp.where(first, -sin_p[:GRID_W], sin_p[:GRID_W])
    for r in range(rows):
        tok = slice(r * GRID_W, (r + 1) * GRID_W)
        cos_ref[tok, :] = jnp.where(row_lanes, cos_p[r:r + 1], cos_c)
        sin_ref[tok, :] = jnp.where(row_lanes, jnp.where(first, -sin_p[r:r + 1], sin_p[r:r + 1]), sin_c)


def _rope_call(seq):
    quarter = HEAD_DIM // 4
    inv = (1.0 / (np.float32(ROPE_BASE) ** (np.arange(quarter, dtype=np.float32) / quarter))).astype(np.float32)
    invf = jnp.asarray(np.tile(inv, LANES // quarter)[None, :])
    return pl.pallas_call(
        functools.partial(_rope_kernel, rows=seq // GRID_W),
        grid=(1,),
        in_specs=[pl.BlockSpec((1, LANES), lambda i: (0, 0))],
        out_specs=[pl.BlockSpec((seq, LANES), lambda i: (0, 0))] * 2,
        out_shape=[jax.ShapeDtypeStruct((seq, LANES), F32)] * 2,
        name="rope_tables",
    )(invf)


def _b_patterns(rows):
    n_blk = rows // NA_ROW_BLOCK
    blocks = [0, 1, 2, n_blk - 2, n_blk - 1]
    return [(NA_ROW_BLOCK * i, _b_key_start(i, rows)) for i in blocks]


def _b_key_start(i, rows):
    return min(max(NA_ROW_BLOCK * i - NA_KH // 2, 0), rows - B_KROWS)


def _bias_kernel(rpb_ref, o_ref, *, patterns, rows):
    h = pl.program_id(0)
    n_dr, n_dc = 2 * NA_KH - 1, 2 * NA_KW - 1
    kc = lax.broadcasted_iota(jnp.int32, (GRID_W, LANES), 0)
    lane = lax.broadcasted_iota(jnp.int32, (GRID_W, LANES), 1)
    qc = lane & (GRID_W - 1)
    dc = jnp.clip(kc - qc + NA_KW - 1, 0, n_dc - 1)
    cs = jnp.clip(qc - NA_KW // 2, 0, GRID_W - NA_KW)
    col_ok = (kc >= cs) & (kc < cs + NA_KW)
    lo = lane < GRID_W
    base = h * (n_dr * n_dc)
    per_dr = []
    for dr in range(n_dr):
        m = jnp.zeros((GRID_W, LANES), F32)
        for d in range(n_dc):
            m = jnp.where(dc == d, rpb_ref[base + dr * n_dc + d] * LOG2E, m)
        per_dr.append(m)
    neg = jnp.full((GRID_W, LANES), NEG, F32)
    for p, (r0, ks) in enumerate(patterns):
        q_rows = [r0 + qr for qr in range(NA_ROW_BLOCK)]
        rs = [min(max(q - NA_KH // 2, 0), rows - NA_KH) for q in q_rows]
        for kr in range(B_KROWS):
            k_row = ks + kr
            ok = [r <= k_row < r + NA_KH for r in rs]
            drs = [min(max(k_row - q + NA_KH - 1, 0), n_dr - 1) for q in q_rows]
            if not any(ok):
                piece = neg
            else:
                vals = jnp.where(lo, per_dr[drs[0]], per_dr[drs[1]])
                mask = col_ok
                if not ok[1]:
                    mask = mask & lo
                if not ok[0]:
                    mask = mask & jnp.logical_not(lo)
                piece = jnp.where(mask, vals, neg)
            o_ref[p, 0, kr * GRID_W:(kr + 1) * GRID_W, :] = piece


def _bias_call(rpb, rows):
    patterns = _b_patterns(rows)
    heads = rpb.shape[0]
    return pl.pallas_call(
        functools.partial(_bias_kernel, patterns=patterns, rows=rows),
        grid=(heads,),
        in_specs=[pl.BlockSpec(memory_space=pltpu.SMEM)],
        out_specs=pl.BlockSpec((N_PATTERNS, 1, B_SPAN, QBLK), lambda h: (0, h // 2, 0, h % 2)),
        out_shape=jax.ShapeDtypeStruct((N_PATTERNS, heads // 2, B_SPAN, 2 * QBLK), F32),
        name="na_bias",
    )(rpb.reshape(-1))


def _group_rms(p, ones_ref):
    sq = (p * p).astype(BF16)
    n = p.shape[1]
    if n <= MXU_TILE:
        ssq = _dot(sq, ones_ref[:n, :n])
    else:
        ssq = jnp.concatenate(
            [_dot(sq[:, c:c + MXU_TILE], ones_ref[...]) for c in range(0, n, MXU_TILE)], axis=1)
    return lax.rsqrt(ssq * (1.0 / HEAD_DIM) + EPS)


def _rope(x, cos, sin):
    lane = lax.broadcasted_iota(jnp.int32, (x.shape[0], LANES), 1)
    first = (lane & (HEAD_DIM // 2 - 1)) < HEAD_DIM // 4
    outs = []
    for c in range(0, x.shape[1], LANES):
        xt = x[:, c:c + LANES]
        sw = jnp.where(first, pltpu.roll(xt, LANES - HEAD_DIM // 4, 1), pltpu.roll(xt, HEAD_DIM // 4, 1))
        outs.append(xt * cos + sw * sin)
    return outs[0] if len(outs) == 1 else jnp.concatenate(outs, axis=1)


def _inproj_kernel(x_ref, sh_ref, sc_ref, g_ref, w_ref, gain_ref, ones_ref, wva_ref, wvb_ref, *rest,
                   groups, use_rope):
    if use_rope:
        cos, sin = rest[0][...], rest[1][...]
        rest = rest[2:]
    n_std = sum(len(subs) for _, _, subs in groups)
    out_refs, vt_refs = rest[:n_std], rest[n_std:]
    geff = g_ref[...] * (1.0 + sc_ref[0])
    shift = sh_ref[0]
    tm = x_ref.shape[1]
    sub = min(tm, INPROJ_SUB_ROWS)
    h_cache = {}

    def h_of(r0):
        if r0 not in h_cache:
            x = x_ref[0, r0:r0 + sub, :]
            ms = jnp.mean(x * x, axis=-1, keepdims=True)
            h_cache[r0] = ((x * lax.rsqrt(ms + EPS)) * geff + shift).astype(BF16)
        return h_cache[r0]

    def project(r0, c0, width, subs):
        return _dot(h_of(r0), w_ref[:, c0:c0 + width])

    def finish(p, r0, c0, width, subs):
        for s0, sw, rope, oi, gr in subs:
            y = p[:, s0:s0 + sw]
            y = y * _group_rms(y, ones_ref) * jnp.tile(gain_ref[gr:gr + 1, :], (1, sw // LANES))
            if rope:
                y = _rope(y, cos[r0:r0 + sub], sin[r0:r0 + sub])
            out_refs[oi][0, r0:r0 + sub, :] = y.astype(BF16)

    def project_vt(r0, wv_ref, vt_ref):
        return _dot_nt(wv_ref[...], h_of(r0))

    def finish_vt(vt, r0, wv_ref, vt_ref):
        for j in range(sub // LANES):
            vt_ref[0, r0 // LANES + j] = vt[:, j * LANES:(j + 1) * LANES].astype(BF16)

    stages = []
    for r0 in range(0, tm, sub):
        stages += [(project, finish, (r0,) + g) for g in groups]
        stages += [(project_vt, finish_vt, (r0,) + a) for a in zip((wva_ref, wvb_ref), vt_refs)]
    pending = None
    for first, second, args in stages:
        res = first(*args)
        if pending is not None:
            pending[0](pending[1], *pending[2])
        pending = (second, res, args)
    pending[0](pending[1], *pending[2])


def _inproj_call(x, mod3, mod_row, g, w, gain, ones, wva_t, wvb_t, rope_tabs, groups, out_widths, tm):
    b, s, d = x.shape
    n = w.shape[1]
    use_rope = rope_tabs is not None
    mod_idx = (lambda bi, i, j: (bi, 0, j)) if mod_row is None else (lambda bi, i, j: (mod_row, 0, j))
    const = lambda arr: pl.BlockSpec(arr.shape, lambda bi, i: (0, 0))
    in_specs = [
        pl.BlockSpec((1, tm, d), lambda bi, i: (bi, i, 0)),
        pl.BlockSpec((1, 1, d), lambda bi, i: mod_idx(bi, i, 0)),
        pl.BlockSpec((1, 1, d), lambda bi, i: mod_idx(bi, i, 1)),
        const(g), const(w), const(gain), const(ones), const(wva_t), const(wvb_t),
    ]
    args = [x, mod3, mod3, g, w, gain, ones, wva_t, wvb_t]
    if use_rope:
        in_specs += [pl.BlockSpec((tm, LANES), lambda bi, i: (i, 0))] * 2
        args += list(rope_tabs)
    vt_dims = [wva_t.shape[0], wvb_t.shape[0]]
    return pl.pallas_call(
        functools.partial(_inproj_kernel, groups=groups, use_rope=use_rope),
        grid=(b, s // tm),
        in_specs=in_specs,
        out_specs=([pl.BlockSpec((1, tm, wd), lambda bi, i: (bi, i, 0)) for wd in out_widths]
                   + [pl.BlockSpec((1, tm // LANES, vd, LANES), lambda bi, i: (bi, i, 0, 0)) for vd in vt_dims]),
        out_shape=([jax.ShapeDtypeStruct((b, s, wd), BF16) for wd in out_widths]
                   + [jax.ShapeDtypeStruct((b, s // LANES, vd, LANES), BF16) for vd in vt_dims]),
        compiler_params=pltpu.CompilerParams(dimension_semantics=("parallel", "arbitrary"),
                                             vmem_limit_bytes=VMEM_LIMIT),
        name="in_proj_rope" if use_rope else "in_proj_ctx",
    )(*args)


def _scores_t(qs, k_loc, k_ctx, add_loc):
    return _dot_nt(k_loc, qs) + add_loc, _dot_nt(k_ctx, qs)


def _softmax_t(s_loc, s_ctx, extra_logit):
    m = jnp.maximum(jnp.max(s_loc, axis=0, keepdims=True), jnp.max(s_ctx, axis=0, keepdims=True))
    if extra_logit is not None:
        m = jnp.maximum(m, extra_logit)
    return jnp.exp2((s_loc - m).astype(BF16)), jnp.exp2((s_ctx - m).astype(BF16)), m


def _pv_t(p_loc, p_ctx, m, vt_loc, vt_ctx, extra_logit):
    n_v = vt_loc.shape[0]
    with_ones = lambda vt: jnp.concatenate([vt, jnp.ones((BF16_ROWS, vt.shape[1]), BF16)], axis=0)
    acc = _dot(with_ones(vt_loc), p_loc) + _dot(with_ones(vt_ctx), p_ctx)
    l = acc[n_v:n_v + 1]
    if extra_logit is not None:
        l = l + jnp.exp2(extra_logit - m)
    return acc[:n_v] * (1.0 / l)


def _group_norm_t(o_t, g_t):
    ms = jnp.mean(o_t * o_t, axis=0, keepdims=True)
    return ((o_t * lax.rsqrt(ms + EPS)) * g_t).T.astype(BF16)


def _lane_tiles(ref, first, n, rows=slice(None)):
    return jnp.concatenate([ref[0, first + j, rows, :] for j in range(n)], axis=1)


def _attn_kernel(sink_ref, qa_ref, qb_ref, ka_ref, vat_ref, kb_ref, vbt_ref, kac_ref, vact_ref,
                 kbc_ref, vbct_ref, ga_ref, gb_ref, bias_ref, *rest, seq, n_sub, n_cast):
    o_ref = rest[n_cast]
    for src_ref, dst_ref in zip(rest[:n_cast], rest[n_cast + 1:]):
        dst_ref[...] = src_ref[0].astype(BF16)
    groups = []
    for j in range(n_sub):
        groups += _attn_block_groups(
            pl.program_id(1) * n_sub + j, slice(j * QBLK, (j + 1) * QBLK), sink_ref, qa_ref, qb_ref, ka_ref,
            vat_ref, kb_ref, vbt_ref, kac_ref, vact_ref, kbc_ref, vbct_ref, ga_ref, gb_ref, bias_ref, o_ref, seq)
    scored, soft = [], []
    for scores, finish in groups:
        scored.append(finish(*scores()))
        if len(scored) > ATTN_LOOKAHEAD:
            gen = scored.pop(0)
            next(gen)
            soft.append(gen)
        if len(soft) > ATTN_PV_LAG:
            next(soft.pop(0), None)
    for gen in scored:
        next(gen)
        soft.append(gen)
    for gen in soft:
        next(gen, None)


def _attn_block_groups(i, qrows, sink_ref, qa_ref, qb_ref, ka_ref, vat_ref, kb_ref, vbt_ref, kac_ref, vact_ref,
                       kbc_ref, vbct_ref, ga_ref, gb_ref, bias_ref, o_ref, seq):
    rows = seq // GRID_W
    stages = []
    n_ctx_tiles = kac_ref.shape[1] // LANES
    lane = lax.broadcasted_iota(jnp.int32, (QBLK, LANES), 1)
    lo = lane < HEAD_DIM
    half_masks = (lo.astype(F32).astype(BF16), jnp.logical_not(lo).astype(F32).astype(BF16))

    group = A_HEADS // A_KV_HEADS
    blk_a = jnp.clip(i - 1, 0, seq // QBLK - A_SPAN // QBLK)
    start_a = pl.multiple_of(blk_a * QBLK, QBLK)
    d = (lax.broadcasted_iota(jnp.int32, (A_SPAN, QBLK), 0)
         - lax.broadcasted_iota(jnp.int32, (A_SPAN, QBLK), 1) + (start_a - i * QBLK))
    amask = jnp.where((d <= A_WINDOW) & (d >= -A_WINDOW), 0.0, NEG).astype(F32)
    amask = jnp.concatenate([amask] * A_STACK, axis=1)
    k_loc = ka_ref[0, pl.ds(start_a, A_SPAN), :]
    k_ctx = kac_ref[0]
    q_tiles = [qa_ref[0, qrows, t * LANES:(t + 1) * LANES] for t in range(group)]
    heads_a = []

    def scores_a(kv, t0):
        qs = jnp.concatenate([q * half_masks[kv] for q in q_tiles[t0:t0 + A_STACK]], axis=0)
        return _scores_t(qs, k_loc, k_ctx, amask)

    def finish_a(kv, t0, s_loc, s_ctx):
        dims = slice(kv * HEAD_DIM, (kv + 1) * HEAD_DIM)
        sink = jnp.concatenate(
            [jnp.full((1, QBLK), sink_ref[kv * group + t0 + t] * LOG2E, F32) for t in range(A_STACK)], axis=1)
        p_loc, p_ctx, m = _softmax_t(s_loc, s_ctx, sink)
        yield
        o_t = _pv_t(p_loc, p_ctx, m, _lane_tiles(vat_ref, blk_a, A_SPAN // LANES, dims),
                    _lane_tiles(vact_ref, 0, n_ctx_tiles, dims), sink)
        heads_a.extend(o_t[:, t * QBLK:(t + 1) * QBLK] for t in range(A_STACK))
        if len(heads_a) == A_HEADS:
            o_ref[0, qrows, :A_WIDTH] = _group_norm_t(jnp.concatenate(heads_a, axis=0), ga_ref[...])

    stages += [(functools.partial(scores_a, kv, t0), functools.partial(finish_a, kv, t0))
               for kv in range(A_KV_HEADS) for t0 in range(0, group, A_STACK)]

    blk_b = jnp.clip(NA_ROW_BLOCK * i - NA_KH // 2, 0, rows - B_KROWS) // NA_ROW_BLOCK
    start_b = pl.multiple_of(blk_b * QBLK, QBLK)
    heads_b = []
    n_blk = seq // QBLK
    pattern = jnp.where(i < 2, i, jnp.where(i >= n_blk - 2, i - (n_blk - N_PATTERNS), 2))

    def scores_b(pair):
        cols = slice(pair * LANES, (pair + 1) * LANES)
        qp = qb_ref[0, qrows, cols]
        qs = jnp.concatenate([qp * half_masks[0], qp * half_masks[1]], axis=0)
        return _scores_t(qs, kb_ref[0, pl.ds(start_b, B_SPAN), cols], kbc_ref[0, :, cols], bias_ref[pattern, pair])

    def finish_b(pair, s_loc, s_ctx):
        cols = slice(pair * LANES, (pair + 1) * LANES)
        p_loc, p_ctx, m = _softmax_t(s_loc, s_ctx, None)
        yield
        o_t = _pv_t(p_loc, p_ctx, m, _lane_tiles(vbt_ref, blk_b, B_SPAN // LANES, cols),
                    _lane_tiles(vbct_ref, 0, n_ctx_tiles, cols), None)
        heads_b.extend([o_t[:HEAD_DIM, :QBLK], o_t[HEAD_DIM:, QBLK:]])
        if pair == B_HEADS // 2 - 1:
            o_ref[0, qrows, A_WIDTH:] = _group_norm_t(jnp.concatenate(heads_b, axis=0), gb_ref[...])

    stages += [(functools.partial(scores_b, p), functools.partial(finish_b, p)) for p in range(B_HEADS // 2)]
    return stages


def _attn_call(sink, qa, qb, ka, vat, kb, vbt, kac, vact, kbc, vbct, bias, ga_t, gb_t, cast_weights):
    b, s, _ = qa.shape
    n_blk = s // QBLK
    assert (NA_KH // 2) % NA_ROW_BLOCK == 0 and (s // GRID_W - B_KROWS) % NA_ROW_BLOCK == 0

    n_sub = ATTN_BLOCKS_PER_STEP
    steps_per_batch = n_blk // n_sub
    full = lambda arr: pl.BlockSpec((1,) + arr.shape[1:], lambda bi, i: (bi,) + (0,) * (arr.ndim - 1))
    const = lambda arr, **kw: pl.BlockSpec(arr.shape, lambda bi, i: (0,) * arr.ndim, **kw)
    blk = lambda wd: pl.BlockSpec((1, n_sub * QBLK, wd), lambda bi, i: (bi, i, 0))
    n_steps = b * steps_per_batch
    in_slabs, out_slabs = [], []
    for w in cast_weights:
        _, rows, cols = w.shape
        c = next(c for c in range(1, n_steps + 1)
                 if n_steps % c == 0 and cols % (c * LANES) == 0 and rows % (n_steps // c * BF16_ROWS) == 0)
        r = n_steps // c
        step = lambda bi, i: bi * steps_per_batch + i
        in_slabs.append(pl.BlockSpec((1, rows // r, cols // c),
                                     lambda bi, i, c=c: (0, step(bi, i) // c, step(bi, i) % c)))
        out_slabs.append(pl.BlockSpec((rows // r, cols // c), lambda bi, i, c=c: (step(bi, i) // c, step(bi, i) % c)))
    outs = pl.pallas_call(
        functools.partial(_attn_kernel, seq=s, n_sub=n_sub, n_cast=len(cast_weights)),
        grid=(b, steps_per_batch),
        in_specs=[
            pl.BlockSpec(memory_space=pltpu.SMEM),
            blk(A_WIDTH), blk(B_WIDTH),
            full(ka), full(vat), full(kb), full(vbt), full(kac), full(vact), full(kbc), full(vbct),
            const(ga_t), const(gb_t), const(bias, pipeline_mode=pl.Buffered(1)),
        ] + in_slabs,
        out_specs=[blk(A_WIDTH + B_WIDTH)] + out_slabs,
        out_shape=([jax.ShapeDtypeStruct((b, s, A_WIDTH + B_WIDTH), BF16)]
                   + [jax.ShapeDtypeStruct(w.shape[1:], BF16) for w in cast_weights]),
        compiler_params=pltpu.CompilerParams(dimension_semantics=("parallel", "arbitrary"),
                                             vmem_limit_bytes=VMEM_LIMIT),
        name="attn",
    )(sink, qa, qb, ka, vat, kb, vbt, kac, vact, kbc, vbct, ga_t, gb_t, bias, *cast_weights)
    return outs[0], outs[1:]


def _ffn_kernel(x_ref, o_ref, g1_ref, sh2_ref, sc2_ref, g2_ref, n2_ref, wo_ref, wg_ref, wu_ref, wd_ref,
                out_ref):
    geff = n2_ref[...] * (1.0 + sc2_ref[0])
    tm = x_ref.shape[1]
    sub = min(tm, FFN_SUB_ROWS)

    def stage_rows(r0):
        rows = slice(r0, r0 + sub)
        y = _dot(o_ref[0, rows, :], wo_ref[...])
        yield
        x1 = x_ref[0, rows, :] + g1_ref[0] * y
        ms = jnp.mean(x1 * x1, axis=-1, keepdims=True)
        h2 = ((x1 * lax.rsqrt(ms + EPS)) * geff + sh2_ref[0]).astype(BF16)
        gate = _dot(h2, wg_ref[...])
        up = _dot(h2, wu_ref[...])
        yield
        act = (_silu(gate) * up).astype(BF16)
        down = _dot(act, wd_ref[...])
        yield
        out_ref[0, rows, :] = x1 + g2_ref[0] * down

    live = [stage_rows(r0) for r0 in range(0, tm, sub)]
    while live:
        for gen in list(live):
            if next(gen, StopIteration) is StopIteration:
                live.remove(gen)


def _ffn_call(x, o, mod3, n2, wo, wg, wu, wd, tm):
    b, s, d = x.shape
    hid = wg.shape[1]
    modspec = lambda j: pl.BlockSpec((1, 1, d), lambda bi, i: (bi, 0, j))
    const = lambda shape: pl.BlockSpec(shape, lambda bi, i: (0, 0), pipeline_mode=pl.Buffered(1))
    return pl.pallas_call(
        _ffn_kernel,
        grid=(b, s // tm),
        in_specs=[
            pl.BlockSpec((1, tm, d), lambda bi, i: (bi, i, 0)),
            pl.BlockSpec((1, tm, o.shape[2]), lambda bi, i: (bi, i, 0)),
            modspec(2), modspec(3), modspec(4), modspec(5),
            pl.BlockSpec((1, d), lambda bi, i: (0, 0)),
            const((o.shape[2], d)), const((d, hid)), const((d, hid)), const((hid, d)),
        ],
        out_specs=pl.BlockSpec((1, tm, d), lambda bi, i: (bi, i, 0)),
        out_shape=jax.ShapeDtypeStruct((b, s, d), F32),
        compiler_params=pltpu.CompilerParams(dimension_semantics=("parallel", "arbitrary"),
                                             vmem_limit_bytes=VMEM_LIMIT),
        name="out_ffn",
    )(x, o, mod3, mod3, mod3, mod3, n2, wo, wg, wu, wd)


def _layer(x, ctx, mod, norm1_g, w_in, qn_a, kn_a, sink_a, qn_b, kn_b, rpb_b, on_a, on_b, w_out,
           norm2_g, w_gate, w_up, w_down, rope_tabs):
    b, s, d = x.shape
    rows = s // GRID_W
    w_l, w_c, wva_t, wvb_t = w_in
    scale = HEAD_DIM ** -0.5 * LOG2E
    gains = jnp.tile(jnp.stack([qn_a * scale, kn_a, qn_b * scale, kn_b]), (1, LANES // HEAD_DIM))
    ones_bd = jnp.asarray(np.kron(np.eye(MXU_TILE // HEAD_DIM), np.ones((HEAD_DIM, HEAD_DIM))), BF16)
    g1 = norm1_g[None]

    qka = A_WIDTH + A_KV_WIDTH
    groups_l = [
        (0, qka, [(0, A_WIDTH, True, 0, 0), (A_WIDTH, A_KV_WIDTH, True, 1, 1)]),
        (qka, B_WIDTH, [(0, B_WIDTH, False, 2, 2)]),
        (qka + B_WIDTH, B_WIDTH, [(0, B_WIDTH, False, 3, 3)]),
    ]
    qa, ka, qb, kb, vat, vbt = _inproj_call(
        x, mod, None, g1, w_l, gains, ones_bd, wva_t, wvb_t, rope_tabs, groups_l,
        [A_WIDTH, A_KV_WIDTH, B_WIDTH, B_WIDTH], tm=INPROJ_ROWS)
    groups_c = [(0, A_KV_WIDTH + B_WIDTH, [(0, A_KV_WIDTH, False, 0, 1), (A_KV_WIDTH, B_WIDTH, False, 1, 3)])]
    n_ctx = ctx.shape[1]
    ctx_out = _inproj_call(
        ctx.reshape(1, b * n_ctx, d), mod, b, g1, w_c, gains, ones_bd, wva_t, wvb_t, None, groups_c,
        [A_KV_WIDTH, B_WIDTH], tm=CTX_ROWS)
    kac, kbc = (a.reshape(b, n_ctx, -1) for a in ctx_out[:2])
    vact, vbct = (a.reshape(b, n_ctx // LANES, -1, LANES) for a in ctx_out[2:])

    bias = _bias_call(rpb_b, rows)
    lanes_of = lambda g: jnp.broadcast_to(g[:, None], (g.shape[0], LANES))
    o, ffn_weights = _attn_call(sink_a, qa, qb, ka, vat, kb, vbt, kac, vact, kbc, vbct, bias, lanes_of(on_a),
                                lanes_of(on_b), [w_out, w_gate, w_up, w_down])
    return _ffn_call(x, o, mod, norm2_g[None], *ffn_weights, tm=FFN_ROWS)


def kernel(x, c, ctx, c_ctx, w_mod, b_mod, norm1_g, w_in, qn_a, kn_a, sink_a, qn_b, kn_b, rpb_b, on_a, on_b,
           w_out, norm2_g, w_gate, w_up, w_down):
    b, s, d = x.shape
    depth = w_mod.shape[0]
    assert depth == 1, "the context stream update is only needed when a later layer reads it"
    rope_tabs = _rope_call(s)
    l = 0
    mod, *w_in_layouts = _mod_call(c, c_ctx[None], w_mod[l], b_mod[l][None], w_in[l])
    return _layer(x, ctx, mod, norm1_g[l], w_in_layouts, qn_a[l], kn_a[l], sink_a[l], qn_b[l], kn_b[l], rpb_b[l],
                  on_a[l], on_b[l], w_out[l:l + 1], norm2_g[l], w_gate[l:l + 1], w_up[l:l + 1], w_down[l:l + 1],
                  rope_tabs)
```

```python
import functools

import numpy as np
import jax
import jax.numpy as jnp
from jax import lax
from jax.experimental import pallas as pl
from jax.experimental.pallas import tpu as pltpu

GRID_W = 64
HEAD_DIM = 64
A_HEADS = 8
A_KV_HEADS = 2
A_WINDOW = 128
B_HEADS = 8
NA_KH = 8
NA_KW = 16
NA_ROW_BLOCK = 2
ROPE_BASE = 10000.0
EPS = 1e-6

A_WIDTH = A_HEADS * HEAD_DIM
A_KV_WIDTH = A_KV_HEADS * HEAD_DIM
B_WIDTH = B_HEADS * HEAD_DIM
QBLK = NA_ROW_BLOCK * GRID_W
A_SPAN = QBLK + 2 * A_WINDOW
B_KROWS = 10
B_SPAN = B_KROWS * GRID_W
N_PATTERNS = 5
ATTN_BLOCKS_PER_STEP = 8
MOD_COLS = 1536
INPROJ_ROWS = 2048
CTX_ROWS = 512
INPROJ_SUB_ROWS = 512
FFN_ROWS = 1024
FFN_SUB_ROWS = 256
A_STACK = 4
ATTN_PV_LAG = 1
ATTN_LOOKAHEAD = 1
NEG = -1e30
LOG2E = 1.4426950408889634

LANES = 128
BF16_ROWS = 16
MXU_TILE = 256
VMEM_LIMIT = 56 * 1024 * 1024

F32 = jnp.float32
BF16 = jnp.bfloat16


def _dot(a, b):
    return jnp.dot(a, b, preferred_element_type=F32)


def _dot_nt(a, b):
    return lax.dot_general(a, b, (((1,), (1,)), ((), ())), preferred_element_type=F32)


def _silu(x):
    return x / (1.0 + jnp.exp(-x))


def _mod_kernel(c_ref, cctx_ref, w_ref, b_ref, win_ref, ona_ref, onb_ref, o_ref, wl_ref, wc_ref, wvat_ref, wvbt_ref,
                ga_ref, gb_ref):
    for on_ref, g_ref in ((ona_ref, ga_ref), (onb_ref, gb_ref)):
        g_ref[...] = jnp.broadcast_to(on_ref[...], (LANES, on_ref.shape[1])).T
    cond = jnp.concatenate([c_ref[...], jnp.broadcast_to(cctx_ref[...], c_ref.shape)], axis=0)
    split = lambda v: (v.astype(BF16), (v - v.astype(BF16).astype(F32)).astype(BF16))
    a_hi, a_lo = split(_silu(cond))
    w_hi, w_lo = split(w_ref[...])
    rows = a_hi.shape[0]
    head = _dot(jnp.concatenate([a_hi, a_lo], axis=0), w_hi)
    res = head[:rows] + head[rows:] + _dot(a_hi, w_lo) + b_ref[...]
    for r in range(rows):
        o_ref[r] = res[r:r + 1]
    _wprep_slab(win_ref, wl_ref, wc_ref, wvat_ref, wvbt_ref)


def _mod_call(c, c_ctx, w_mod, b_mod, w_in, on_a, on_b):
    b, d = c.shape
    assert b % 8 == 0
    rows = 2 * b
    n = w_mod.shape[1]
    bn = MOD_COLS
    steps = n // bn
    d_in, n_in = w_in.shape
    slab = d_in // steps
    assert (A_HEADS // A_KV_HEADS) % 2 == 0 and n_in == A_WIDTH + 2 * A_KV_WIDTH + 3 * B_WIDTH
    assert d_in % steps == 0 and slab % LANES == 0
    row_slab = lambda cols: pl.BlockSpec((slab, cols), lambda j: (j, 0))
    col_slab = lambda r: pl.BlockSpec((r, slab), lambda j: (0, j))
    n_l, n_c = A_WIDTH + A_KV_WIDTH + 2 * B_WIDTH, A_KV_WIDTH + B_WIDTH
    return pl.pallas_call(
        _mod_kernel,
        grid=(steps,),
        in_specs=[pl.BlockSpec((b, d), lambda j: (0, 0)),
                  pl.BlockSpec((1, d), lambda j: (0, 0)),
                  pl.BlockSpec((d, bn), lambda j: (0, j)),
                  pl.BlockSpec((1, bn), lambda j: (0, j)),
                  row_slab(n_in),
                  pl.BlockSpec(on_a.shape, lambda j: (0, 0)), pl.BlockSpec(on_b.shape, lambda j: (0, 0))],
        out_specs=[pl.BlockSpec((rows, 1, bn), lambda j: (0, 0, j)),
                   row_slab(n_l), row_slab(n_c), col_slab(A_KV_WIDTH), col_slab(B_WIDTH),
                   pl.BlockSpec((on_a.shape[1], LANES), lambda j: (0, 0)),
                   pl.BlockSpec((on_b.shape[1], LANES), lambda j: (0, 0))],
        out_shape=[jax.ShapeDtypeStruct((rows, 1, n), F32),
                   jax.ShapeDtypeStruct((d_in, n_l), BF16), jax.ShapeDtypeStruct((d_in, n_c), BF16),
                   jax.ShapeDtypeStruct((A_KV_WIDTH, d_in), BF16), jax.ShapeDtypeStruct((B_WIDTH, d_in), BF16),
                   jax.ShapeDtypeStruct((on_a.shape[1], LANES), F32),
                   jax.ShapeDtypeStruct((on_b.shape[1], LANES), F32)],
        compiler_params=pltpu.CompilerParams(vmem_limit_bytes=VMEM_LIMIT),
        name="mod_w_in_layout",
    )(c, c_ctx, w_mod, b_mod, w_in, on_a, on_b)


def _wprep_slab(w_ref, wl_ref, wc_ref, wvat_ref, wvbt_ref):
    group = A_HEADS // A_KV_HEADS
    c_ka, c_va, c_qb, c_kb, c_vb = np.cumsum([A_WIDTH, A_KV_WIDTH, A_KV_WIDTH, B_WIDTH, B_WIDTH])
    lane = lax.broadcasted_iota(jnp.int32, (w_ref.shape[0], LANES), 1)
    lo = lane < HEAD_DIM
    for t in range(group):
        src = [w_ref[:, (h // 2) * LANES:(h // 2 + 1) * LANES] for h in (t, t + group)]
        if t % 2 == 0:
            tile = jnp.where(lo, src[0], pltpu.roll(src[1], HEAD_DIM, 1))
        else:
            tile = jnp.where(lo, pltpu.roll(src[0], HEAD_DIM, 1), src[1])
        wl_ref[:, t * LANES:(t + 1) * LANES] = tile.astype(BF16)
    k_a = w_ref[:, c_ka:c_va].astype(BF16)
    k_b = w_ref[:, c_kb:c_vb].astype(BF16)
    wl_ref[:, A_WIDTH:A_WIDTH + A_KV_WIDTH] = k_a
    wl_ref[:, A_WIDTH + A_KV_WIDTH:A_WIDTH + A_KV_WIDTH + B_WIDTH] = w_ref[:, c_qb:c_kb].astype(BF16)
    wl_ref[:, A_WIDTH + A_KV_WIDTH + B_WIDTH:] = k_b
    wc_ref[:, :A_KV_WIDTH] = k_a
    wc_ref[:, A_KV_WIDTH:] = k_b
    wvat_ref[...] = w_ref[:, c_va:c_qb].T.astype(BF16)
    for c in range(0, B_WIDTH, LANES):
        wvbt_ref[c:c + LANES, :] = w_ref[:, c_vb + c:c_vb + c + LANES].T.astype(BF16)


def _rope_kernel(invf_ref, cos_ref, sin_ref, *, rows):
    n_pos = max(rows, GRID_W)
    pos = lax.broadcasted_iota(jnp.int32, (n_pos, LANES), 0).astype(F32)
    lane = lax.broadcasted_iota(jnp.int32, (GRID_W, LANES), 1)
    row_lanes = (lane & (HEAD_DIM - 1)) < HEAD_DIM // 2
    first = (lane & (HEAD_DIM // 2 - 1)) < HEAD_DIM // 4
    ang = pos * invf_ref[...]
    cos_p, sin_p = jnp.cos(ang), jnp.sin(ang)
    cos_c = cos_p[:GRID_W]
    sin_c = jnp.where(first, -sin_p[:GRID_W], sin_p[:GRID_W])
    for r in range(rows):
        tok = slice(r * GRID_W, (r + 1) * GRID_W)
        cos_ref[tok, :] = jnp.where(row_lanes, cos_p[r:r + 1], cos_c)
        sin_ref[tok, :] = jnp.where(row_lanes, jnp.where(first, -sin_p[r:r + 1], sin_p[r:r + 1]), sin_c)


def _rope_call(seq):
    quarter = HEAD_DIM // 4
    inv = (1.0 / (np.float32(ROPE_BASE) ** (np.arange(quarter, dtype=np.float32) / quarter))).astype(np.float32)
    invf = jnp.asarray(np.tile(inv, LANES // quarter)[None, :])
    return pl.pallas_call(
        functools.partial(_rope_kernel, rows=seq // GRID_W),
        grid=(1,),
        in_specs=[pl.BlockSpec((1, LANES), lambda i: (0, 0))],
        out_specs=[pl.BlockSpec((seq, LANES), lambda i: (0, 0))] * 2,
        out_shape=[jax.ShapeDtypeStruct((seq, LANES), F32)] * 2,
        name="rope_tables",
    )(invf)


def _b_patterns(rows):
    n_blk = rows // NA_ROW_BLOCK
    blocks = [0, 1, 2, n_blk - 2, n_blk - 1]
    return [(NA_ROW_BLOCK * i, _b_key_start(i, rows)) for i in blocks]


def _b_key_start(i, rows):
    return min(max(NA_ROW_BLOCK * i - NA_KH // 2, 0), rows - B_KROWS)


def _bias_kernel(rpb_ref, o_ref, *, patterns, rows):
    h = pl.program_id(0)
    n_dr, n_dc = 2 * NA_KH - 1, 2 * NA_KW - 1
    kc = lax.broadcasted_iota(jnp.int32, (GRID_W, LANES), 0)
    lane = lax.broadcasted_iota(jnp.int32, (GRID_W, LANES), 1)
    qc = lane & (GRID_W - 1)
    dc = jnp.clip(kc - qc + NA_KW - 1, 0, n_dc - 1)
    cs = jnp.clip(qc - NA_KW // 2, 0, GRID_W - NA_KW)
    col_ok = (kc >= cs) & (kc < cs + NA_KW)
    lo = lane < GRID_W
    base = h * (n_dr * n_dc)
    per_dr = []
    for dr in range(n_dr):
        m = jnp.zeros((GRID_W, LANES), F32)
        for d in range(n_dc):
            m = jnp.where(dc == d, rpb_ref[base + dr * n_dc + d] * LOG2E, m)
        per_dr.append(m)
    neg = jnp.full((GRID_W, LANES), NEG, F32)
    for p, (r0, ks) in enumerate(patterns):
        q_rows = [r0 + qr for qr in range(NA_ROW_BLOCK)]
        rs = [min(max(q - NA_KH // 2, 0), rows - NA_KH) for q in q_rows]
        for kr in range(B_KROWS):
            k_row = ks + kr
            ok = [r <= k_row < r + NA_KH for r in rs]
            drs = [min(max(k_row - q + NA_KH - 1, 0), n_dr - 1) for q in q_rows]
            if not any(ok):
                piece = neg
            else:
                vals = jnp.where(lo, per_dr[drs[0]], per_dr[drs[1]])
                mask = col_ok
                if not ok[1]:
                    mask = mask & lo
                if not ok[0]:
                    mask = mask & jnp.logical_not(lo)
                piece = jnp.where(mask, vals, neg)
            o_ref[p, 0, kr * GRID_W:(kr + 1) * GRID_W, :] = piece


def _bias_call(rpb, rows):
    patterns = _b_patterns(rows)
    heads = rpb.shape[0]
    return pl.pallas_call(
        functools.partial(_bias_kernel, patterns=patterns, rows=rows),
        grid=(heads,),
        in_specs=[pl.BlockSpec(memory_space=pltpu.SMEM)],
        out_specs=pl.BlockSpec((N_PATTERNS, 1, B_SPAN, QBLK), lambda h: (0, h // 2, 0, h % 2)),
        out_shape=jax.ShapeDtypeStruct((N_PATTERNS, heads // 2, B_SPAN, 2 * QBLK), F32),
        name="na_bias",
    )(rpb.reshape(-1))


def _group_rms(p, ones_ref):
    sq = (p * p).astype(BF16)
    n = p.shape[1]
    if n <= MXU_TILE:
        ssq = _dot(sq, ones_ref[:n, :n])
    else:
        ssq = jnp.concatenate(
            [_dot(sq[:, c:c + MXU_TILE], ones_ref[...]) for c in range(0, n, MXU_TILE)], axis=1)
    return lax.rsqrt(ssq * (1.0 / HEAD_DIM) + EPS)


def _rope(x, cos, sin):
    lane = lax.broadcasted_iota(jnp.int32, (x.shape[0], LANES), 1)
    first = (lane & (HEAD_DIM // 2 - 1)) < HEAD_DIM // 4
    outs = []
    for c in range(0, x.shape[1], LANES):
        xt = x[:, c:c + LANES]
        sw = jnp.where(first, pltpu.roll(xt, LANES - HEAD_DIM // 4, 1), pltpu.roll(xt, HEAD_DIM // 4, 1))
        outs.append(xt * cos + sw * sin)
    return outs[0] if len(outs) == 1 else jnp.concatenate(outs, axis=1)


def _inproj_kernel(x_ref, sh_ref, sc_ref, g_ref, w_ref, gain_ref, ones_ref, wva_ref, wvb_ref, *rest,
                   groups, use_rope):
    if use_rope:
        cos, sin = rest[0][...], rest[1][...]
        rest = rest[2:]
    n_std = sum(len(subs) for _, _, subs in groups)
    out_refs, vt_refs = rest[:n_std], rest[n_std:]
    geff = g_ref[...] * (1.0 + sc_ref[0])
    shift = sh_ref[0]
    tm = x_ref.shape[1]
    sub = min(tm, INPROJ_SUB_ROWS)
    h_cache = {}

    def h_of(r0):
        if r0 not in h_cache:
            x = x_ref[0, r0:r0 + sub, :]
            ms = jnp.mean(x * x, axis=-1, keepdims=True)
            h_cache[r0] = ((x * lax.rsqrt(ms + EPS)) * geff + shift).astype(BF16)
        return h_cache[r0]

    def project(r0, c0, width, subs):
        return _dot(h_of(r0), w_ref[:, c0:c0 + width])

    def finish(p, r0, c0, width, subs):
        for s0, sw, rope, oi, gr in subs:
            y = p[:, s0:s0 + sw]
            y = y * _group_rms(y, ones_ref) * jnp.tile(gain_ref[gr:gr + 1, :], (1, sw // LANES))
            if rope:
                y = _rope(y, cos[r0:r0 + sub], sin[r0:r0 + sub])
            out_refs[oi][0, r0:r0 + sub, :] = y.astype(BF16)

    def project_vt(r0, wv_ref, vt_ref):
        return _dot_nt(wv_ref[...], h_of(r0))

    def finish_vt(vt, r0, wv_ref, vt_ref):
        for j in range(sub // LANES):
            vt_ref[0, r0 // LANES + j] = vt[:, j * LANES:(j + 1) * LANES].astype(BF16)

    stages = []
    for r0 in range(0, tm, sub):
        stages += [(project, finish, (r0,) + g) for g in groups]
        stages += [(project_vt, finish_vt, (r0,) + a) for a in zip((wva_ref, wvb_ref), vt_refs)]
    pending = None
    for first, second, args in stages:
        res = first(*args)
        if pending is not None:
            pending[0](pending[1], *pending[2])
        pending = (second, res, args)
    pending[0](pending[1], *pending[2])


def _inproj_call(x, mod3, mod_row, g, w, gain, ones, wva_t, wvb_t, rope_tabs, groups, out_widths, tm):
    b, s, d = x.shape
    n = w.shape[1]
    use_rope = rope_tabs is not None
    mod_idx = (lambda bi, i, j: (bi, 0, j)) if mod_row is None else (lambda bi, i, j: (mod_row, 0, j))
    const = lambda arr: pl.BlockSpec(arr.shape, lambda bi, i: (0, 0))
    in_specs = [
        pl.BlockSpec((1, tm, d), lambda bi, i: (bi, i, 0)),
        pl.BlockSpec((1, 1, d), lambda bi, i: mod_idx(bi, i, 0)),
        pl.BlockSpec((1, 1, d), lambda bi, i: mod_idx(bi, i, 1)),
        const(g), const(w), const(gain), const(ones), const(wva_t), const(wvb_t),
    ]
    args = [x, mod3, mod3, g, w, gain, ones, wva_t, wvb_t]
    if use_rope:
        in_specs += [pl.BlockSpec((tm, LANES), lambda bi, i: (i, 0))] * 2
        args += list(rope_tabs)
    vt_dims = [wva_t.shape[0], wvb_t.shape[0]]
    return pl.pallas_call(
        functools.partial(_inproj_kernel, groups=groups, use_rope=use_rope),
        grid=(b, s // tm),
        in_specs=in_specs,
        out_specs=([pl.BlockSpec((1, tm, wd), lambda bi, i: (bi, i, 0)) for wd in out_widths]
                   + [pl.BlockSpec((1, tm // LANES, vd, LANES), lambda bi, i: (bi, i, 0, 0)) for vd in vt_dims]),
        out_shape=([jax.ShapeDtypeStruct((b, s, wd), BF16) for wd in out_widths]
                   + [jax.ShapeDtypeStruct((b, s // LANES, vd, LANES), BF16) for vd in vt_dims]),
        compiler_params=pltpu.CompilerParams(dimension_semantics=("parallel", "arbitrary"),
                                             vmem_limit_bytes=VMEM_LIMIT),
        name="in_proj_rope" if use_rope else "in_proj_ctx",
    )(*args)


def _scores_t(qs, k_loc, k_ctx, add_loc):
    return _dot_nt(k_loc, qs) + add_loc, _dot_nt(k_ctx, qs)


def _softmax_t(s_loc, s_ctx, extra_logit):
    m = jnp.maximum(jnp.max(s_loc, axis=0, keepdims=True), jnp.max(s_ctx, axis=0, keepdims=True))
    if extra_logit is not None:
        m = jnp.maximum(m, extra_logit)
    return jnp.exp2((s_loc - m).astype(BF16)), jnp.exp2((s_ctx - m).astype(BF16)), m


def _pv_t(p_loc, p_ctx, m, vt_loc, vt_ctx, extra_logit):
    n_v = vt_loc.shape[0]
    with_ones = lambda vt: jnp.concatenate([vt, jnp.ones((BF16_ROWS, vt.shape[1]), BF16)], axis=0)
    acc = _dot(with_ones(vt_loc), p_loc) + _dot(with_ones(vt_ctx), p_ctx)
    l = acc[n_v:n_v + 1]
    if extra_logit is not None:
        l = l + jnp.exp2(extra_logit - m)
    return acc[:n_v] * (1.0 / l)


def _group_norm_t(o_t, g_t):
    ms = jnp.mean(o_t * o_t, axis=0, keepdims=True)
    return ((o_t * lax.rsqrt(ms + EPS)) * g_t).T.astype(BF16)


def _lane_tiles(ref, first, n, rows=slice(None)):
    return jnp.concatenate([ref[0, first + j, rows, :] for j in range(n)], axis=1)


def _attn_kernel(sink_ref, qa_ref, qb_ref, ka_ref, vat_ref, kb_ref, vbt_ref, kac_ref, vact_ref,
                 kbc_ref, vbct_ref, ga_ref, gb_ref, bias_ref, *rest, seq, n_sub, n_cast):
    o_ref = rest[n_cast]
    for src_ref, dst_ref in zip(rest[:n_cast], rest[n_cast + 1:]):
        dst_ref[...] = src_ref[0].astype(BF16)
    groups = []
    for j in range(n_sub):
        groups += _attn_block_groups(
            pl.program_id(1) * n_sub + j, slice(j * QBLK, (j + 1) * QBLK), sink_ref, qa_ref, qb_ref, ka_ref,
            vat_ref, kb_ref, vbt_ref, kac_ref, vact_ref, kbc_ref, vbct_ref, ga_ref, gb_ref, bias_ref, o_ref, seq)
    scored, soft = [], []
    for scores, finish in groups:
        scored.append(finish(*scores()))
        if len(scored) > ATTN_LOOKAHEAD:
            gen = scored.pop(0)
            next(gen)
            soft.append(gen)
        if len(soft) > ATTN_PV_LAG:
            next(soft.pop(0), None)
    for gen in scored:
        next(gen)
        soft.append(gen)
    for gen in soft:
        next(gen, None)


def _attn_block_groups(i, qrows, sink_ref, qa_ref, qb_ref, ka_ref, vat_ref, kb_ref, vbt_ref, kac_ref, vact_ref,
                       kbc_ref, vbct_ref, ga_ref, gb_ref, bias_ref, o_ref, seq):
    rows = seq // GRID_W
    stages = []
    n_ctx_tiles = kac_ref.shape[1] // LANES
    lane = lax.broadcasted_iota(jnp.int32, (QBLK, LANES), 1)
    lo = lane < HEAD_DIM
    half_masks = (lo.astype(F32).astype(BF16), jnp.logical_not(lo).astype(F32).astype(BF16))

    group = A_HEADS // A_KV_HEADS
    blk_a = jnp.clip(i - 1, 0, seq // QBLK - A_SPAN // QBLK)
    start_a = pl.multiple_of(blk_a * QBLK, QBLK)
    d = (lax.broadcasted_iota(jnp.int32, (A_SPAN, QBLK), 0)
         - lax.broadcasted_iota(jnp.int32, (A_SPAN, QBLK), 1) + (start_a - i * QBLK))
    amask = jnp.where((d <= A_WINDOW) & (d >= -A_WINDOW), 0.0, NEG).astype(F32)
    amask = jnp.concatenate([amask] * A_STACK, axis=1)
    k_loc = ka_ref[0, pl.ds(start_a, A_SPAN), :]
    k_ctx = kac_ref[0]
    q_tiles = [qa_ref[0, qrows, t * LANES:(t + 1) * LANES] for t in range(group)]
    heads_a = []

    def scores_a(kv, t0):
        qs = jnp.concatenate([q * half_masks[kv] for q in q_tiles[t0:t0 + A_STACK]], axis=0)
        return _scores_t(qs, k_loc, k_ctx, amask)

    def finish_a(kv, t0, s_loc, s_ctx):
        dims = slice(kv * HEAD_DIM, (kv + 1) * HEAD_DIM)
        sink = jnp.concatenate(
            [jnp.full((1, QBLK), sink_ref[kv * group + t0 + t] * LOG2E, F32) for t in range(A_STACK)], axis=1)
        p_loc, p_ctx, m = _softmax_t(s_loc, s_ctx, sink)
        yield
        o_t = _pv_t(p_loc, p_ctx, m, _lane_tiles(vat_ref, blk_a, A_SPAN // LANES, dims),
                    _lane_tiles(vact_ref, 0, n_ctx_tiles, dims), sink)
        heads_a.extend(o_t[:, t * QBLK:(t + 1) * QBLK] for t in range(A_STACK))
        if len(heads_a) == A_HEADS:
            o_ref[0, qrows, :A_WIDTH] = _group_norm_t(jnp.concatenate(heads_a, axis=0), ga_ref[...])

    stages += [(functools.partial(scores_a, kv, t0), functools.partial(finish_a, kv, t0))
               for kv in range(A_KV_HEADS) for t0 in range(0, group, A_STACK)]

    blk_b = jnp.clip(NA_ROW_BLOCK * i - NA_KH // 2, 0, rows - B_KROWS) // NA_ROW_BLOCK
    start_b = pl.multiple_of(blk_b * QBLK, QBLK)
    heads_b = []
    n_blk = seq // QBLK
    pattern = jnp.where(i < 2, i, jnp.where(i >= n_blk - 2, i - (n_blk - N_PATTERNS), 2))

    def scores_b(pair):
        cols = slice(pair * LANES, (pair + 1) * LANES)
        qp = qb_ref[0, qrows, cols]
        qs = jnp.concatenate([qp * half_masks[0], qp * half_masks[1]], axis=0)
        return _scores_t(qs, kb_ref[0, pl.ds(start_b, B_SPAN), cols], kbc_ref[0, :, cols], bias_ref[pattern, pair])

    def finish_b(pair, s_loc, s_ctx):
        cols = slice(pair * LANES, (pair + 1) * LANES)
        p_loc, p_ctx, m = _softmax_t(s_loc, s_ctx, None)
        yield
        o_t = _pv_t(p_loc, p_ctx, m, _lane_tiles(vbt_ref, blk_b, B_SPAN // LANES, cols),
                    _lane_tiles(vbct_ref, 0, n_ctx_tiles, cols), None)
        heads_b.extend([o_t[:HEAD_DIM, :QBLK], o_t[HEAD_DIM:, QBLK:]])
        if pair == B_HEADS // 2 - 1:
            o_ref[0, qrows, A_WIDTH:] = _group_norm_t(jnp.concatenate(heads_b, axis=0), gb_ref[...])

    stages += [(functools.partial(scores_b, p), functools.partial(finish_b, p)) for p in range(B_HEADS // 2)]
    return stages


def _attn_call(sink, qa, qb, ka, vat, kb, vbt, kac, vact, kbc, vbct, bias, ga_t, gb_t, cast_weights):
    b, s, _ = qa.shape
    n_blk = s // QBLK
    assert (NA_KH // 2) % NA_ROW_BLOCK == 0 and (s // GRID_W - B_KROWS) % NA_ROW_BLOCK == 0

    n_sub = ATTN_BLOCKS_PER_STEP
    steps_per_batch = n_blk // n_sub
    full = lambda arr: pl.BlockSpec((1,) + arr.shape[1:], lambda bi, i: (bi,) + (0,) * (arr.ndim - 1))
    const = lambda arr, **kw: pl.BlockSpec(arr.shape, lambda bi, i: (0,) * arr.ndim, **kw)
    blk = lambda wd: pl.BlockSpec((1, n_sub * QBLK, wd), lambda bi, i: (bi, i, 0))
    n_steps = b * steps_per_batch
    in_slabs, out_slabs = [], []
    for w in cast_weights:
        _, rows, cols = w.shape
        c = next(c for c in range(1, n_steps + 1)
                 if n_steps % c == 0 and cols % (c * LANES) == 0 and rows % (n_steps // c * BF16_ROWS) == 0)
        r = n_steps // c
        step = lambda bi, i: bi * steps_per_batch + i
        in_slabs.append(pl.BlockSpec((1, rows // r, cols // c),
                                     lambda bi, i, c=c: (0, step(bi, i) // c, step(bi, i) % c)))
        out_slabs.append(pl.BlockSpec((rows // r, cols // c), lambda bi, i, c=c: (step(bi, i) // c, step(bi, i) % c)))
    outs = pl.pallas_call(
        functools.partial(_attn_kernel, seq=s, n_sub=n_sub, n_cast=len(cast_weights)),
        grid=(b, steps_per_batch),
        in_specs=[
            pl.BlockSpec(memory_space=pltpu.SMEM),
            blk(A_WIDTH), blk(B_WIDTH),
            full(ka), full(vat), full(kb), full(vbt), full(kac), full(vact), full(kbc), full(vbct),
            const(ga_t), const(gb_t), const(bias, pipeline_mode=pl.Buffered(1)),
        ] + in_slabs,
        out_specs=[blk(A_WIDTH + B_WIDTH)] + out_slabs,
        out_shape=([jax.ShapeDtypeStruct((b, s, A_WIDTH + B_WIDTH), BF16)]
                   + [jax.ShapeDtypeStruct(w.shape[1:], BF16) for w in cast_weights]),
        compiler_params=pltpu.CompilerParams(dimension_semantics=("parallel", "arbitrary"),
                                             vmem_limit_bytes=VMEM_LIMIT),
        name="attn",
    )(sink, qa, qb, ka, vat, kb, vbt, kac, vact, kbc, vbct, ga_t, gb_t, bias, *cast_weights)
    return outs[0], outs[1:]


def _ffn_kernel(x_ref, o_ref, g1_ref, sh2_ref, sc2_ref, g2_ref, n2_ref, wo_ref, wg_ref, wu_ref, wd_ref,
                out_ref):
    geff = n2_ref[...] * (1.0 + sc2_ref[0])
    tm = x_ref.shape[1]
    sub = min(tm, FFN_SUB_ROWS)

    def stage_rows(r0):
        rows = slice(r0, r0 + sub)
        y = _dot(o_ref[0, rows, :], wo_ref[...])
        yield
        x1 = x_ref[0, rows, :] + g1_ref[0] * y
        ms = jnp.mean(x1 * x1, axis=-1, keepdims=True)
        h2 = ((x1 * lax.rsqrt(ms + EPS)) * geff + sh2_ref[0]).astype(BF16)
        gate = _dot(h2, wg_ref[...])
        up = _dot(h2, wu_ref[...])
        yield
        act = (_silu(gate) * up).astype(BF16)
        down = _dot(act, wd_ref[...])
        yield
        out_ref[0, rows, :] = x1 + g2_ref[0] * down

    live = [stage_rows(r0) for r0 in range(0, tm, sub)]
    while live:
        for gen in list(live):
            if next(gen, StopIteration) is StopIteration:
                live.remove(gen)


def _ffn_call(x, o, mod3, n2, wo, wg, wu, wd, tm):
    b, s, d = x.shape
    hid = wg.shape[1]
    modspec = lambda j: pl.BlockSpec((1, 1, d), lambda bi, i: (bi, 0, j))
    const = lambda shape: pl.BlockSpec(shape, lambda bi, i: (0, 0), pipeline_mode=pl.Buffered(1))
    return pl.pallas_call(
        _ffn_kernel,
        grid=(b, s // tm),
        in_specs=[
            pl.BlockSpec((1, tm, d), lambda bi, i: (bi, i, 0)),
            pl.BlockSpec((1, tm, o.shape[2]), lambda bi, i: (bi, i, 0)),
            modspec(2), modspec(3), modspec(4), modspec(5),
            pl.BlockSpec((1, d), lambda bi, i: (0, 0)),
            const((o.shape[2], d)), const((d, hid)), const((d, hid)), const((hid, d)),
        ],
        out_specs=pl.BlockSpec((1, tm, d), lambda bi, i: (bi, i, 0)),
        out_shape=jax.ShapeDtypeStruct((b, s, d), F32),
        compiler_params=pltpu.CompilerParams(dimension_semantics=("parallel", "arbitrary"),
                                             vmem_limit_bytes=VMEM_LIMIT),
        name="out_ffn",
    )(x, o, mod3, mod3, mod3, mod3, n2, wo, wg, wu, wd)


def _layer(x, ctx, mod, norm1_g, w_in, qn_a, kn_a, sink_a, qn_b, kn_b, rpb_b, on_a, on_b, w_out,
           norm2_g, w_gate, w_up, w_down, rope_tabs):
    b, s, d = x.shape
    rows = s // GRID_W
    w_l, w_c, wva_t, wvb_t = w_in
    scale = HEAD_DIM ** -0.5 * LOG2E
    gains = jnp.tile(jnp.stack([qn_a * scale, kn_a, qn_b * scale, kn_b]), (1, LANES // HEAD_DIM))
    ones_bd = jnp.asarray(np.kron(np.eye(MXU_TILE // HEAD_DIM), np.ones((HEAD_DIM, HEAD_DIM))), BF16)
    g1 = norm1_g[None]

    qka = A_WIDTH + A_KV_WIDTH
    groups_l = [
        (0, qka, [(0, A_WIDTH, True, 0, 0), (A_WIDTH, A_KV_WIDTH, True, 1, 1)]),
        (qka, B_WIDTH, [(0, B_WIDTH, False, 2, 2)]),
        (qka + B_WIDTH, B_WIDTH, [(0, B_WIDTH, False, 3, 3)]),
    ]
    qa, ka, qb, kb, vat, vbt = _inproj_call(
        x, mod, None, g1, w_l, gains, ones_bd, wva_t, wvb_t, rope_tabs, groups_l,
        [A_WIDTH, A_KV_WIDTH, B_WIDTH, B_WIDTH], tm=INPROJ_ROWS)
    groups_c = [(0, A_KV_WIDTH + B_WIDTH, [(0, A_KV_WIDTH, False, 0, 1), (A_KV_WIDTH, B_WIDTH, False, 1, 3)])]
    n_ctx = ctx.shape[1]
    ctx_out = _inproj_call(
        ctx.reshape(1, b * n_ctx, d), mod, b, g1, w_c, gains, ones_bd, wva_t, wvb_t, None, groups_c,
        [A_KV_WIDTH, B_WIDTH], tm=CTX_ROWS)
    kac, kbc = (a.reshape(b, n_ctx, -1) for a in ctx_out[:2])
    vact, vbct = (a.reshape(b, n_ctx // LANES, -1, LANES) for a in ctx_out[2:])

    bias = _bias_call(rpb_b, rows)
    o, ffn_weights = _attn_call(sink_a, qa, qb, ka, vat, kb, vbt, kac, vact, kbc, vbct, bias, on_a, on_b,
                                [w_out, w_gate, w_up, w_down])
    return _ffn_call(x, o, mod, norm2_g[None], *ffn_weights, tm=FFN_ROWS)


def kernel(x, c, ctx, c_ctx, w_mod, b_mod, norm1_g, w_in, qn_a, kn_a, sink_a, qn_b, kn_b, rpb_b, on_a, on_b,
           w_out, norm2_g, w_gate, w_up, w_down):
    b, s, d = x.shape
    depth = w_mod.shape[0]
    assert depth == 1, "the context stream update is only needed when a later layer reads it"
    rope_tabs = _rope_call(s)
    l = 0
    mod, *w_in_layouts, ga_t, gb_t = _mod_call(c, c_ctx[None], w_mod[l], b_mod[l][None], w_in[l],
                                               on_a[l][None], on_b[l][None])
    return _layer(x, ctx, mod, norm1_g[l], w_in_layouts, qn_a[l], kn_a[l], sink_a[l], qn_b[l], kn_b[l], rpb_b[l],
                  ga_t, gb_t, w_out[l:l + 1], norm2_g[l], w_gate[l:l + 1], w_up[l:l + 1], w_down[l:l + 1],
                  rope_tabs)
```
